```python
import math
import jax, jax.numpy as jnp
from jax import lax
import numpy as np

D_MODEL = 1024
BATCH = 32
SEQ = 2048
DEPTH = 2

RMS_EPS = 1e-6
D_FF = -(-8 * D_MODEL // (3 * 256)) * 256
GDN_HEADS = 4
GDN_HEAD_DIM = 128
GDN_WIDTH = GDN_HEADS * GDN_HEAD_DIM
GDN_CONV = 4
GDN_CHUNK = 64
NSA_HEADS = 8
NSA_HEAD_DIM = 64
NSA_GROUPS = 2
NSA_HPG = NSA_HEADS // NSA_GROUPS
NSA_WIDTH = NSA_HEADS * NSA_HEAD_DIM
NSA_KV_WIDTH = NSA_GROUPS * NSA_HEAD_DIM
CMP_LEN = 32
CMP_STRIDE = 16
CMP_HIDDEN = 256
SLC_LEN = 64
N_SELECT = 8
WINDOW = 512
NSA_QBLOCK = SLC_LEN
FORCED_SCORE = 1e6
NEG_INF = -1e30
D_MIX = GDN_WIDTH + NSA_WIDTH
IN_SPLITS = (GDN_WIDTH,) * 4 + (GDN_HEADS,) * 2 + (NSA_WIDTH,) + (NSA_KV_WIDTH,) * 6 + (3 * NSA_HEADS,)
IN_COLS = sum(IN_SPLITS)
NUM_BUCKETS = 32
REL_MAX_DIST = 128
RWKV_HEAD = 64
RWKV_HEADS = D_MODEL // RWKV_HEAD
DECAY_LORA = 64
AAA_LORA = 64
GATE_LORA = 160
RWKV_GN_EPS = 64e-5

kernel_name = 'hybrid_gdn_nsa_rwkv7_trunk'


def rmsnorm(x, w):
    xf = x.astype(jnp.float32)
    return xf * lax.rsqrt(jnp.mean(xf * xf, axis=-1, keepdims=True) + RMS_EPS) * w.astype(jnp.float32)


def l2norm(x, eps=1e-6):
    xf = x.astype(jnp.float32)
    return xf * lax.rsqrt(jnp.sum(xf * xf, axis=-1, keepdims=True) + eps)


def masked_softmax(s, mask):
    s = jnp.where(mask, s.astype(jnp.float32), NEG_INF)
    p = jnp.where(mask, jnp.exp(s - jnp.max(s, axis=-1, keepdims=True)), 0.0)
    return p / jnp.maximum(jnp.sum(p, axis=-1, keepdims=True), 1e-30)


def causal_depthwise_conv(x, w):
    k = w.shape[0]
    return lax.conv_general_dilated(x, w[:, None, :].astype(x.dtype), (1,), [(k - 1, 0)],
                                    dimension_numbers=('NWC', 'WIO', 'NWC'),
                                    feature_group_count=x.shape[-1])


def swiglu(h, w_gate, w_up, w_down):
    return (jax.nn.silu(h @ w_gate) * (h @ w_up)) @ w_down


def rel_bucket(dist):
    n = jnp.maximum(dist, 0)
    exact = NUM_BUCKETS // 2
    nf = jnp.maximum(n, 1).astype(jnp.float32)
    large = exact + (jnp.log(nf / exact) / math.log(REL_MAX_DIST / exact) * (NUM_BUCKETS - exact)).astype(jnp.int32)
    return jnp.where(n < exact, n, jnp.minimum(large, NUM_BUCKETS - 1))


def chunk_gated_delta_rule(q, k, v, g, beta):
    B, S, H, DK = q.shape
    DV = v.shape[-1]
    C = GDN_CHUNK
    NCH = S // C

    def chunks(t):
        t = t.astype(jnp.float32).reshape((B, NCH, C, H) + t.shape[3:])
        return jnp.moveaxis(t, 3, 1)

    qc = chunks(q) * DK ** -0.5
    kc, vc, bc = chunks(k), chunks(v), chunks(beta)
    gc = jnp.cumsum(chunks(g), axis=-1)
    idx = jnp.arange(C)
    causal = idx[:, None] >= idx[None, :]
    decay = jnp.exp(jnp.where(causal, gc[..., :, None] - gc[..., None, :], -jnp.inf))
    kb = kc * bc[..., None]
    strict = jnp.where(idx[:, None] > idx[None, :], jnp.einsum('bhncd,bhnsd->bhncs', kb, kc) * decay, 0.0)
    tmat = strict + jnp.eye(C, dtype=jnp.float32)
    u = lax.linalg.triangular_solve(tmat, vc * bc[..., None], left_side=True, lower=True, unit_diagonal=True)
    w = lax.linalg.triangular_solve(tmat, kb * jnp.exp(gc)[..., None], left_side=True, lower=True, unit_diagonal=True)
    qk = jnp.einsum('bhncd,bhnsd->bhncs', qc, kc) * decay

    def step(state, inp):
        q_i, k_i, u_i, w_i, qk_i, g_i = inp
        v_new = u_i - jnp.einsum('bhcd,bhde->bhce', w_i, state)
        o = (jnp.einsum('bhcd,bhde->bhce', q_i * jnp.exp(g_i)[..., None], state)
             + jnp.einsum('bhcs,bhse->bhce', qk_i, v_new))
        g_last = g_i[..., -1:]
        state = (state * jnp.exp(g_last)[..., None]
                 + jnp.einsum('bhcd,bhce->bhde', k_i * jnp.exp(g_last - g_i)[..., None], v_new))
        return state, o

    xs = tuple(jnp.moveaxis(t, 2, 0) for t in (qc, kc, u, w, qk, gc))
    _, o = lax.scan(step, jnp.zeros((B, H, DK, DV), jnp.float32), xs)
    return o.transpose(1, 0, 3, 2, 4).reshape(B, S, H, DV)


def gated_deltanet(q, k, v, z, b, a, conv_w, a_log, dt_bias, norm_w):
    B, S, _ = q.shape
    qkv = jax.nn.silu(causal_depthwise_conv(jnp.concatenate([q, k, v], axis=-1), conv_w))
    q, k, v = jnp.split(qkv, 3, axis=-1)
    q = l2norm(q.reshape(B, S, GDN_HEADS, GDN_HEAD_DIM))
    k = l2norm(k.reshape(B, S, GDN_HEADS, GDN_HEAD_DIM))
    v = v.reshape(B, S, GDN_HEADS, GDN_HEAD_DIM).astype(jnp.float32)
    beta = jax.nn.sigmoid(b.astype(jnp.float32))
    g = -jnp.exp(a_log.astype(jnp.float32)) * jax.nn.softplus(a.astype(jnp.float32) + dt_bias.astype(jnp.float32))
    o = chunk_gated_delta_rule(q, k, v, g, beta)
    o = rmsnorm(o, norm_w) * jax.nn.silu(z.reshape(B, S, GDN_HEADS, GDN_HEAD_DIM).astype(jnp.float32))
    return o.reshape(B, S, GDN_WIDTH)


def compress_blocks(x, pe, w1, w2):
    B, S, G, DH = x.shape
    r = CMP_LEN // CMP_STRIDE
    ch = x.astype(jnp.float32).reshape(B, S // CMP_STRIDE, CMP_STRIDE, G, DH)
    nc = S // CMP_STRIDE - r + 1
    blocks = jnp.concatenate([ch[:, i:i + nc] for i in range(r)], axis=2)
    blocks = blocks + pe[None, None, :, None, :]
    flat = blocks.transpose(0, 1, 3, 2, 4).reshape(B, nc, G, CMP_LEN * DH)
    return jax.nn.silu(flat @ w1) @ w2


def nsa_attention(q, k_cmp, v_cmp, k_slc, v_slc, k_win, v_win, gate_logits,
                  cmpk_pe, cmpk_w1, cmpk_w2, cmpv_pe, cmpv_w1, cmpv_w2, rel_bias):
    B, S = q.shape[:2]
    G, HPG, DH, QB = NSA_GROUPS, NSA_HPG, NSA_HEAD_DIM, NSA_QBLOCK
    q = (q.astype(jnp.float32) * DH ** -0.5).reshape(B, S, G, HPG, DH)
    gates = jax.nn.sigmoid(gate_logits.astype(jnp.float32)).reshape(B, S, G, HPG, 3)
    kc = compress_blocks(k_cmp, cmpk_pe, cmpk_w1, cmpk_w2)
    vc = compress_blocks(v_cmp, cmpv_pe, cmpv_w1, cmpv_w2)
    NC = kc.shape[1]
    NS = S // SLC_LEN
    n_sel = min(N_SELECT, NS)
    k_blk = k_slc.astype(jnp.float32).reshape(B, NS, SLC_LEN, G, DH).transpose(0, 3, 1, 2, 4)
    v_blk = v_slc.astype(jnp.float32).reshape(B, NS, SLC_LEN, G, DH).transpose(0, 3, 1, 2, 4)
    k_wpad = jnp.pad(k_win.astype(jnp.float32), ((0, 0), (WINDOW, 0), (0, 0), (0, 0)))
    v_wpad = jnp.pad(v_win.astype(jnp.float32), ((0, 0), (WINDOW, 0), (0, 0), (0, 0)))
    cmp_start = jnp.arange(NC) * CMP_STRIDE
    cmp_end = cmp_start + CMP_LEN - 1
    slc_j = jnp.arange(NS)
    overlap = ((cmp_start[:, None] < (slc_j[None, :] + 1) * SLC_LEN)
               & (cmp_end[:, None] >= slc_j[None, :] * SLC_LEN)).astype(jnp.float32)
    table = rel_bias.astype(jnp.float32)
    table_grouped = table.reshape(NUM_BUCKETS, G, HPG).transpose(1, 0, 2).reshape(G * NUM_BUCKETS, HPG)
    bi = jnp.arange(B)[:, None, None, None]
    gi = jnp.arange(G)[None, None, :, None]

    def head_bias(dist):
        return table[rel_bucket(dist)].reshape(dist.shape + (G, HPG)).transpose(0, 2, 3, 1)

    def block(n):
        t0 = n * QB
        t = t0 + jnp.arange(QB)
        qb = lax.dynamic_slice_in_dim(q, t0, QB, axis=1)
        gb = lax.dynamic_slice_in_dim(gates, t0, QB, axis=1)
        dist_c = t[:, None] - cmp_end[None, :]
        s_c = jnp.einsum('btghd,bcgd->btghc', qb, kc) + head_bias(dist_c)[None]
        p_c = masked_softmax(s_c, (dist_c >= 0)[None, :, None, None, :])
        o_c = jnp.einsum('btghc,bcgd->btghd', p_c, vc)
        imp = jnp.einsum('btghc,cs->btgs', p_c, overlap)
        forced = (slc_j == 0) | (slc_j == n) | (slc_j == n - 1)
        imp = jnp.where(slc_j > n, -1.0, jnp.where(forced, FORCED_SCORE, imp))
        _, idx = lax.top_k(imp, n_sel)
        k_sel = k_blk[bi, gi, idx]
        v_sel = v_blk[bi, gi, idx]
        pos = idx[..., None] * SLC_LEN + jnp.arange(SLC_LEN)
        dist_s = t[None, :, None, None, None] - pos
        bias_s = table_grouped[gi[..., None] * NUM_BUCKETS + rel_bucket(dist_s)]
        s_s = jnp.einsum('btghd,btgnld->btghnl', qb, k_sel) + bias_s.transpose(0, 1, 2, 5, 3, 4)
        mask_s = (dist_s >= 0)[:, :, :, None].reshape(B, QB, G, 1, n_sel * SLC_LEN)
        p_s = masked_softmax(s_s.reshape(B, QB, G, HPG, n_sel * SLC_LEN), mask_s).reshape(s_s.shape)
        o_s = jnp.einsum('btghnl,btgnld->btghd', p_s, v_sel)
        k_w = lax.dynamic_slice_in_dim(k_wpad, t0, WINDOW + QB, axis=1)
        v_w = lax.dynamic_slice_in_dim(v_wpad, t0, WINDOW + QB, axis=1)
        kpos = t0 - WINDOW + jnp.arange(WINDOW + QB)
        dist_w = t[:, None] - kpos[None, :]
        mask_w = (dist_w >= 0) & (dist_w < WINDOW) & (kpos >= 0)[None, :]
        s_w = jnp.einsum('btghd,bkgd->btghk', qb, k_w) + head_bias(dist_w)[None]
        p_w = masked_softmax(s_w, mask_w[None, :, None, None, :])
        o_w = jnp.einsum('btghk,bkgd->btghd', p_w, v_w)
        return gb[..., 0:1] * o_c + gb[..., 1:2] * o_s + gb[..., 2:3] * o_w

    out = lax.map(block, jnp.arange(S // QB))
    return out.transpose(1, 0, 2, 3, 4, 5).reshape(B, S, NSA_WIDTH)


def delta_nsa_mixer(h, w_in, conv_w, a_log, dt_bias, gdn_norm, cmpk_pe, cmpk_w1, cmpk_w2,
                    cmpv_pe, cmpv_w1, cmpv_w2, w_out, rel_bias):
    B, S, _ = h.shape
    cuts = [int(c) for c in np.cumsum(IN_SPLITS)[:-1]]
    (gq, gk, gv, gz, gb, ga, nq, nkc, nvc, nks, nvs, nkw, nvw, ngate) = jnp.split(h @ w_in, cuts, axis=-1)
    o_gdn = gated_deltanet(gq, gk, gv, gz, gb, ga, conv_w, a_log, dt_bias, gdn_norm)
    kv = lambda t: t.reshape(B, S, NSA_GROUPS, NSA_HEAD_DIM)
    o_nsa = nsa_attention(nq.reshape(B, S, NSA_HEADS, NSA_HEAD_DIM), kv(nkc), kv(nvc), kv(nks), kv(nvs),
                          kv(nkw), kv(nvw), ngate.reshape(B, S, NSA_HEADS, 3),
                          cmpk_pe, cmpk_w1, cmpk_w2, cmpv_pe, cmpv_w1, cmpv_w2, rel_bias)
    return jnp.concatenate([o_gdn, o_nsa], axis=-1) @ w_out


def wkv7_scan(r, w, k, v, a, b):
    B, S, H, N = r.shape

    def step(state, inp):
        r_t, w_t, k_t, v_t, a_t, b_t = inp
        sa = jnp.einsum('bhij,bhj->bhi', state, a_t)
        state = state * w_t[:, :, None, :] + sa[..., None] * b_t[:, :, None, :] + v_t[..., None] * k_t[:, :, None, :]
        return state, jnp.einsum('bhij,bhj->bhi', state, r_t)

    xs = tuple(jnp.moveaxis(t, 1, 0) for t in (r, w, k, v, a, b))
    _, y = lax.scan(step, jnp.zeros((B, H, N, N), jnp.float32), xs)
    return jnp.moveaxis(y, 0, 1)


def rwkv7_time_mix(h, mu, w_r, w_k, w_v, w_o, w0, w1, w2, a0, a1, a2, g1, g2, k_k, k_a, r_k, lnx_w, lnx_b):
    B, S, D = h.shape
    heads = lambda t: t.reshape(B, S, RWKV_HEADS, RWKV_HEAD).astype(jnp.float32)
    xx = jnp.pad(h, ((0, 0), (1, 0), (0, 0)))[:, :-1] - h
    xr, xw, xk, xv, xa, xg = (h + xx * mu[j] for j in range(6))
    r = xr @ w_r
    w = -jax.nn.softplus(-(w0 + jnp.tanh(xw @ w1) @ w2)) - 0.5
    k = xk @ w_k
    v = xv @ w_v
    a = jax.nn.sigmoid(a0 + (xa @ a1) @ a2)
    g = jax.nn.sigmoid(xg @ g1) @ g2
    kk = l2norm(heads(k * k_k), eps=1e-12)
    k = k * (1.0 + (a - 1.0) * k_a)
    rh, kh, vh = heads(r), heads(k), heads(v)
    y = wkv7_scan(rh, heads(jnp.exp(-jnp.exp(w))), kh, vh, -kk, kk * heads(a))
    mean = jnp.mean(y, axis=-1, keepdims=True)
    var = jnp.mean(jnp.square(y - mean), axis=-1, keepdims=True)
    y = ((y - mean) * lax.rsqrt(var + RWKV_GN_EPS)).reshape(B, S, D) * lnx_w + lnx_b
    y = y + (jnp.sum(rh * kh * r_k, axis=-1, keepdims=True) * vh).reshape(B, S, D)
    return (y * g) @ w_o


def setup_inputs(seed: int = 0) -> dict:
    key = jax.random.key(seed)
    subkeys = jax.random.split(key, 64)
    counter = iter(range(64))
    nk = lambda: subkeys[next(counter)]
    nrm = lambda shape, scale: scale * jax.random.normal(nk(), shape, jnp.float32)
    gain = lambda shape: 1.0 + nrm(shape, 0.02)
    ne, no = (DEPTH + 1) // 2, DEPTH // 2
    d = D_MODEL
    x = nrm((BATCH, SEQ, d), 1.0)
    rel_bias = nrm((NUM_BUCKETS, NSA_HEADS), 0.5)
    final_norm = gain((d,))
    e_attn_norm = gain((ne, d))
    e_w_in = nrm((ne, d, IN_COLS), d ** -0.5)
    e_conv_w = nrm((ne, GDN_CONV, 3 * GDN_WIDTH), GDN_CONV ** -0.5)
    e_a_log = jnp.log(jax.random.uniform(nk(), (ne, GDN_HEADS), jnp.float32, 1.0, 16.0))
    dt = jnp.exp(jax.random.uniform(nk(), (ne, GDN_HEADS), jnp.float32, math.log(1e-3), math.log(1e-1)))
    e_dt_bias = dt + jnp.log(-jnp.expm1(-dt))
    e_gdn_norm = gain((ne, GDN_HEAD_DIM))
    e_cmpk_pe = nrm((ne, CMP_LEN, NSA_HEAD_DIM), 0.02)
    e_cmpk_w1 = nrm((ne, CMP_LEN * NSA_HEAD_DIM, CMP_HIDDEN), (CMP_LEN * NSA_HEAD_DIM) ** -0.5)
    e_cmpk_w2 = nrm((ne, CMP_HIDDEN, NSA_HEAD_DIM), CMP_HIDDEN ** -0.5)
    e_cmpv_pe = nrm((ne, CMP_LEN, NSA_HEAD_DIM), 0.02)
    e_cmpv_w1 = nrm((ne, CMP_LEN * NSA_HEAD_DIM, CMP_HIDDEN), (CMP_LEN * NSA_HEAD_DIM) ** -0.5)
    e_cmpv_w2 = nrm((ne, CMP_HIDDEN, NSA_HEAD_DIM), CMP_HIDDEN ** -0.5)
    e_w_out = nrm((ne, D_MIX, d), D_MIX ** -0.5)
    e_ffn_norm = gain((ne, d))
    e_ffn_gate = nrm((ne, d, D_FF), d ** -0.5)
    e_ffn_up = nrm((ne, d, D_FF), d ** -0.5)
    e_ffn_down = nrm((ne, D_FF, d), D_FF ** -0.5)
    o_attn_norm = gain((no, d))
    o_mu = jax.random.uniform(nk(), (no, 6, d), jnp.float32)
    o_w_r = nrm((no, d, d), d ** -0.5)
    o_w_k = nrm((no, d, d), d ** -0.5)
    o_w_v = nrm((no, d, d), d ** -0.5)
    o_w_o = nrm((no, d, d), d ** -0.5)
    o_w0 = jax.random.uniform(nk(), (no, d), jnp.float32, -6.0, -1.0)
    o_w1 = nrm((no, d, DECAY_LORA), d ** -0.5)
    o_w2 = nrm((no, DECAY_LORA, d), 0.1 * DECAY_LORA ** -0.5)
    o_a0 = nrm((no, d), 0.1)
    o_a1 = nrm((no, d, AAA_LORA), d ** -0.5)
    o_a2 = nrm((no, AAA_LORA, d), 0.5 * AAA_LORA ** -0.5)
    o_g1 = nrm((no, d, GATE_LORA), d ** -0.5)
    o_g2 = nrm((no, GATE_LORA, d), GATE_LORA ** -0.5)
    o_k_k = 0.85 + nrm((no, d), 0.02)
    o_k_a = gain((no, d))
    o_r_k = nrm((no, RWKV_HEADS, RWKV_HEAD), 0.1)
    o_lnx_w = gain((no, d))
    o_lnx_b = nrm((no, d), 0.01)
    o_ffn_norm = gain((no, d))
    o_ffn_gate = nrm((no, d, D_FF), d ** -0.5)
    o_ffn_up = nrm((no, d, D_FF), d ** -0.5)
    o_ffn_down = nrm((no, D_FF, d), D_FF ** -0.5)
    return {'x': x, 'rel_bias': rel_bias, 'final_norm': final_norm,
            'e_attn_norm': e_attn_norm, 'e_w_in': e_w_in, 'e_conv_w': e_conv_w, 'e_a_log': e_a_log,
            'e_dt_bias': e_dt_bias, 'e_gdn_norm': e_gdn_norm,
            'e_cmpk_pe': e_cmpk_pe, 'e_cmpk_w1': e_cmpk_w1, 'e_cmpk_w2': e_cmpk_w2,
            'e_cmpv_pe': e_cmpv_pe, 'e_cmpv_w1': e_cmpv_w1, 'e_cmpv_w2': e_cmpv_w2,
            'e_w_out': e_w_out, 'e_ffn_norm': e_ffn_norm, 'e_ffn_gate': e_ffn_gate,
            'e_ffn_up': e_ffn_up, 'e_ffn_down': e_ffn_down,
            'o_attn_norm': o_attn_norm, 'o_mu': o_mu, 'o_w_r': o_w_r, 'o_w_k': o_w_k, 'o_w_v': o_w_v,
            'o_w_o': o_w_o, 'o_w0': o_w0, 'o_w1': o_w1, 'o_w2': o_w2, 'o_a0': o_a0, 'o_a1': o_a1,
            'o_a2': o_a2, 'o_g1': o_g1, 'o_g2': o_g2, 'o_k_k': o_k_k, 'o_k_a': o_k_a, 'o_r_k': o_r_k,
            'o_lnx_w': o_lnx_w, 'o_lnx_b': o_lnx_b, 'o_ffn_norm': o_ffn_norm, 'o_ffn_gate': o_ffn_gate,
            'o_ffn_up': o_ffn_up, 'o_ffn_down': o_ffn_down}


def reference(x, rel_bias, final_norm,
              e_attn_norm, e_w_in, e_conv_w, e_a_log, e_dt_bias, e_gdn_norm,
              e_cmpk_pe, e_cmpk_w1, e_cmpk_w2, e_cmpv_pe, e_cmpv_w1, e_cmpv_w2,
              e_w_out, e_ffn_norm, e_ffn_gate, e_ffn_up, e_ffn_down,
              o_attn_norm, o_mu, o_w_r, o_w_k, o_w_v, o_w_o, o_w0, o_w1, o_w2, o_a0, o_a1, o_a2,
              o_g1, o_g2, o_k_k, o_k_a, o_r_k, o_lnx_w, o_lnx_b,
              o_ffn_norm, o_ffn_gate, o_ffn_up, o_ffn_down):
    for layer in range(DEPTH):
        i = layer // 2
        if layer % 2 == 0:
            h = rmsnorm(x, e_attn_norm[i])
            x = x + delta_nsa_mixer(h, e_w_in[i], e_conv_w[i], e_a_log[i], e_dt_bias[i], e_gdn_norm[i],
                                    e_cmpk_pe[i], e_cmpk_w1[i], e_cmpk_w2[i],
                                    e_cmpv_pe[i], e_cmpv_w1[i], e_cmpv_w2[i], e_w_out[i], rel_bias)
            x = x + swiglu(rmsnorm(x, e_ffn_norm[i]), e_ffn_gate[i], e_ffn_up[i], e_ffn_down[i])
        else:
            h = rmsnorm(x, o_attn_norm[i])
            x = x + rwkv7_time_mix(h, o_mu[i], o_w_r[i], o_w_k[i], o_w_v[i], o_w_o[i], o_w0[i], o_w1[i],
                                   o_w2[i], o_a0[i], o_a1[i], o_a2[i], o_g1[i], o_g2[i], o_k_k[i],
                                   o_k_a[i], o_r_k[i], o_lnx_w[i], o_lnx_b[i])
            x = x + swiglu(rmsnorm(x, o_ffn_norm[i]), o_ffn_gate[i], o_ffn_up[i], o_ffn_down[i])
    return rmsnorm(x, final_norm)
```

```python
import functools
import math

import jax
import jax.numpy as jnp
from jax import lax
from jax.experimental import pallas as pl
from jax.experimental.pallas import tpu as pltpu

F32 = jnp.float32
BF16 = jnp.bfloat16

VMEM_LIMIT_BYTES = 52 * 1024 * 1024
LANES = 128
CHUNK = 64

RMS_EPS = 1e-6
GDN_HEADS = 4
GDN_CONV = 4
NSA_HEADS = 8
NSA_GROUPS = 2
NSA_HPG = 4
NSA_DH = 64
CMP_LEN = 32
CMP_STRIDE = 16
SLC_LEN = 64
N_SELECT = 8
WINDOW = 512
NUM_BUCKETS = 32
REL_MAX_DIST = 128
NEG = -1e30
KV_PAD = 640
WIN_TILE = 640
RWKV_GN_EPS = 64e-5


def _cparams(sem):
    return pltpu.CompilerParams(dimension_semantics=sem, vmem_limit_bytes=VMEM_LIMIT_BYTES)


def _dot(a, b):
    return jnp.dot(a, b, preferred_element_type=F32)


def _dot_nt(a, b):
    return lax.dot_general(a, b, (((1,), (1,)), ((), ())), preferred_element_type=F32)


def _split(x, terms):
    out = []
    rem = x
    for _ in range(terms - 1):
        p = rem.astype(BF16)
        out.append(p)
        rem = rem - p.astype(F32)
    out.append(rem.astype(BF16))
    return out


def _mm(a, b, passes=1, nt=False):
    d = _dot_nt if nt else _dot
    if passes == 1:
        return d(a.astype(BF16), b.astype(BF16))
    ah, al = _split(a, 2)
    bh, bl = _split(b, 2)
    return d(ah, bh) + (d(ah, bl) + d(al, bh))


def _mm_exact_lhs(a01, b):
    a = a01.astype(BF16)
    b0, b1, b2 = _split(b, 3)
    return _dot(a, b0) + (_dot(a, b1) + _dot(a, b2))


def _mm_exact_rhs(a, b01):
    b = b01.astype(BF16)
    a0, a1, a2 = _split(a, 3)
    return _dot(a0, b) + (_dot(a1, b) + _dot(a2, b))


def _silu(x):
    return x * jax.nn.sigmoid(x)


def _softplus(x):
    return jnp.maximum(x, 0.0) + jnp.log1p(jnp.exp(-jnp.abs(x)))


def _lane_masks():
    lane = lax.broadcasted_iota(jnp.int32, (1, LANES), 1)
    m0 = (lane < 64).astype(F32)
    return m0, 1.0 - m0


def _stack2(x, m0, m1):
    return jnp.concatenate([x * m0, x * m1], axis=0)


def _tri_inv_cat(lcat, passes):
    m0, m1 = _lane_masks()
    i = lax.broadcasted_iota(jnp.int32, (CHUNK, LANES), 0)
    j = lax.broadcasted_iota(jnp.int32, (CHUNK, LANES), 1) & (CHUNK - 1)
    eye = (i == j).astype(F32)
    blk16 = (i >> 4) == (j >> 4)
    blk32 = (i >> 5) == (j >> 5)
    st = lambda x: _stack2(x, m0, m1)
    mm = lambda a, b: _mm(a, b, passes)
    ld = jnp.where(blk16, lcat, 0.0)
    n = eye - ld
    l2 = mm(ld, st(ld))
    n = n + mm(n, st(l2))
    l4 = mm(l2, st(l2))
    n = n + mm(n, st(l4))
    l8 = mm(l4, st(l4))
    n = n + mm(n, st(l8))
    off1 = jnp.where(jnp.logical_and(blk32, jnp.logical_not(blk16)), lcat, 0.0)
    n = n - mm(mm(n, st(off1)), st(n))
    off2 = jnp.where(blk32, 0.0, lcat)
    n = n - mm(mm(n, st(off2)), st(n))
    return n


def _tri_ones(n, lower, block=None):
    i = lax.broadcasted_iota(jnp.int32, (n, n), 0)
    j = lax.broadcasted_iota(jnp.int32, (n, n), 1)
    m = (i >= j) if lower else (i <= j)
    if block is not None:
        sh = block.bit_length() - 1
        m = jnp.logical_and(m, (i >> sh) == (j >> sh))
    return m.astype(F32)


def _norm_linear_kernel(x_ref, nw_ref, *refs, n_out):
    w_refs, o_refs = refs[:n_out], refs[n_out:]
    x = x_ref[...]
    h = x * lax.rsqrt(jnp.mean(x * x, axis=-1, keepdims=True) + RMS_EPS) * nw_ref[...]
    hb = h.astype(BF16)
    for w_ref, o_ref in zip(w_refs, o_refs):
        o_ref[...] = _dot(hb, w_ref[...]).astype(o_ref.dtype)


def norm_linear(x, nw, weights, out_dtypes, tm=512):
    m, k = x.shape
    n_out = len(weights)
    in_specs = [pl.BlockSpec((tm, k), lambda i: (i, 0)), pl.BlockSpec((1, k), lambda i: (0, 0))]
    in_specs += [pl.BlockSpec(w.shape, lambda i: (0, 0)) for w in weights]
    out_specs = [pl.BlockSpec((tm, w.shape[1]), lambda i: (i, 0)) for w in weights]
    out_shape = [jax.ShapeDtypeStruct((m, w.shape[1]), dt) for w, dt in zip(weights, out_dtypes)]
    return pl.pallas_call(
        functools.partial(_norm_linear_kernel, n_out=n_out),
        grid=(m // tm,), in_specs=in_specs, out_specs=out_specs, out_shape=out_shape,
        compiler_params=_cparams(("parallel",)), name="norm_linear",
    )(x, nw.reshape(1, k), *weights)


def _linear_res_kernel(res_ref, *refs, n_in):
    a_refs, w_refs, o_ref = refs[:n_in], refs[n_in:2 * n_in], refs[2 * n_in]
    acc = res_ref[...]
    for a_ref, w_ref in zip(a_refs, w_refs):
        acc = acc + _dot(a_ref[...].astype(BF16), w_ref[...])
    o_ref[...] = acc


def linear_res(res, acts, weights, tm=512):
    m, n = res.shape
    n_in = len(acts)
    in_specs = [pl.BlockSpec((tm, n), lambda i: (i, 0))]
    in_specs += [pl.BlockSpec((tm, a.shape[1]), lambda i: (i, 0)) for a in acts]
    in_specs += [pl.BlockSpec(w.shape, lambda i: (0, 0)) for w in weights]
    return pl.pallas_call(
        functools.partial(_linear_res_kernel, n_in=n_in),
        grid=(m // tm,), in_specs=in_specs, out_specs=pl.BlockSpec((tm, n), lambda i: (i, 0)),
        out_shape=jax.ShapeDtypeStruct((m, n), F32),
        compiler_params=_cparams(("parallel",)), name="linear_res",
    )(res, *acts, *weights)


def _ffn_kernel(x_ref, nw_ref, wg_ref, wu_ref, wd_ref, fw_ref, o_ref, xn_scr, acc_scr, *, final_norm):
    f = pl.program_id(1)

    @pl.when(f == 0)
    def _():
        x = x_ref[...]
        h = x * lax.rsqrt(jnp.mean(x * x, axis=-1, keepdims=True) + RMS_EPS) * nw_ref[...]
        xn_scr[...] = h.astype(BF16)
        acc_scr[...] = jnp.zeros_like(acc_scr)

    xn = xn_scr[...]
    g = _dot(xn, wg_ref[...])
    u = _dot(xn, wu_ref[...])
    hid = (_silu(g) * u).astype(BF16)
    acc_scr[...] += _dot(hid, wd_ref[...])

    @pl.when(f == pl.num_programs(1) - 1)
    def _():
        y = x_ref[...] + acc_scr[...]
        if final_norm:
            y = y * lax.rsqrt(jnp.mean(y * y, axis=-1, keepdims=True) + RMS_EPS) * fw_ref[...]
        o_ref[...] = y


def ffn(x, nw, wg, wu, wd, fw, final_norm, tm=512, tf=1408):
    m, d = x.shape
    dff = wg.shape[1]
    return pl.pallas_call(
        functools.partial(_ffn_kernel, final_norm=final_norm),
        grid=(m // tm, dff // tf),
        in_specs=[pl.BlockSpec((tm, d), lambda i, f: (i, 0)),
                  pl.BlockSpec((1, d), lambda i, f: (0, 0)),
                  pl.BlockSpec((d, tf), lambda i, f: (0, f)),
                  pl.BlockSpec((d, tf), lambda i, f: (0, f)),
                  pl.BlockSpec((tf, d), lambda i, f: (f, 0)),
                  pl.BlockSpec((1, d), lambda i, f: (0, 0))],
        out_specs=pl.BlockSpec((tm, d), lambda i, f: (i, 0)),
        out_shape=jax.ShapeDtypeStruct((m, d), F32),
        scratch_shapes=[pltpu.VMEM((tm, d), BF16), pltpu.VMEM((tm, d), F32)],
        compiler_params=_cparams(("parallel", "arbitrary")), name="ffn",
    )(x, nw.reshape(1, d), wg, wu, wd, fw.reshape(1, d))


def _gdn_kernel(q_ref, k_ref, v_ref, qh_ref, kh_ref, vh_ref, cwq_ref, cwk_ref, cwv_ref,
                z_ref, ba_ref, bat_ref, hp_ref, nw_ref, o_ref, scr, state, *, tc, inv_passes):
    h = pl.program_id(1)
    t = pl.program_id(2)

    @pl.when(t == 0)
    def _():
        state[...] = jnp.zeros_like(state)

    not_first = (t > 0).astype(F32)

    def conv(x_ref, halo_ref, w_ref, slot):
        scr[slot, 0:8, :] = halo_ref[0] * not_first
        scr[slot, 8:8 + tc, :] = x_ref[0]
        w = w_ref[...]
        y = (w[3:4] * scr[slot, 8:8 + tc, :] + w[2:3] * scr[slot, 7:7 + tc, :]
             + w[1:2] * scr[slot, 6:6 + tc, :] + w[0:1] * scr[slot, 5:5 + tc, :])
        return _silu(y)

    q = conv(q_ref, qh_ref, cwq_ref, 0)
    k = conv(k_ref, kh_ref, cwk_ref, 1)
    v = conv(v_ref, vh_ref, cwv_ref, 2)
    qn = q * lax.rsqrt(jnp.sum(q * q, axis=-1, keepdims=True) + 1e-6)
    kn = k * lax.rsqrt(jnp.sum(k * k, axis=-1, keepdims=True) + 1e-6)

    lane = lax.broadcasted_iota(jnp.int32, (1, LANES), 1)
    ba = ba_ref[0]
    b_col = jnp.sum(jnp.where(lane == h, ba, 0.0), axis=-1, keepdims=True)
    a_col = jnp.sum(jnp.where(lane == h + GDN_HEADS, ba, 0.0), axis=-1, keepdims=True)
    beta = jax.nn.sigmoid(b_col)
    a_log = hp_ref[0, 0:1, :]
    dt_b = hp_ref[0, 1:2, :]
    g_rep = -jnp.exp(a_log) * _softplus(a_col + dt_b)

    m0, m1 = _lane_masks()
    i64 = lax.broadcasted_iota(jnp.int32, (CHUNK, LANES), 0)
    j64 = lax.broadcasted_iota(jnp.int32, (CHUNK, LANES), 1) & (CHUNK - 1)
    low_half = lax.broadcasted_iota(jnp.int32, (CHUNK, LANES), 1) < CHUNK
    ltri_bd = _tri_ones(2 * CHUNK, True, CHUNK)
    utri_bd = _tri_ones(2 * CHUNK, False, CHUNK)
    eye128 = (lax.broadcasted_iota(jnp.int32, (LANES, LANES), 0)
              == lax.broadcasted_iota(jnp.int32, (LANES, LANES), 1)).astype(F32)
    zeros64 = jnp.zeros((CHUNK, LANES), F32)
    scale = 1.0 / math.sqrt(LANES)
    nw = nw_ref[...]

    s_state = state[...]
    for p in range(tc // (2 * CHUNK)):
        r0 = 2 * CHUNK * p
        rows = slice(r0, r0 + 2 * CHUNK)
        kp, qp, vp = kn[rows], qn[rows] * scale, v[rows]
        beta_p = beta[rows]
        gc_col = _mm_exact_lhs(ltri_bd, g_rep[rows])
        a_row = bat_ref[0, 0, p, 1:2, :]
        g_row = -jnp.exp(a_log[:, 0:1]) * _softplus(a_row + dt_b[:, 0:1])
        gc_row = _mm_exact_rhs(jnp.broadcast_to(g_row, (CHUNK, LANES)), utri_bd)
        gc_cat = jnp.where(low_half, gc_col[0:CHUNK], gc_col[CHUNK:2 * CHUNK])
        dcat = jnp.exp(jnp.where(i64 >= j64, gc_cat - gc_row, -jnp.inf))
        kb = kp * beta_p
        gram = _mm(kb, kp, nt=True)
        a_cat = jnp.where(low_half, gram[0:CHUNK], gram[CHUNK:2 * CHUNK])
        strict = jnp.where(i64 > j64, a_cat * dcat, 0.0)
        tinv_bd = _stack2(_tri_inv_cat(strict, inv_passes), m0, m1)
        eg = jnp.exp(gc_col)
        rhs = jnp.concatenate([vp * beta_p, kb * eg], axis=1)
        uw = _mm(tinv_bd, rhs)
        u, w = uw[:, 0:LANES], uw[:, LANES:2 * LANES]
        gq = _mm(qp, kp, nt=True)
        qk_cat = jnp.where(i64 >= j64, jnp.where(low_half, gq[0:CHUNK], gq[CHUNK:2 * CHUNK]) * dcat, 0.0)
        qk_bd = _stack2(qk_cat, m0, m1)
        q_hat = qp * eg - _mm(qk_bd, w)
        o_hat = _mm(qk_bd, u)
        g_last = [gc_col[CHUNK * (c + 1) - 1:CHUNK * (c + 1), :] for c in range(2)]
        kd = jnp.concatenate([kp[CHUNK * c:CHUNK * (c + 1)] * jnp.exp(g_last[c] - gc_col[CHUNK * c:CHUNK * (c + 1)])
                              for c in range(2)], axis=0)
        kdt = kd.T
        for c in range(2):
            cr = slice(CHUNK * c, CHUNK * (c + 1))
            wz = jnp.concatenate([w[cr], zeros64] if c == 0 else [zeros64, w[cr]], axis=0)
            uz = jnp.concatenate([u[cr], zeros64] if c == 0 else [zeros64, u[cr]], axis=0)
            trans = eye128 * jnp.exp(g_last[c]) - _mm(kdt, wz)
            h_add = _mm(kdt, uz)
            o = _mm(q_hat[cr], s_state) + o_hat[cr]
            s_state = _mm(trans, s_state, passes=3) + h_add
            on = o * lax.rsqrt(jnp.mean(o * o, axis=-1, keepdims=True) + RMS_EPS) * nw
            zz = z_ref[0, r0 + CHUNK * c:r0 + CHUNK * (c + 1), :]
            o_ref[0, r0 + CHUNK * c:r0 + CHUNK * (c + 1), :] = on * _silu(zz)
    state[...] = s_state


def gated_deltanet(qkv, z, ba, conv_w, a_log, dt_bias, norm_w, tc=256, inv_passes=3):
    b, s, _ = qkv.shape
    hh = GDN_HEADS
    bat = ba[..., :2 * hh].reshape(b, s // 128, 128, 2, hh).transpose(0, 4, 1, 3, 2)
    hp = jnp.zeros((hh, 8, LANES), F32)
    hp = hp.at[:, 0, :].set(a_log.astype(F32)[:, None]).at[:, 1, :].set(dt_bias.astype(F32)[:, None])
    cw = jnp.zeros((8, 3 * hh * LANES), F32).at[:GDN_CONV].set(conv_w.astype(F32))
    hb = tc // 8
    blk = lambda off: pl.BlockSpec((1, tc, LANES), lambda bi, hi, ti: (bi, ti, hi + off))
    halo = lambda off: pl.BlockSpec((1, 8, LANES), lambda bi, hi, ti: (bi, jnp.maximum(ti * hb - 1, 0), hi + off))
    cws = lambda off: pl.BlockSpec((8, LANES), lambda bi, hi, ti: (0, hi + off))
    return pl.pallas_call(
        functools.partial(_gdn_kernel, tc=tc, inv_passes=inv_passes),
        grid=(b, hh, s // tc),
        in_specs=[blk(0), blk(hh), blk(2 * hh), halo(0), halo(hh), halo(2 * hh), cws(0), cws(hh), cws(2 * hh),
                  blk(0),
                  pl.BlockSpec((1, tc, LANES), lambda bi, hi, ti: (bi, ti, 0)),
                  pl.BlockSpec((1, 1, tc // 128, 2, LANES), lambda bi, hi, ti: (bi, hi, ti, 0, 0)),
                  pl.BlockSpec((1, 8, LANES), lambda bi, hi, ti: (hi, 0, 0)),
                  pl.BlockSpec((1, LANES), lambda bi, hi, ti: (0, 0))],
        out_specs=blk(0),
        out_shape=jax.ShapeDtypeStruct((b, s, hh * LANES), F32),
        scratch_shapes=[pltpu.VMEM((3, tc + 8, LANES), F32), pltpu.VMEM((LANES, LANES), F32)],
        compiler_params=_cparams(("parallel", "parallel", "arbitrary")), name="gated_deltanet",
    )(qkv, qkv, qkv, qkv, qkv, qkv, cw, cw, cw, z, ba, bat, hp, norm_w.astype(F32).reshape(1, LANES))


def _compress_kernel(x_ref, pe_ref, w1_ref, w2_ref, o_ref):
    nb = x_ref.shape[3]
    acc = jnp.zeros((nb, LANES), F32)
    for g in range(NSA_GROUPS):
        x = x_ref[0, 0, g]
        first = _dot((x + pe_ref[0, 0]).astype(BF16), w1_ref[0, 0])
        second = _dot((x + pe_ref[0, 1]).astype(BF16), w1_ref[0, 1])
        hid = _silu(first + pltpu.roll(second, nb - 1, 0))
        acc = acc + _dot(hid.astype(BF16), w2_ref[0, g])
    o_ref[0, 0] = acc.astype(o_ref.dtype)


def compress(kvc, pe_k, w1_k, w2_k, pe_v, w1_v, w2_v):
    b, s, _ = kvc.shape
    nb = s // CMP_STRIDE
    width = CMP_STRIDE * NSA_DH
    x = kvc.reshape(b, nb, CMP_STRIDE, 2, NSA_GROUPS, NSA_DH).transpose(0, 3, 4, 1, 2, 5).reshape(b, 2, NSA_GROUPS, nb, width)
    pe = jnp.stack([pe_k, pe_v]).astype(F32).reshape(2, 2, 1, width)
    w1 = jnp.stack([w1_k, w1_v]).astype(BF16).reshape(2, 2, width, -1)
    hid = w1.shape[-1]
    w2 = jnp.stack([w2_k, w2_v]).astype(BF16)
    w2p = jnp.zeros((2, NSA_GROUPS, hid, LANES), BF16)
    for g in range(NSA_GROUPS):
        w2p = w2p.at[:, g, :, NSA_DH * g:NSA_DH * (g + 1)].set(w2)
    return pl.pallas_call(
        _compress_kernel,
        grid=(b, 2),
        in_specs=[pl.BlockSpec((1, 1, NSA_GROUPS, nb, width), lambda bi, j: (bi, j, 0, 0, 0)),
                  pl.BlockSpec((1, 2, 1, width), lambda bi, j: (j, 0, 0, 0)),
                  pl.BlockSpec((1, 2, width, hid), lambda bi, j: (j, 0, 0, 0)),
                  pl.BlockSpec((1, NSA_GROUPS, hid, LANES), lambda bi, j: (j, 0, 0, 0))],
        out_specs=pl.BlockSpec((1, 1, nb, LANES), lambda bi, j: (bi, j, 0, 0)),
        out_shape=jax.ShapeDtypeStruct((b, 2, nb, LANES), BF16),
        compiler_params=_cparams(("parallel", "parallel")), name="nsa_compress",
    )(x, pe, w1, w2p)


def _softmax_pv(s, v):
    m = jnp.max(s, axis=-1, keepdims=True)
    p = jnp.exp(s - m)
    l = jnp.sum(p, axis=-1, keepdims=True)
    return _dot(p.astype(BF16), v) / l


def _nsa_kernel(q_ref, gl_ref, kcvc_ref, kv_ref, bw_ref, bfar_ref, bc_ref, ovl_ref, o_ref):
    n = pl.program_id(1)
    qb_rows = q_ref.shape[1]
    lane = lax.broadcasted_iota(jnp.int32, (qb_rows, LANES), 1)
    lane_f = lane.astype(F32)
    gates = jax.nn.sigmoid(gl_ref[0])
    kc = kcvc_ref[0, 0]
    vc = kcvc_ref[0, 1]
    ovl = ovl_ref[...]
    row_blk = lax.broadcasted_iota(jnp.int32, (LANES, 4 * SLC_LEN), 0)
    col_blk = lax.broadcasted_iota(jnp.int32, (LANES, 4 * SLC_LEN), 1) >> 6
    wcol = lax.broadcasted_iota(jnp.int32, (NSA_HPG * qb_rows, WIN_TILE), 1)

    for g in range(NSA_GROUPS):
        qs = jnp.concatenate([q_ref[0, :, LANES * (NSA_HPG * g + hh):LANES * (NSA_HPG * g + hh + 1)]
                              for hh in range(NSA_HPG)], axis=0)
        bc = bc_ref[NSA_HPG * g:NSA_HPG * (g + 1)].reshape(NSA_HPG * qb_rows, LANES)
        s = _dot_nt(qs, kc) + bc
        valid = bc > 0.5 * NEG
        m = jnp.max(s, axis=-1, keepdims=True)
        e = jnp.where(valid, jnp.exp(s - m), 0.0)
        p = e / jnp.maximum(jnp.sum(e, axis=-1, keepdims=True), 1e-30)
        o_cmp = _dot(p.astype(BF16), vc)
        psum = p[0:qb_rows] + p[qb_rows:2 * qb_rows] + p[2 * qb_rows:3 * qb_rows] + p[3 * qb_rows:4 * qb_rows]
        imp = _mm_exact_rhs(psum, ovl)
        forced = jnp.logical_or(jnp.logical_or(lane == 0, lane == n), lane == n - 1)
        val = jnp.where(lane > n, -1.0, jnp.where(forced, 1e6, imp))
        val = jnp.where(lane >= kv_ref.shape[1] // SLC_LEN - KV_PAD // SLC_LEN, -3.0, val)
        sel = jnp.zeros((qb_rows, LANES), F32)
        for _ in range(N_SELECT):
            mx = jnp.max(val, axis=-1, keepdims=True)
            pick = jnp.min(jnp.where(val == mx, lane_f, 1e4), axis=-1, keepdims=True)
            hit = lane_f == pick
            sel = jnp.where(hit, 1.0, sel)
            val = jnp.where(hit, -4.0, val)
        start = pl.multiple_of(n * SLC_LEN + (KV_PAD - 3 * SLC_LEN), SLC_LEN)
        k_near = kv_ref[0, pl.ds(start, 4 * SLC_LEN), 0:LANES]
        v_near = kv_ref[0, pl.ds(start, 4 * SLC_LEN), LANES:2 * LANES]
        s = _dot_nt(qs, k_near) + bw_ref[g, :, WIN_TILE - 4 * SLC_LEN:WIN_TILE]
        e_near = (row_blk == col_blk + (n - 3)).astype(BF16)
        sm = _dot(sel.astype(BF16), e_near)
        s = jnp.where(jnp.concatenate([sm] * NSA_HPG, axis=0) > 0.5, s, NEG)
        m = jnp.max(s, axis=-1, keepdims=True)
        p = jnp.exp(s - m)
        l = jnp.sum(p, axis=-1, keepdims=True)
        acc = _dot(p.astype(BF16), v_near)
        sel_far = jnp.where(lane <= n - 4, sel, 0.0).astype(BF16)
        b_far = jnp.concatenate([bfar_ref[g], bfar_ref[g]], axis=1)

        def far_body(c, carry):
            m_i, l_i, acc_i = carry
            st = pl.multiple_of(KV_PAD + c * 4 * SLC_LEN, math.gcd(KV_PAD, 4 * SLC_LEN))
            k_c = kv_ref[0, pl.ds(st, 4 * SLC_LEN), 0:LANES]
            v_c = kv_ref[0, pl.ds(st, 4 * SLC_LEN), LANES:2 * LANES]
            sc = _dot_nt(qs, k_c) + b_far
            e_c = (row_blk == col_blk + 4 * c).astype(BF16)
            smc = _dot(sel_far, e_c)
            sc = jnp.where(jnp.concatenate([smc] * NSA_HPG, axis=0) > 0.5, sc, NEG)
            m_n = jnp.maximum(m_i, jnp.max(sc, axis=-1, keepdims=True))
            alpha = jnp.exp(m_i - m_n)
            pc = jnp.exp(sc - m_n)
            return (m_n, alpha * l_i + jnp.sum(pc, axis=-1, keepdims=True),
                    alpha * acc_i + _dot(pc.astype(BF16), v_c))

        m, l, acc = lax.fori_loop(0, n // 4, far_body, (m, l, acc))
        o_slc = acc / l
        wst = pl.multiple_of(n * SLC_LEN + (KV_PAD + SLC_LEN - WIN_TILE), SLC_LEN)
        k_w = kv_ref[0, pl.ds(wst, WIN_TILE), 2 * LANES:3 * LANES]
        v_w = kv_ref[0, pl.ds(wst, WIN_TILE), 3 * LANES:4 * LANES]
        s = _dot_nt(qs, k_w) + bw_ref[g]
        s = jnp.where(wcol >= (WIN_TILE - SLC_LEN) - SLC_LEN * n, s, NEG)
        o_win = _softmax_pv(s, v_w)
        pieces = []
        for hh in range(NSA_HPG):
            hd = NSA_HPG * g + hh
            rs = slice(qb_rows * hh, qb_rows * (hh + 1))
            piece = (gates[:, 3 * hd:3 * hd + 1] * o_cmp[rs] + gates[:, 3 * hd + 1:3 * hd + 2] * o_slc[rs]
                     + gates[:, 3 * hd + 2:3 * hd + 3] * o_win[rs])
            pieces.append(piece)
        for hp in range(2):
            lo, hi = pieces[2 * hp], pieces[2 * hp + 1]
            if g == 0:
                hi = pltpu.roll(hi, NSA_DH, 1)
            else:
                lo = pltpu.roll(lo, NSA_DH, 1)
            o_ref[0, :, LANES * (2 * g + hp):LANES * (2 * g + hp + 1)] = jnp.where(lane < NSA_DH, lo, hi)


def _rel_bucket(dist):
    n = jnp.maximum(dist, 0)
    exact = NUM_BUCKETS // 2
    nf = jnp.maximum(n, 1).astype(F32)
    large = exact + (jnp.log(nf / exact) / math.log(REL_MAX_DIST / exact) * (NUM_BUCKETS - exact)).astype(jnp.int32)
    return jnp.where(n < exact, n, jnp.minimum(large, NUM_BUCKETS - 1))


def nsa_attention(qpad, gl, kcvc, kv4, rel_bias):
    b, s, _ = qpad.shape
    qb = SLC_LEN
    nq = s // qb
    nc = s // CMP_STRIDE - CMP_LEN // CMP_STRIDE + 1
    ns = s // SLC_LEN
    table = rel_bias.astype(F32)
    dist_w = jnp.arange(qb)[:, None] + (WIN_TILE - SLC_LEN) - jnp.arange(WIN_TILE)[None, :]
    ok_w = jnp.logical_and(dist_w >= 0, dist_w < WINDOW)
    bw = jnp.where(ok_w[None], table[_rel_bucket(dist_w)].transpose(2, 0, 1), NEG)
    bw = bw.reshape(NSA_GROUPS, NSA_HPG * qb, WIN_TILE)
    far_val = table[NUM_BUCKETS - 1]
    bfar = jnp.broadcast_to(far_val[:, None, None], (NSA_HEADS, qb, LANES)).reshape(NSA_GROUPS, NSA_HPG * qb, LANES)
    cmp_end = jnp.arange(LANES) * CMP_STRIDE + CMP_LEN - 1
    dist_c = jnp.arange(s)[:, None] - cmp_end[None, :]
    ok_c = jnp.logical_and(dist_c >= 0, jnp.arange(LANES)[None, :] < nc)
    bc = jnp.where(ok_c[None], table[_rel_bucket(dist_c)].transpose(2, 0, 1), NEG)
    cmp_start = jnp.arange(LANES) * CMP_STRIDE
    slc_j = jnp.arange(LANES)
    ovl = jnp.logical_and(jnp.logical_and(cmp_start[:, None] < (slc_j[None, :] + 1) * SLC_LEN,
                                          cmp_end[:, None] >= slc_j[None, :] * SLC_LEN),
                          jnp.logical_and(jnp.arange(LANES)[:, None] < nc, slc_j[None, :] < ns)).astype(F32)
    kvp = jnp.pad(kv4, ((0, 0), (KV_PAD, 0), (0, 0)))
    sp = s + KV_PAD
    return pl.pallas_call(
        _nsa_kernel,
        grid=(b, nq),
        in_specs=[pl.BlockSpec((1, qb, NSA_HEADS * LANES), lambda bi, n: (bi, n, 0)),
                  pl.BlockSpec((1, qb, LANES), lambda bi, n: (bi, n, 0)),
                  pl.BlockSpec((1, 2, s // CMP_STRIDE, LANES), lambda bi, n: (bi, 0, 0, 0)),
                  pl.BlockSpec((1, sp, 4 * LANES), lambda bi, n: (bi, 0, 0)),
                  pl.BlockSpec((NSA_GROUPS, NSA_HPG * qb, WIN_TILE), lambda bi, n: (0, 0, 0)),
                  pl.BlockSpec((NSA_GROUPS, NSA_HPG * qb, LANES), lambda bi, n: (0, 0, 0)),
                  pl.BlockSpec((NSA_HEADS, qb, LANES), lambda bi, n: (0, n, 0)),
                  pl.BlockSpec((LANES, LANES), lambda bi, n: (0, 0))],
        out_specs=pl.BlockSpec((1, qb, NSA_HEADS * NSA_DH), lambda bi, n: (bi, n, 0)),
        out_shape=jax.ShapeDtypeStruct((b, s, NSA_HEADS * NSA_DH), F32),
        compiler_params=_cparams(("parallel", "arbitrary")), name="nsa_attention",
    )(qpad, gl, kcvc, kvp, bw, bfar, bc, ovl)


def _rwkv_proj_kernel(x_ref, halo_ref, nw_ref, mu_ref, vec_ref, wr_ref, wk_ref, wv_ref,
                      w1_ref, w2_ref, a1_ref, a2_ref, g1_ref, g2_ref,
                      r_ref, k_ref, v_ref, lw_ref, kk_ref, a_ref, g_ref, scr, *, tm):
    t = pl.program_id(1)
    nw = nw_ref[...]

    def norm(x):
        return x * lax.rsqrt(jnp.mean(x * x, axis=-1, keepdims=True) + RMS_EPS) * nw

    h = norm(x_ref[0])
    scr[0:8, :] = norm(halo_ref[0]) * (t > 0).astype(F32)
    scr[8:8 + tm, :] = h
    xx = scr[7:7 + tm, :] - h
    mix = lambda j: (h + xx * mu_ref[j:j + 1, :]).astype(BF16)
    w0, a0, k_k, k_a = vec_ref[0:1, :], vec_ref[1:2, :], vec_ref[2:3, :], vec_ref[3:4, :]
    r = _dot(mix(0), wr_ref[...])
    wl = w0 + _dot(jnp.tanh(_dot(mix(1), w1_ref[...])).astype(BF16), w2_ref[...])
    k = _dot(mix(2), wk_ref[...])
    v = _dot(mix(3), wv_ref[...])
    a = jax.nn.sigmoid(a0 + _dot(_dot(mix(4), a1_ref[...]).astype(BF16), a2_ref[...]))
    g = _dot(jax.nn.sigmoid(_dot(mix(5), g1_ref[...])).astype(BF16), g2_ref[...])
    w_raw = -_softplus(-wl) - 0.5
    r_ref[0] = r
    k_ref[0] = k * (1.0 + (a - 1.0) * k_a)
    v_ref[0] = v
    lw_ref[0] = -jnp.exp(w_raw)
    kk_ref[0] = k * k_k
    a_ref[0] = a
    g_ref[0] = g


def rwkv_proj(x, nw, mu, w_r, w_k, w_v, w0, w1, w2, a0, a1, a2, g1, g2, k_k, k_a, tm=256):
    b, s, d = x.shape
    pad_to = lambda w, rows, cols: jnp.zeros((rows, cols), BF16).at[:w.shape[0], :w.shape[1]].set(w.astype(BF16))
    lora = lambda n: -(-n // LANES) * LANES
    mu8 = jnp.zeros((8, d), F32).at[:6].set(mu.astype(F32))
    vec = jnp.zeros((8, d), F32).at[0].set(w0).at[1].set(a0).at[2].set(k_k).at[3].set(k_a)
    big = [w.astype(BF16) for w in (w_r, w_k, w_v)]
    small = [pad_to(w1, d, lora(w1.shape[1])), pad_to(w2, lora(w2.shape[0]), d),
             pad_to(a1, d, lora(a1.shape[1])), pad_to(a2, lora(a2.shape[0]), d),
             pad_to(g1, d, lora(g1.shape[1])), pad_to(g2, lora(g2.shape[0]), d)]
    whole = lambda w: pl.BlockSpec(w.shape, lambda bi, ti: (0, 0))
    hb = tm // 8
    tile = pl.BlockSpec((1, tm, d), lambda bi, ti: (bi, ti, 0))
    return pl.pallas_call(
        functools.partial(_rwkv_proj_kernel, tm=tm),
        grid=(b, s // tm),
        in_specs=[tile, pl.BlockSpec((1, 8, d), lambda bi, ti: (bi, jnp.maximum(ti * hb - 1, 0), 0)),
                  pl.BlockSpec((1, d), lambda bi, ti: (0, 0)), whole(mu8), whole(vec)]
                 + [whole(w) for w in big] + [whole(w) for w in small],
        out_specs=[tile] * 7,
        out_shape=[jax.ShapeDtypeStruct((b, s, d), F32)] * 7,
        scratch_shapes=[pltpu.VMEM((tm + 8, d), F32)],
        compiler_params=_cparams(("parallel", "arbitrary")), name="rwkv_proj",
    )(x, x, nw.reshape(1, d), mu8, vec, *big, *small)


def _wkv_kernel(r_ref, k_ref, v_ref, lw_ref, kk_ref, a_ref, g_ref, lnw_ref, lnb_ref, rk_ref,
                o_ref, state, *, tc, inv_passes):
    t = pl.program_id(2)

    @pl.when(t == 0)
    def _():
        state[...] = jnp.zeros_like(state)

    m0, m1 = _lane_masks()
    lane = lax.broadcasted_iota(jnp.int32, (1, LANES), 1)
    low = lane < 64

    def seg_sum(x):
        return jnp.where(low, jnp.sum(x * m0, axis=-1, keepdims=True), jnp.sum(x * m1, axis=-1, keepdims=True))

    r_all, k_all, v_all = r_ref[0], k_ref[0], v_ref[0]
    kk = kk_ref[0]
    kk = kk * lax.rsqrt(seg_sum(kk * kk) + 1e-12)
    a_all = -kk
    b_all = kk * a_ref[0]

    i64 = lax.broadcasted_iota(jnp.int32, (CHUNK, LANES), 0)
    j64 = lax.broadcasted_iota(jnp.int32, (CHUNK, LANES), 1) & (CHUNK - 1)
    strict_m = i64 > j64
    incl_m = i64 >= j64
    ltri = _tri_ones(CHUNK, True)
    eye128 = (lax.broadcasted_iota(jnp.int32, (LANES, LANES), 0)
              == lax.broadcasted_iota(jnp.int32, (LANES, LANES), 1)).astype(F32)
    bd = ((lax.broadcasted_iota(jnp.int32, (LANES, LANES), 0) >> 6)
          == (lax.broadcasted_iota(jnp.int32, (LANES, LANES), 1) >> 6))
    zeros64 = jnp.zeros((CHUNK, LANES), F32)
    st = lambda x: _stack2(x, m0, m1)

    hs = state[...]
    ys = []
    for c in range(tc // CHUNK):
        rows = slice(CHUNK * c, CHUNK * (c + 1))
        r, k, v, av, bv = r_all[rows], k_all[rows], v_all[rows], a_all[rows], b_all[rows]
        lw = lw_ref[0, rows, :]
        cs = _mm_exact_lhs(ltri, lw)
        cl = cs[CHUNK - 1:CHUNK, :]
        a_t = av * jnp.exp(cs - lw)
        r_t = r * jnp.exp(cs)
        inv_w = jnp.exp(-cs)
        b_t, k_t = bv * inv_w, k * inv_w
        to_end = jnp.exp(cl - cs)
        b_e, k_e = bv * to_end, k * to_end
        ar = jnp.concatenate([a_t, r_t], axis=0)
        gb = _mm(ar, st(b_t), nt=True)
        gk = _mm(ar, st(k_t), nt=True)
        l_ab = jnp.where(strict_m, gb[0:CHUNK], 0.0)
        l_ak = jnp.where(strict_m, gk[0:CHUNK], 0.0)
        m_rb = jnp.where(incl_m, gb[CHUNK:2 * CHUNK], 0.0)
        m_rk = jnp.where(incl_m, gk[CHUNK:2 * CHUNK], 0.0)
        tinv = _tri_inv_cat(-l_ab, inv_passes)
        vst = st(v)
        p_mat = _mm(tinv, st(a_t))
        q_mat = _mm(tinv, st(_mm(l_ak, vst)))
        r_hat = r_t + _mm(m_rb, st(p_mat))
        y_hat = _mm(m_rb, st(q_mat)) + _mm(m_rk, vst)
        bkt = jnp.concatenate([b_e, k_e], axis=0).T
        trans = eye128 * jnp.exp(cl) + jnp.where(bd, _mm(bkt, jnp.concatenate([p_mat, zeros64], axis=0)), 0.0)
        h_add = jnp.where(bd, _mm(bkt, jnp.concatenate([q_mat, v], axis=0)), 0.0)
        ys.append(_mm(r_hat, hs) + y_hat)
        hs = _mm(trans, hs, passes=3) + h_add
    state[...] = hs

    y = jnp.concatenate(ys, axis=0)
    mean = seg_sum(y) * (1.0 / 64)
    dlt = y - mean
    var = seg_sum(dlt * dlt) * (1.0 / 64)
    yn = dlt * lax.rsqrt(var + RWKV_GN_EPS) * lnw_ref[...] + lnb_ref[...]
    yn = yn + seg_sum(r_all * k_all * rk_ref[...]) * v_all
    o_ref[0] = yn * g_ref[0]


def wkv7(r, k, v, lw, kk, a, g, lnx_w, lnx_b, r_k, tc=256, inv_passes=3):
    b, s, d = r.shape
    tile = pl.BlockSpec((1, tc, LANES), lambda bi, pi, ti: (bi, ti, pi))
    vec = pl.BlockSpec((1, LANES), lambda bi, pi, ti: (0, pi))
    return pl.pallas_call(
        functools.partial(_wkv_kernel, tc=tc, inv_passes=inv_passes),
        grid=(b, d // LANES, s // tc),
        in_specs=[tile] * 7 + [vec] * 3,
        out_specs=tile,
        out_shape=jax.ShapeDtypeStruct((b, s, d), F32),
        scratch_shapes=[pltpu.VMEM((LANES, LANES), F32)],
        compiler_params=_cparams(("parallel", "parallel", "arbitrary")), name="wkv7",
    )(r, k, v, lw, kk, a, g, lnx_w.reshape(1, d), lnx_b.reshape(1, d), r_k.reshape(1, d))


def _in_proj_weights(w_in):
    gw = GDN_HEADS * LANES
    c = 4 * gw
    w_qkv, w_z = w_in[:, 0:3 * gw], w_in[:, 3 * gw:c]
    d = w_in.shape[0]
    w_ba = jnp.zeros((d, LANES), w_in.dtype).at[:, :2 * GDN_HEADS].set(w_in[:, c:c + 2 * GDN_HEADS])
    c += 2 * GDN_HEADS
    nw = NSA_HEADS * NSA_DH
    w_q = w_in[:, c:c + nw].reshape(d, NSA_GROUPS, NSA_HPG, NSA_DH) * (NSA_DH ** -0.5)
    c += nw
    w_qpad = jnp.zeros((d, NSA_GROUPS, NSA_HPG, NSA_GROUPS, NSA_DH), w_in.dtype)
    for g in range(NSA_GROUPS):
        w_qpad = w_qpad.at[:, g, :, g, :].set(w_q[:, g])
    w_qpad = w_qpad.reshape(d, NSA_HEADS * LANES)
    kvw = NSA_GROUPS * NSA_DH
    w_cmp = w_in[:, c:c + 2 * kvw]
    w_kv4 = w_in[:, c + 2 * kvw:c + 6 * kvw]
    c += 6 * kvw
    w_gate = jnp.zeros((d, LANES), w_in.dtype).at[:, :3 * NSA_HEADS].set(w_in[:, c:c + 3 * NSA_HEADS])
    ws = [w_qkv, w_z, w_ba, w_qpad, w_cmp, w_kv4, w_gate]
    dts = [F32, F32, F32, BF16, F32, BF16, F32]
    return [w.astype(BF16) for w in ws], dts


def kernel(x, rel_bias, final_norm, e_attn_norm, e_w_in, e_conv_w, e_a_log, e_dt_bias, e_gdn_norm, e_cmpk_pe, e_cmpk_w1, e_cmpk_w2, e_cmpv_pe, e_cmpv_w1, e_cmpv_w2, e_w_out, e_ffn_norm, e_ffn_gate, e_ffn_up, e_ffn_down, o_attn_norm, o_mu, o_w_r, o_w_k, o_w_v, o_w_o, o_w0, o_w1, o_w2, o_a0, o_a1, o_a2, o_g1, o_g2, o_k_k, o_k_a, o_r_k, o_lnx_w, o_lnx_b, o_ffn_norm, o_ffn_gate, o_ffn_up, o_ffn_down):
    b, s, d = x.shape
    m = b * s
    xf = x.reshape(m, d)
    bf = lambda w: w.astype(BF16)

    ws, dts = _in_proj_weights(e_w_in[0])
    qkv, z, ba, qpad, kvc, kv4, gl = norm_linear(xf, e_attn_norm[0], ws, dts)
    sh = lambda t: t.reshape(b, s, t.shape[-1])
    o_gdn = gated_deltanet(sh(qkv), sh(z), sh(ba), e_conv_w[0], e_a_log[0], e_dt_bias[0], e_gdn_norm[0])
    kcvc = compress(sh(kvc), e_cmpk_pe[0], e_cmpk_w1[0], e_cmpk_w2[0], e_cmpv_pe[0], e_cmpv_w1[0], e_cmpv_w2[0])
    o_nsa = nsa_attention(sh(qpad), sh(gl), kcvc, sh(kv4), rel_bias)
    gw = GDN_HEADS * LANES
    xf = linear_res(xf, [o_gdn.reshape(m, -1), o_nsa.reshape(m, -1)], [bf(e_w_out[0][:gw]), bf(e_w_out[0][gw:])])
    xf = ffn(xf, e_ffn_norm[0], bf(e_ffn_gate[0]), bf(e_ffn_up[0]), bf(e_ffn_down[0]), final_norm, False)

    r, k, v, lw, kk, a, g = rwkv_proj(xf.reshape(b, s, d), o_attn_norm[0], o_mu[0], o_w_r[0], o_w_k[0], o_w_v[0],
                                      o_w0[0], o_w1[0], o_w2[0], o_a0[0], o_a1[0], o_a2[0], o_g1[0], o_g2[0],
                                      o_k_k[0], o_k_a[0])
    yg = wkv7(r, k, v, lw, kk, a, g, o_lnx_w[0], o_lnx_b[0], o_r_k[0].reshape(-1))
    xf = linear_res(xf, [yg.reshape(m, d)], [bf(o_w_o[0])])
    xf = ffn(xf, o_ffn_norm[0], bf(o_ffn_gate[0]), bf(o_ffn_up[0]), bf(o_ffn_down[0]), final_norm, True)
    return xf.reshape(b, s, d)
```

```python
import functools
import math

import jax
import jax.numpy as jnp
from jax import lax
from jax.experimental import pallas as pl
from jax.experimental.pallas import tpu as pltpu

F32 = jnp.float32
BF16 = jnp.bfloat16

VMEM_LIMIT_BYTES = 52 * 1024 * 1024
LANES = 128
CHUNK = 64

RMS_EPS = 1e-6
GDN_HEADS = 4
GDN_CONV = 4
NSA_HEADS = 8
NSA_GROUPS = 2
NSA_HPG = 4
NSA_DH = 64
CMP_LEN = 32
CMP_STRIDE = 16
SLC_LEN = 64
N_SELECT = 8
WINDOW = 512
NUM_BUCKETS = 32
REL_MAX_DIST = 128
NEG = -1e30
KV_PAD = 640
WIN_TILE = 640
RWKV_GN_EPS = 64e-5


def _cparams(sem):
    return pltpu.CompilerParams(dimension_semantics=sem, vmem_limit_bytes=VMEM_LIMIT_BYTES)


def _dot(a, b):
    return jnp.dot(a, b, preferred_element_type=F32)


def _dot_nt(a, b):
    return lax.dot_general(a, b, (((1,), (1,)), ((), ())), preferred_element_type=F32)


def _split(x, terms):
    out = []
    rem = x
    for _ in range(terms - 1):
        p = rem.astype(BF16)
        out.append(p)
        rem = rem - p.astype(F32)
    out.append(rem.astype(BF16))
    return out


def _mm(a, b, passes=1, nt=False):
    d = _dot_nt if nt else _dot
    if passes == 1:
        return d(a.astype(BF16), b.astype(BF16))
    ah, al = _split(a, 2)
    bh, bl = _split(b, 2)
    return d(ah, bh) + (d(ah, bl) + d(al, bh))


def _mm_exact_lhs(a01, b):
    a = a01.astype(BF16)
    b0, b1, b2 = _split(b, 3)
    return _dot(a, b0) + (_dot(a, b1) + _dot(a, b2))


def _mm_exact_rhs(a, b01):
    b = b01.astype(BF16)
    a0, a1, a2 = _split(a, 3)
    return _dot(a0, b) + (_dot(a1, b) + _dot(a2, b))


def _silu(x):
    return x * jax.nn.sigmoid(x)


def _softplus(x):
    return jnp.maximum(x, 0.0) + jnp.log1p(jnp.exp(-jnp.abs(x)))


def _lane_masks():
    lane = lax.broadcasted_iota(jnp.int32, (1, LANES), 1)
    m0 = (lane < 64).astype(F32)
    return m0, 1.0 - m0


def _stack2(x, m0, m1):
    return jnp.concatenate([x * m0, x * m1], axis=0)


def _tri_inv_cat(lcats, passes):
    m0, m1 = _lane_masks()
    i = lax.broadcasted_iota(jnp.int32, (CHUNK, LANES), 0)
    j = lax.broadcasted_iota(jnp.int32, (CHUNK, LANES), 1) & (CHUNK - 1)
    eye = (i == j).astype(F32)
    blk16 = (i >> 4) == (j >> 4)
    st = lambda x: _stack2(x, m0, m1)
    mm = lambda a, b: _mm(a, b, passes)
    each = lambda f, *ls: [f(*xs) for xs in zip(*ls)]
    ld = each(lambda l: jnp.where(blk16, l, 0.0), lcats)
    off = each(lambda l: jnp.where(blk16, 0.0, l), lcats)
    sld = each(st, ld)
    l2 = each(mm, ld, sld)
    sl2 = each(st, l2)
    n1 = each(lambda d, l, s: ((eye - d) + l) - mm(d, s), ld, l2, sl2)
    l4 = each(mm, l2, sl2)
    sl4 = each(st, l4)
    n2 = each(lambda a, s: a + mm(a, s), n1, sl4)
    l8 = each(mm, l4, sl4)
    x = each(lambda a, l: a + mm(a, st(l)), n2, l8)
    sx = each(st, x)
    mt = each(lambda a, o: mm(a, st(o)), x, off)
    m2 = each(lambda a: mm(a, st(a)), mt)
    mx = each(mm, mt, sx)
    m2x = each(mm, m2, sx)
    m3x = each(lambda a, b: mm(a, st(b)), m2, mx)
    return each(lambda a, b, c, d: (a - b) + (c - d), x, mx, m2x, m3x)


def _tri_ones(n, lower, block=None):
    i = lax.broadcasted_iota(jnp.int32, (n, n), 0)
    j = lax.broadcasted_iota(jnp.int32, (n, n), 1)
    m = (i >= j) if lower else (i <= j)
    if block is not None:
        sh = block.bit_length() - 1
        m = jnp.logical_and(m, (i >> sh) == (j >> sh))
    return m.astype(F32)


def _norm_linear_kernel(x_ref, nw_ref, *refs, n_out):
    w_refs, o_refs = refs[:n_out], refs[n_out:]
    x = x_ref[...]
    h = x * lax.rsqrt(jnp.mean(x * x, axis=-1, keepdims=True) + RMS_EPS) * nw_ref[...]
    hb = h.astype(BF16)
    for w_ref, o_ref in zip(w_refs, o_refs):
        o_ref[...] = _dot(hb, w_ref[...]).astype(o_ref.dtype)


def norm_linear(x, nw, weights, out_dtypes, tm=512):
    m, k = x.shape
    n_out = len(weights)
    in_specs = [pl.BlockSpec((tm, k), lambda i: (i, 0)), pl.BlockSpec((1, k), lambda i: (0, 0))]
    in_specs += [pl.BlockSpec(w.shape, lambda i: (0, 0)) for w in weights]
    out_specs = [pl.BlockSpec((tm, w.shape[1]), lambda i: (i, 0)) for w in weights]
    out_shape = [jax.ShapeDtypeStruct((m, w.shape[1]), dt) for w, dt in zip(weights, out_dtypes)]
    return pl.pallas_call(
        functools.partial(_norm_linear_kernel, n_out=n_out),
        grid=(m // tm,), in_specs=in_specs, out_specs=out_specs, out_shape=out_shape,
        compiler_params=_cparams(("parallel",)), name="norm_linear",
    )(x, nw.reshape(1, k), *weights)


def _linear_res_kernel(res_ref, *refs, n_in):
    a_refs, w_refs, o_ref = refs[:n_in], refs[n_in:2 * n_in], refs[2 * n_in]
    acc = res_ref[...]
    for a_ref, w_ref in zip(a_refs, w_refs):
        acc = acc + _dot(a_ref[...].astype(BF16), w_ref[...])
    o_ref[...] = acc


def linear_res(res, acts, weights, tm=512):
    m, n = res.shape
    n_in = len(acts)
    in_specs = [pl.BlockSpec((tm, n), lambda i: (i, 0))]
    in_specs += [pl.BlockSpec((tm, a.shape[1]), lambda i: (i, 0)) for a in acts]
    in_specs += [pl.BlockSpec(w.shape, lambda i: (0, 0)) for w in weights]
    return pl.pallas_call(
        functools.partial(_linear_res_kernel, n_in=n_in),
        grid=(m // tm,), in_specs=in_specs, out_specs=pl.BlockSpec((tm, n), lambda i: (i, 0)),
        out_shape=jax.ShapeDtypeStruct((m, n), F32),
        compiler_params=_cparams(("parallel",)), name="linear_res",
    )(res, *acts, *weights)


def _ffn_kernel(x_ref, nw_ref, wg_ref, wu_ref, wd_ref, fw_ref, o_ref, xn_scr, acc_scr, *, final_norm):
    f = pl.program_id(1)

    @pl.when(f == 0)
    def _():
        x = x_ref[...]
        h = x * lax.rsqrt(jnp.mean(x * x, axis=-1, keepdims=True) + RMS_EPS) * nw_ref[...]
        xn_scr[...] = h.astype(BF16)
        acc_scr[...] = jnp.zeros_like(acc_scr)

    xn = xn_scr[...]
    g = _dot(xn, wg_ref[...])
    u = _dot(xn, wu_ref[...])
    hid = (_silu(g) * u).astype(BF16)
    acc_scr[...] += _dot(hid, wd_ref[...])

    @pl.when(f == pl.num_programs(1) - 1)
    def _():
        y = x_ref[...] + acc_scr[...]
        if final_norm:
            y = y * lax.rsqrt(jnp.mean(y * y, axis=-1, keepdims=True) + RMS_EPS) * fw_ref[...]
        o_ref[...] = y


def ffn(x, nw, wg, wu, wd, fw, final_norm, tm=512, tf=1408):
    m, d = x.shape
    dff = wg.shape[1]
    return pl.pallas_call(
        functools.partial(_ffn_kernel, final_norm=final_norm),
        grid=(m // tm, dff // tf),
        in_specs=[pl.BlockSpec((tm, d), lambda i, f: (i, 0)),
                  pl.BlockSpec((1, d), lambda i, f: (0, 0)),
                  pl.BlockSpec((d, tf), lambda i, f: (0, f)),
                  pl.BlockSpec((d, tf), lambda i, f: (0, f)),
                  pl.BlockSpec((tf, d), lambda i, f: (f, 0)),
                  pl.BlockSpec((1, d), lambda i, f: (0, 0))],
        out_specs=pl.BlockSpec((tm, d), lambda i, f: (i, 0)),
        out_shape=jax.ShapeDtypeStruct((m, d), F32),
        scratch_shapes=[pltpu.VMEM((tm, d), BF16), pltpu.VMEM((tm, d), F32)],
        compiler_params=_cparams(("parallel", "arbitrary")), name="ffn",
    )(x, nw.reshape(1, d), wg, wu, wd, fw.reshape(1, d))


def _gdn_kernel(q_ref, k_ref, v_ref, qh_ref, kh_ref, vh_ref, cwq_ref, cwk_ref, cwv_ref,
                z_ref, ba_ref, bat_ref, hp_ref, nw_ref, o_ref, scr, state, *, tc, inv_passes):
    h = pl.program_id(1)
    t = pl.program_id(2)

    @pl.when(t == 0)
    def _():
        state[...] = jnp.zeros_like(state)

    not_first = (t > 0).astype(F32)

    def conv(x_ref, halo_ref, w_ref, slot):
        scr[slot, 0:8, :] = halo_ref[0] * not_first
        scr[slot, 8:8 + tc, :] = x_ref[0]
        w = w_ref[...]
        y = (w[3:4] * scr[slot, 8:8 + tc, :] + w[2:3] * scr[slot, 7:7 + tc, :]
             + w[1:2] * scr[slot, 6:6 + tc, :] + w[0:1] * scr[slot, 5:5 + tc, :])
        return _silu(y)

    q = conv(q_ref, qh_ref, cwq_ref, 0)
    k = conv(k_ref, kh_ref, cwk_ref, 1)
    v = conv(v_ref, vh_ref, cwv_ref, 2)
    qn = q * lax.rsqrt(jnp.sum(q * q, axis=-1, keepdims=True) + 1e-6)
    kn = k * lax.rsqrt(jnp.sum(k * k, axis=-1, keepdims=True) + 1e-6)

    lane = lax.broadcasted_iota(jnp.int32, (1, LANES), 1)
    ba = ba_ref[0]
    b_col = jnp.sum(jnp.where(lane == h, ba, 0.0), axis=-1, keepdims=True)
    a_col = jnp.sum(jnp.where(lane == h + GDN_HEADS, ba, 0.0), axis=-1, keepdims=True)
    beta = jax.nn.sigmoid(b_col)
    a_log = hp_ref[0, 0:1, :]
    dt_b = hp_ref[0, 1:2, :]
    g_rep = -jnp.exp(a_log) * _softplus(a_col + dt_b)

    m0, m1 = _lane_masks()
    i64 = lax.broadcasted_iota(jnp.int32, (CHUNK, LANES), 0)
    j64 = lax.broadcasted_iota(jnp.int32, (CHUNK, LANES), 1) & (CHUNK - 1)
    low_half = lax.broadcasted_iota(jnp.int32, (CHUNK, LANES), 1) < CHUNK
    ltri_bd = _tri_ones(2 * CHUNK, True, CHUNK)
    utri_bd = _tri_ones(2 * CHUNK, False, CHUNK)
    eye128 = (lax.broadcasted_iota(jnp.int32, (LANES, LANES), 0)
              == lax.broadcasted_iota(jnp.int32, (LANES, LANES), 1)).astype(F32)
    zeros64 = jnp.zeros((CHUNK, LANES), F32)
    scale = 1.0 / math.sqrt(LANES)
    nw = nw_ref[...]

    npair = tc // (2 * CHUNK)
    each = lambda f, *ls: [f(*xs) for xs in zip(*ls)]
    halves = lambda x: jnp.where(low_half, x[0:CHUNK], x[CHUNK:2 * CHUNK])
    prow = [slice(2 * CHUNK * p, 2 * CHUNK * (p + 1)) for p in range(npair)]
    kp = [kn[rs] for rs in prow]
    qp = [qn[rs] * scale for rs in prow]
    vp = [v[rs] for rs in prow]
    beta_p = [beta[rs] for rs in prow]
    gc_col = [_mm_exact_lhs(ltri_bd, g_rep[rs]) for rs in prow]
    g_row = [-jnp.exp(a_log[:, 0:1]) * _softplus(bat_ref[0, 0, p, 1:2, :] + dt_b[:, 0:1]) for p in range(npair)]
    gc_row = each(lambda x: _mm_exact_rhs(jnp.broadcast_to(x, (CHUNK, LANES)), utri_bd), g_row)
    dcat = each(lambda c_, r_: jnp.exp(jnp.where(i64 >= j64, halves(c_) - r_, -jnp.inf)), gc_col, gc_row)
    kb = each(lambda k_, b_: k_ * b_, kp, beta_p)
    gram = each(lambda x, y_: _mm(x, y_, nt=True), kb, kp)
    gq = each(lambda x, y_: _mm(x, y_, nt=True), qp, kp)
    strict = each(lambda x, d_: jnp.where(i64 > j64, halves(x) * d_, 0.0), gram, dcat)
    qk_bd = each(lambda x, d_: _stack2(jnp.where(i64 >= j64, halves(x) * d_, 0.0), m0, m1), gq, dcat)
    eg = each(jnp.exp, gc_col)
    g_last = each(lambda x: [x[CHUNK * (c + 1) - 1:CHUNK * (c + 1), :] for c in range(2)], gc_col)
    kdt = each(lambda k_, gl_, gc_: jnp.concatenate(
        [k_[CHUNK * c:CHUNK * (c + 1)] * jnp.exp(gl_[c] - gc_[CHUNK * c:CHUNK * (c + 1)]) for c in range(2)],
        axis=0).T, kp, g_last, gc_col)
    tinv = _tri_inv_cat(strict, inv_passes)
    uw = each(lambda t_, v_, b_, kb_, e_: _mm(_stack2(t_, m0, m1), jnp.concatenate([v_ * b_, kb_ * e_], axis=1)),
              tinv, vp, beta_p, kb, eg)
    u = each(lambda x: x[:, 0:LANES], uw)
    w = each(lambda x: x[:, LANES:2 * LANES], uw)
    q_hat = each(lambda q_, e_, m_, w_: q_ * e_ - _mm(m_, w_), qp, eg, qk_bd, w)
    o_hat = each(_mm, qk_bd, u)
    pad = lambda x, c: jnp.concatenate([x[0:CHUNK], zeros64] if c == 0 else [zeros64, x[CHUNK:2 * CHUNK]], axis=0)
    trans = [eye128 * jnp.exp(g_last[p][c]) - _mm(kdt[p], pad(w[p], c)) for p in range(npair) for c in range(2)]
    h_add = [_mm(kdt[p], pad(u[p], c)) for p in range(npair) for c in range(2)]
    s_state = state[...]
    for p in range(npair):
        for c in range(2):
            cr = slice(CHUNK * c, CHUNK * (c + 1))
            o = _mm(q_hat[p][cr], s_state) + o_hat[p][cr]
            s_state = _mm(trans[2 * p + c], s_state, passes=3) + h_add[2 * p + c]
            on = o * lax.rsqrt(jnp.mean(o * o, axis=-1, keepdims=True) + RMS_EPS) * nw
            r0 = 2 * CHUNK * p + CHUNK * c
            o_ref[0, r0:r0 + CHUNK, :] = on * _silu(z_ref[0, r0:r0 + CHUNK, :])
    state[...] = s_state


def gated_deltanet(qkv, z, ba, conv_w, a_log, dt_bias, norm_w, tc=512, inv_passes=3):
    b, s, _ = qkv.shape
    hh = GDN_HEADS
    bat = ba[..., :2 * hh].reshape(b, s // 128, 128, 2, hh).transpose(0, 4, 1, 3, 2)
    hp = jnp.zeros((hh, 8, LANES), F32)
    hp = hp.at[:, 0, :].set(a_log.astype(F32)[:, None]).at[:, 1, :].set(dt_bias.astype(F32)[:, None])
    cw = jnp.zeros((8, 3 * hh * LANES), F32).at[:GDN_CONV].set(conv_w.astype(F32))
    hb = tc // 8
    blk = lambda off: pl.BlockSpec((1, tc, LANES), lambda bi, hi, ti: (bi, ti, hi + off))
    halo = lambda off: pl.BlockSpec((1, 8, LANES), lambda bi, hi, ti: (bi, jnp.maximum(ti * hb - 1, 0), hi + off))
    cws = lambda off: pl.BlockSpec((8, LANES), lambda bi, hi, ti: (0, hi + off))
    return pl.pallas_call(
        functools.partial(_gdn_kernel, tc=tc, inv_passes=inv_passes),
        grid=(b, hh, s // tc),
        in_specs=[blk(0), blk(hh), blk(2 * hh), halo(0), halo(hh), halo(2 * hh), cws(0), cws(hh), cws(2 * hh),
                  blk(0),
                  pl.BlockSpec((1, tc, LANES), lambda bi, hi, ti: (bi, ti, 0)),
                  pl.BlockSpec((1, 1, tc // 128, 2, LANES), lambda bi, hi, ti: (bi, hi, ti, 0, 0)),
                  pl.BlockSpec((1, 8, LANES), lambda bi, hi, ti: (hi, 0, 0)),
                  pl.BlockSpec((1, LANES), lambda bi, hi, ti: (0, 0))],
        out_specs=blk(0),
        out_shape=jax.ShapeDtypeStruct((b, s, hh * LANES), F32),
        scratch_shapes=[pltpu.VMEM((3, tc + 8, LANES), F32), pltpu.VMEM((LANES, LANES), F32)],
        compiler_params=_cparams(("parallel", "parallel", "arbitrary")), name="gated_deltanet",
    )(qkv, qkv, qkv, qkv, qkv, qkv, cw, cw, cw, z, ba, bat, hp, norm_w.astype(F32).reshape(1, LANES))


def _compress_kernel(x_ref, pe_ref, w1_ref, w2_ref, o_ref):
    nb = x_ref.shape[3]
    acc = jnp.zeros((nb, LANES), F32)
    for g in range(NSA_GROUPS):
        x = x_ref[0, 0, g]
        first = _dot((x + pe_ref[0, 0]).astype(BF16), w1_ref[0, 0])
        second = _dot((x + pe_ref[0, 1]).astype(BF16), w1_ref[0, 1])
        hid = _silu(first + pltpu.roll(second, nb - 1, 0))
        acc = acc + _dot(hid.astype(BF16), w2_ref[0, g])
    o_ref[0, 0] = acc.astype(o_ref.dtype)


def compress(kvc, pe_k, w1_k, w2_k, pe_v, w1_v, w2_v):
    b, s, _ = kvc.shape
    nb = s // CMP_STRIDE
    width = CMP_STRIDE * NSA_DH
    x = kvc.reshape(b, nb, CMP_STRIDE, 2, NSA_GROUPS, NSA_DH).transpose(0, 3, 4, 1, 2, 5).reshape(b, 2, NSA_GROUPS, nb, width)
    pe = jnp.stack([pe_k, pe_v]).astype(F32).reshape(2, 2, 1, width)
    w1 = jnp.stack([w1_k, w1_v]).astype(BF16).reshape(2, 2, width, -1)
    hid = w1.shape[-1]
    w2 = jnp.stack([w2_k, w2_v]).astype(BF16)
    w2p = jnp.zeros((2, NSA_GROUPS, hid, LANES), BF16)
    for g in range(NSA_GROUPS):
        w2p = w2p.at[:, g, :, NSA_DH * g:NSA_DH * (g + 1)].set(w2)
    return pl.pallas_call(
        _compress_kernel,
        grid=(b, 2),
        in_specs=[pl.BlockSpec((1, 1, NSA_GROUPS, nb, width), lambda bi, j: (bi, j, 0, 0, 0)),
                  pl.BlockSpec((1, 2, 1, width), lambda bi, j: (j, 0, 0, 0)),
                  pl.BlockSpec((1, 2, width, hid), lambda bi, j: (j, 0, 0, 0)),
                  pl.BlockSpec((1, NSA_GROUPS, hid, LANES), lambda bi, j: (j, 0, 0, 0))],
        out_specs=pl.BlockSpec((1, 1, nb, LANES), lambda bi, j: (bi, j, 0, 0)),
        out_shape=jax.ShapeDtypeStruct((b, 2, nb, LANES), BF16),
        compiler_params=_cparams(("parallel", "parallel")), name="nsa_compress",
    )(x, pe, w1, w2p)


def _softmax_pv(s, v):
    m = jnp.max(s, axis=-1, keepdims=True)
    p = jnp.exp(s - m)
    l = jnp.sum(p, axis=-1, keepdims=True)
    return _dot(p.astype(BF16), v) / l


def _nsa_kernel(q_ref, gl_ref, kcvc_ref, kv_ref, bw_ref, bfar_ref, bc_ref, ovl_ref, o_ref):
    n = pl.program_id(1)
    qb_rows = q_ref.shape[1]
    lane = lax.broadcasted_iota(jnp.int32, (qb_rows, LANES), 1)
    lane_f = lane.astype(F32)
    gates = jax.nn.sigmoid(gl_ref[0])
    kc = kcvc_ref[0, 0]
    vc = kcvc_ref[0, 1]
    ovl = ovl_ref[...]
    row_blk = lax.broadcasted_iota(jnp.int32, (LANES, 4 * SLC_LEN), 0)
    col_blk = lax.broadcasted_iota(jnp.int32, (LANES, 4 * SLC_LEN), 1) >> 6
    wcol = lax.broadcasted_iota(jnp.int32, (NSA_HPG * qb_rows, WIN_TILE), 1)

    for g in range(NSA_GROUPS):
        qs = jnp.concatenate([q_ref[0, :, LANES * (NSA_HPG * g + hh):LANES * (NSA_HPG * g + hh + 1)]
                              for hh in range(NSA_HPG)], axis=0)
        bc = bc_ref[NSA_HPG * g:NSA_HPG * (g + 1)].reshape(NSA_HPG * qb_rows, LANES)
        s = _dot_nt(qs, kc) + bc
        valid = bc > 0.5 * NEG
        m = jnp.max(s, axis=-1, keepdims=True)
        e = jnp.where(valid, jnp.exp(s - m), 0.0)
        p = e / jnp.maximum(jnp.sum(e, axis=-1, keepdims=True), 1e-30)
        o_cmp = _dot(p.astype(BF16), vc)
        psum = p[0:qb_rows] + p[qb_rows:2 * qb_rows] + p[2 * qb_rows:3 * qb_rows] + p[3 * qb_rows:4 * qb_rows]
        imp = _mm_exact_rhs(psum, ovl)
        forced = jnp.logical_or(jnp.logical_or(lane == 0, lane == n), lane == n - 1)
        val = jnp.where(lane > n, -1.0, jnp.where(forced, 1e6, imp))
        val = jnp.where(lane >= kv_ref.shape[1] // SLC_LEN - KV_PAD // SLC_LEN, -3.0, val)
        sel = jnp.zeros((qb_rows, LANES), F32)
        for _ in range(N_SELECT):
            mx = jnp.max(val, axis=-1, keepdims=True)
            pick = jnp.min(jnp.where(val == mx, lane_f, 1e4), axis=-1, keepdims=True)
            hit = lane_f == pick
            sel = jnp.where(hit, 1.0, sel)
            val = jnp.where(hit, -4.0, val)
        start = pl.multiple_of(n * SLC_LEN + (KV_PAD - 3 * SLC_LEN), SLC_LEN)
        k_near = kv_ref[0, pl.ds(start, 4 * SLC_LEN), 0:LANES]
        v_near = kv_ref[0, pl.ds(start, 4 * SLC_LEN), LANES:2 * LANES]
        s = _dot_nt(qs, k_near) + bw_ref[g, :, WIN_TILE - 4 * SLC_LEN:WIN_TILE]
        e_near = (row_blk == col_blk + (n - 3)).astype(BF16)
        sm = _dot(sel.astype(BF16), e_near)
        s = jnp.where(jnp.concatenate([sm] * NSA_HPG, axis=0) > 0.5, s, NEG)
        m = jnp.max(s, axis=-1, keepdims=True)
        p = jnp.exp(s - m)
        l = jnp.sum(p, axis=-1, keepdims=True)
        acc = _dot(p.astype(BF16), v_near)
        sel_far = jnp.where(lane <= n - 4, sel, 0.0).astype(BF16)
        b_far = jnp.concatenate([bfar_ref[g], bfar_ref[g]], axis=1)

        def far_body(c, carry):
            m_i, l_i, acc_i = carry
            st = pl.multiple_of(KV_PAD + c * 4 * SLC_LEN, math.gcd(KV_PAD, 4 * SLC_LEN))
            k_c = kv_ref[0, pl.ds(st, 4 * SLC_LEN), 0:LANES]
            v_c = kv_ref[0, pl.ds(st, 4 * SLC_LEN), LANES:2 * LANES]
            sc = _dot_nt(qs, k_c) + b_far
            e_c = (row_blk == col_blk + 4 * c).astype(BF16)
            smc = _dot(sel_far, e_c)
            sc = jnp.where(jnp.concatenate([smc] * NSA_HPG, axis=0) > 0.5, sc, NEG)
            m_n = jnp.maximum(m_i, jnp.max(sc, axis=-1, keepdims=True))
            alpha = jnp.exp(m_i - m_n)
            pc = jnp.exp(sc - m_n)
            return (m_n, alpha * l_i + jnp.sum(pc, axis=-1, keepdims=True),
                    alpha * acc_i + _dot(pc.astype(BF16), v_c))

        m, l, acc = lax.fori_loop(0, n // 4, far_body, (m, l, acc))
        o_slc = acc / l
        wst = pl.multiple_of(n * SLC_LEN + (KV_PAD + SLC_LEN - WIN_TILE), SLC_LEN)
        k_w = kv_ref[0, pl.ds(wst, WIN_TILE), 2 * LANES:3 * LANES]
        v_w = kv_ref[0, pl.ds(wst, WIN_TILE), 3 * LANES:4 * LANES]
        s = _dot_nt(qs, k_w) + bw_ref[g]
        s = jnp.where(wcol >= (WIN_TILE - SLC_LEN) - SLC_LEN * n, s, NEG)
        o_win = _softmax_pv(s, v_w)
        pieces = []
        for hh in range(NSA_HPG):
            hd = NSA_HPG * g + hh
            rs = slice(qb_rows * hh, qb_rows * (hh + 1))
            piece = (gates[:, 3 * hd:3 * hd + 1] * o_cmp[rs] + gates[:, 3 * hd + 1:3 * hd + 2] * o_slc[rs]
                     + gates[:, 3 * hd + 2:3 * hd + 3] * o_win[rs])
            pieces.append(piece)
        for hp in range(2):
            lo, hi = pieces[2 * hp], pieces[2 * hp + 1]
            if g == 0:
                hi = pltpu.roll(hi, NSA_DH, 1)
            else:
                lo = pltpu.roll(lo, NSA_DH, 1)
            o_ref[0, :, LANES * (2 * g + hp):LANES * (2 * g + hp + 1)] = jnp.where(lane < NSA_DH, lo, hi)


def _rel_bucket(dist):
    n = jnp.maximum(dist, 0)
    exact = NUM_BUCKETS // 2
    nf = jnp.maximum(n, 1).astype(F32)
    large = exact + (jnp.log(nf / exact) / math.log(REL_MAX_DIST / exact) * (NUM_BUCKETS - exact)).astype(jnp.int32)
    return jnp.where(n < exact, n, jnp.minimum(large, NUM_BUCKETS - 1))


def nsa_attention(qpad, gl, kcvc, kv4, rel_bias):
    b, s, _ = qpad.shape
    qb = SLC_LEN
    nq = s // qb
    nc = s // CMP_STRIDE - CMP_LEN // CMP_STRIDE + 1
    ns = s // SLC_LEN
    table = rel_bias.astype(F32)
    dist_w = jnp.arange(qb)[:, None] + (WIN_TILE - SLC_LEN) - jnp.arange(WIN_TILE)[None, :]
    ok_w = jnp.logical_and(dist_w >= 0, dist_w < WINDOW)
    bw = jnp.where(ok_w[None], table[_rel_bucket(dist_w)].transpose(2, 0, 1), NEG)
    bw = bw.reshape(NSA_GROUPS, NSA_HPG * qb, WIN_TILE)
    far_val = table[NUM_BUCKETS - 1]
    bfar = jnp.broadcast_to(far_val[:, None, None], (NSA_HEADS, qb, LANES)).reshape(NSA_GROUPS, NSA_HPG * qb, LANES)
    cmp_end = jnp.arange(LANES) * CMP_STRIDE + CMP_LEN - 1
    dist_c = jnp.arange(s)[:, None] - cmp_end[None, :]
    ok_c = jnp.logical_and(dist_c >= 0, jnp.arange(LANES)[None, :] < nc)
    bc = jnp.where(ok_c[None], table[_rel_bucket(dist_c)].transpose(2, 0, 1), NEG)
    cmp_start = jnp.arange(LANES) * CMP_STRIDE
    slc_j = jnp.arange(LANES)
    ovl = jnp.logical_and(jnp.logical_and(cmp_start[:, None] < (slc_j[None, :] + 1) * SLC_LEN,
                                          cmp_end[:, None] >= slc_j[None, :] * SLC_LEN),
                          jnp.logical_and(jnp.arange(LANES)[:, None] < nc, slc_j[None, :] < ns)).astype(F32)
    kvp = jnp.pad(kv4, ((0, 0), (KV_PAD, 0), (0, 0)))
    sp = s + KV_PAD
    return pl.pallas_call(
        _nsa_kernel,
        grid=(b, nq),
        in_specs=[pl.BlockSpec((1, qb, NSA_HEADS * LANES), lambda bi, n: (bi, n, 0)),
                  pl.BlockSpec((1, qb, LANES), lambda bi, n: (bi, n, 0)),
                  pl.BlockSpec((1, 2, s // CMP_STRIDE, LANES), lambda bi, n: (bi, 0, 0, 0)),
                  pl.BlockSpec((1, sp, 4 * LANES), lambda bi, n: (bi, 0, 0)),
                  pl.BlockSpec((NSA_GROUPS, NSA_HPG * qb, WIN_TILE), lambda bi, n: (0, 0, 0)),
                  pl.BlockSpec((NSA_GROUPS, NSA_HPG * qb, LANES), lambda bi, n: (0, 0, 0)),
                  pl.BlockSpec((NSA_HEADS, qb, LANES), lambda bi, n: (0, n, 0)),
                  pl.BlockSpec((LANES, LANES), lambda bi, n: (0, 0))],
        out_specs=pl.BlockSpec((1, qb, NSA_HEADS * NSA_DH), lambda bi, n: (bi, n, 0)),
        out_shape=jax.ShapeDtypeStruct((b, s, NSA_HEADS * NSA_DH), F32),
        compiler_params=_cparams(("parallel", "arbitrary")), name="nsa_attention",
    )(qpad, gl, kcvc, kvp, bw, bfar, bc, ovl)


def _rwkv_proj_kernel(x_ref, halo_ref, nw_ref, mu_ref, vec_ref, wr_ref, wk_ref, wv_ref,
                      w1_ref, w2_ref, a1_ref, a2_ref, g1_ref, g2_ref,
                      r_ref, k_ref, v_ref, lw_ref, kk_ref, a_ref, g_ref, scr, *, tm):
    t = pl.program_id(1)
    nw = nw_ref[...]

    def norm(x):
        return x * lax.rsqrt(jnp.mean(x * x, axis=-1, keepdims=True) + RMS_EPS) * nw

    h = norm(x_ref[0])
    scr[0:8, :] = norm(halo_ref[0]) * (t > 0).astype(F32)
    scr[8:8 + tm, :] = h
    xx = scr[7:7 + tm, :] - h
    mix = lambda j: (h + xx * mu_ref[j:j + 1, :]).astype(BF16)
    w0, a0, k_k, k_a = vec_ref[0:1, :], vec_ref[1:2, :], vec_ref[2:3, :], vec_ref[3:4, :]
    r = _dot(mix(0), wr_ref[...])
    wl = w0 + _dot(jnp.tanh(_dot(mix(1), w1_ref[...])).astype(BF16), w2_ref[...])
    k = _dot(mix(2), wk_ref[...])
    v = _dot(mix(3), wv_ref[...])
    a = jax.nn.sigmoid(a0 + _dot(_dot(mix(4), a1_ref[...]).astype(BF16), a2_ref[...]))
    g = _dot(jax.nn.sigmoid(_dot(mix(5), g1_ref[...])).astype(BF16), g2_ref[...])
    w_raw = -_softplus(-wl) - 0.5
    r_ref[0] = r
    k_ref[0] = k * (1.0 + (a - 1.0) * k_a)
    v_ref[0] = v
    lw_ref[0] = -jnp.exp(w_raw)
    kk_ref[0] = k * k_k
    a_ref[0] = a
    g_ref[0] = g


def rwkv_proj(x, nw, mu, w_r, w_k, w_v, w0, w1, w2, a0, a1, a2, g1, g2, k_k, k_a, tm=256):
    b, s, d = x.shape
    pad_to = lambda w, rows, cols: jnp.zeros((rows, cols), BF16).at[:w.shape[0], :w.shape[1]].set(w.astype(BF16))
    lora = lambda n: -(-n // LANES) * LANES
    mu8 = jnp.zeros((8, d), F32).at[:6].set(mu.astype(F32))
    vec = jnp.zeros((8, d), F32).at[0].set(w0).at[1].set(a0).at[2].set(k_k).at[3].set(k_a)
    big = [w.astype(BF16) for w in (w_r, w_k, w_v)]
    small = [pad_to(w1, d, lora(w1.shape[1])), pad_to(w2, lora(w2.shape[0]), d),
             pad_to(a1, d, lora(a1.shape[1])), pad_to(a2, lora(a2.shape[0]), d),
             pad_to(g1, d, lora(g1.shape[1])), pad_to(g2, lora(g2.shape[0]), d)]
    whole = lambda w: pl.BlockSpec(w.shape, lambda bi, ti: (0, 0))
    hb = tm // 8
    tile = pl.BlockSpec((1, tm, d), lambda bi, ti: (bi, ti, 0))
    return pl.pallas_call(
        functools.partial(_rwkv_proj_kernel, tm=tm),
        grid=(b, s // tm),
        in_specs=[tile, pl.BlockSpec((1, 8, d), lambda bi, ti: (bi, jnp.maximum(ti * hb - 1, 0), 0)),
                  pl.BlockSpec((1, d), lambda bi, ti: (0, 0)), whole(mu8), whole(vec)]
                 + [whole(w) for w in big] + [whole(w) for w in small],
        out_specs=[tile] * 7,
        out_shape=[jax.ShapeDtypeStruct((b, s, d), F32)] * 7,
        scratch_shapes=[pltpu.VMEM((tm + 8, d), F32)],
        compiler_params=_cparams(("parallel", "arbitrary")), name="rwkv_proj",
    )(x, x, nw.reshape(1, d), mu8, vec, *big, *small)


def _wkv_kernel(r_ref, k_ref, v_ref, lw_ref, kk_ref, a_ref, g_ref, lnw_ref, lnb_ref, rk_ref,
                o_ref, state, *, tc, inv_passes):
    t = pl.program_id(2)

    @pl.when(t == 0)
    def _():
        state[...] = jnp.zeros_like(state)

    m0, m1 = _lane_masks()
    lane = lax.broadcasted_iota(jnp.int32, (1, LANES), 1)
    low = lane < 64

    def seg_sum(x):
        return jnp.where(low, jnp.sum(x * m0, axis=-1, keepdims=True), jnp.sum(x * m1, axis=-1, keepdims=True))

    r_all, k_all, v_all = r_ref[0], k_ref[0], v_ref[0]
    kk = kk_ref[0]
    kk = kk * lax.rsqrt(seg_sum(kk * kk) + 1e-12)
    a_all = -kk
    b_all = kk * a_ref[0]

    i64 = lax.broadcasted_iota(jnp.int32, (CHUNK, LANES), 0)
    j64 = lax.broadcasted_iota(jnp.int32, (CHUNK, LANES), 1) & (CHUNK - 1)
    strict_m = i64 > j64
    incl_m = i64 >= j64
    ltri = _tri_ones(CHUNK, True)
    eye128 = (lax.broadcasted_iota(jnp.int32, (LANES, LANES), 0)
              == lax.broadcasted_iota(jnp.int32, (LANES, LANES), 1)).astype(F32)
    bd = ((lax.broadcasted_iota(jnp.int32, (LANES, LANES), 0) >> 6)
          == (lax.broadcasted_iota(jnp.int32, (LANES, LANES), 1) >> 6))
    zeros64 = jnp.zeros((CHUNK, LANES), F32)
    st = lambda x: _stack2(x, m0, m1)

    nch = tc // CHUNK
    each = lambda f, *ls: [f(*xs) for xs in zip(*ls)]
    rows = [slice(CHUNK * c, CHUNK * (c + 1)) for c in range(nch)]
    lw = [lw_ref[0, rs, :] for rs in rows]
    r, k, v = ([x[rs] for rs in rows] for x in (r_all, k_all, v_all))
    av, bv = ([x[rs] for rs in rows] for x in (a_all, b_all))
    cs = each(lambda x: _mm_exact_lhs(ltri, x), lw)
    cl = each(lambda x: x[CHUNK - 1:CHUNK, :], cs)
    a_t = each(lambda a_, c_, l_: a_ * jnp.exp(c_ - l_), av, cs, lw)
    r_t = each(lambda r_, c_: r_ * jnp.exp(c_), r, cs)
    inv_w = each(lambda c_: jnp.exp(-c_), cs)
    b_t = each(lambda x, w_: x * w_, bv, inv_w)
    k_t = each(lambda x, w_: x * w_, k, inv_w)
    to_end = each(lambda e_, c_: jnp.exp(e_ - c_), cl, cs)
    bkt = each(lambda b_, k_, e_: jnp.concatenate([b_ * e_, k_ * e_], axis=0).T, bv, k, to_end)
    ar = each(lambda a_, r_: jnp.concatenate([a_, r_], axis=0), a_t, r_t)
    gb = each(lambda x, y_: _mm(x, st(y_), nt=True), ar, b_t)
    gk = each(lambda x, y_: _mm(x, st(y_), nt=True), ar, k_t)
    l_ab = each(lambda x: jnp.where(strict_m, -x[0:CHUNK], 0.0), gb)
    l_ak = each(lambda x: jnp.where(strict_m, x[0:CHUNK], 0.0), gk)
    m_rb = each(lambda x: jnp.where(incl_m, x[CHUNK:2 * CHUNK], 0.0), gb)
    m_rk = each(lambda x: jnp.where(incl_m, x[CHUNK:2 * CHUNK], 0.0), gk)
    vst = each(st, v)
    lakv = each(_mm, l_ak, vst)
    tinv = _tri_inv_cat(l_ab, inv_passes)
    p_mat = each(lambda t_, a_: _mm(t_, st(a_)), tinv, a_t)
    q_mat = each(lambda t_, x: _mm(t_, st(x)), tinv, lakv)
    r_hat = each(lambda r_, m_, p_: r_ + _mm(m_, st(p_)), r_t, m_rb, p_mat)
    y_hat = each(lambda mb, q_, mk, vs: _mm(mb, st(q_)) + _mm(mk, vs), m_rb, q_mat, m_rk, vst)
    trans = each(lambda e_, bk, p_: eye128 * jnp.exp(e_)
                 + jnp.where(bd, _mm(bk, jnp.concatenate([p_, zeros64], axis=0)), 0.0), cl, bkt, p_mat)
    h_add = each(lambda bk, q_, v_: jnp.where(bd, _mm(bk, jnp.concatenate([q_, v_], axis=0)), 0.0), bkt, q_mat, v)
    hs = state[...]
    ys = []
    for c in range(nch):
        ys.append(_mm(r_hat[c], hs) + y_hat[c])
        hs = _mm(trans[c], hs, passes=3) + h_add[c]
    state[...] = hs

    y = jnp.concatenate(ys, axis=0)
    mean = seg_sum(y) * (1.0 / 64)
    dlt = y - mean
    var = seg_sum(dlt * dlt) * (1.0 / 64)
    yn = dlt * lax.rsqrt(var + RWKV_GN_EPS) * lnw_ref[...] + lnb_ref[...]
    yn = yn + seg_sum(r_all * k_all * rk_ref[...]) * v_all
    o_ref[0] = yn * g_ref[0]


def wkv7(r, k, v, lw, kk, a, g, lnx_w, lnx_b, r_k, tc=512, inv_passes=3):
    b, s, d = r.shape
    tile = pl.BlockSpec((1, tc, LANES), lambda bi, pi, ti: (bi, ti, pi))
    vec = pl.BlockSpec((1, LANES), lambda bi, pi, ti: (0, pi))
    return pl.pallas_call(
        functools.partial(_wkv_kernel, tc=tc, inv_passes=inv_passes),
        grid=(b, d // LANES, s // tc),
        in_specs=[tile] * 7 + [vec] * 3,
        out_specs=tile,
        out_shape=jax.ShapeDtypeStruct((b, s, d), F32),
        scratch_shapes=[pltpu.VMEM((LANES, LANES), F32)],
        compiler_params=_cparams(("parallel", "parallel", "arbitrary")), name="wkv7",
    )(r, k, v, lw, kk, a, g, lnx_w.reshape(1, d), lnx_b.reshape(1, d), r_k.reshape(1, d))


def _in_proj_weights(w_in):
    gw = GDN_HEADS * LANES
    c = 4 * gw
    w_qkv, w_z = w_in[:, 0:3 * gw], w_in[:, 3 * gw:c]
    d = w_in.shape[0]
    w_ba = jnp.zeros((d, LANES), w_in.dtype).at[:, :2 * GDN_HEADS].set(w_in[:, c:c + 2 * GDN_HEADS])
    c += 2 * GDN_HEADS
    nw = NSA_HEADS * NSA_DH
    w_q = w_in[:, c:c + nw].reshape(d, NSA_GROUPS, NSA_HPG, NSA_DH) * (NSA_DH ** -0.5)
    c += nw
    w_qpad = jnp.zeros((d, NSA_GROUPS, NSA_HPG, NSA_GROUPS, NSA_DH), w_in.dtype)
    for g in range(NSA_GROUPS):
        w_qpad = w_qpad.at[:, g, :, g, :].set(w_q[:, g])
    w_qpad = w_qpad.reshape(d, NSA_HEADS * LANES)
    kvw = NSA_GROUPS * NSA_DH
    w_cmp = w_in[:, c:c + 2 * kvw]
    w_kv4 = w_in[:, c + 2 * kvw:c + 6 * kvw]
    c += 6 * kvw
    w_gate = jnp.zeros((d, LANES), w_in.dtype).at[:, :3 * NSA_HEADS].set(w_in[:, c:c + 3 * NSA_HEADS])
    ws = [w_qkv, w_z, w_ba, w_qpad, w_cmp, w_kv4, w_gate]
    dts = [F32, F32, F32, BF16, F32, BF16, F32]
    return [w.astype(BF16) for w in ws], dts


def kernel(x, rel_bias, final_norm, e_attn_norm, e_w_in, e_conv_w, e_a_log, e_dt_bias, e_gdn_norm, e_cmpk_pe, e_cmpk_w1, e_cmpk_w2, e_cmpv_pe, e_cmpv_w1, e_cmpv_w2, e_w_out, e_ffn_norm, e_ffn_gate, e_ffn_up, e_ffn_down, o_attn_norm, o_mu, o_w_r, o_w_k, o_w_v, o_w_o, o_w0, o_w1, o_w2, o_a0, o_a1, o_a2, o_g1, o_g2, o_k_k, o_k_a, o_r_k, o_lnx_w, o_lnx_b, o_ffn_norm, o_ffn_gate, o_ffn_up, o_ffn_down):
    b, s, d = x.shape
    m = b * s
    xf = x.reshape(m, d)
    bf = lambda w: w.astype(BF16)

    ws, dts = _in_proj_weights(e_w_in[0])
    qkv, z, ba, qpad, kvc, kv4, gl = norm_linear(xf, e_attn_norm[0], ws, dts)
    sh = lambda t: t.reshape(b, s, t.shape[-1])
    o_gdn = gated_deltanet(sh(qkv), sh(z), sh(ba), e_conv_w[0], e_a_log[0], e_dt_bias[0], e_gdn_norm[0])
    kcvc = compress(sh(kvc), e_cmpk_pe[0], e_cmpk_w1[0], e_cmpk_w2[0], e_cmpv_pe[0], e_cmpv_w1[0], e_cmpv_w2[0])
    o_nsa = nsa_attention(sh(qpad), sh(gl), kcvc, sh(kv4), rel_bias)
    gw = GDN_HEADS * LANES
    xf = linear_res(xf, [o_gdn.reshape(m, -1), o_nsa.reshape(m, -1)], [bf(e_w_out[0][:gw]), bf(e_w_out[0][gw:])])
    xf = ffn(xf, e_ffn_norm[0], bf(e_ffn_gate[0]), bf(e_ffn_up[0]), bf(e_ffn_down[0]), final_norm, False)

    r, k, v, lw, kk, a, g = rwkv_proj(xf.reshape(b, s, d), o_attn_norm[0], o_mu[0], o_w_r[0], o_w_k[0], o_w_v[0],
                                      o_w0[0], o_w1[0], o_w2[0], o_a0[0], o_a1[0], o_a2[0], o_g1[0], o_g2[0],
                                      o_k_k[0], o_k_a[0])
    yg = wkv7(r, k, v, lw, kk, a, g, o_lnx_w[0], o_lnx_b[0], o_r_k[0].reshape(-1))
    xf = linear_res(xf, [yg.reshape(m, d)], [bf(o_w_o[0])])
    xf = ffn(xf, o_ffn_norm[0], bf(o_ffn_gate[0]), bf(o_ffn_up[0]), bf(o_ffn_down[0]), final_norm, True)
    return xf.reshape(b, s, d)
```

```python
import functools
import math

import jax
import jax.numpy as jnp
from jax import lax
from jax.experimental import pallas as pl
from jax.experimental.pallas import tpu as pltpu

F32 = jnp.float32
BF16 = jnp.bfloat16

VMEM_LIMIT_BYTES = 52 * 1024 * 1024
LANES = 128
CHUNK = 64

RMS_EPS = 1e-6
GDN_HEADS = 4
GDN_CONV = 4
NSA_HEADS = 8
NSA_GROUPS = 2
NSA_HPG = 4
NSA_DH = 64
CMP_LEN = 32
CMP_STRIDE = 16
SLC_LEN = 64
N_SELECT = 8
WINDOW = 512
NUM_BUCKETS = 32
REL_MAX_DIST = 128
NEG = -1e30
KV_PAD = 640
WIN_TILE = 640
RWKV_GN_EPS = 64e-5


def _cparams(sem):
    return pltpu.CompilerParams(dimension_semantics=sem, vmem_limit_bytes=VMEM_LIMIT_BYTES)


def _dot(a, b):
    return jnp.dot(a, b, preferred_element_type=F32)


def _dot_nt(a, b):
    return lax.dot_general(a, b, (((1,), (1,)), ((), ())), preferred_element_type=F32)


def _split(x, terms):
    out = []
    rem = x
    for _ in range(terms - 1):
        p = rem.astype(BF16)
        out.append(p)
        rem = rem - p.astype(F32)
    out.append(rem.astype(BF16))
    return out


def _mm(a, b, passes=1, nt=False):
    d = _dot_nt if nt else _dot
    if passes == 1:
        return d(a.astype(BF16), b.astype(BF16))
    ah, al = _split(a, 2)
    bh, bl = _split(b, 2)
    return d(ah, bh) + (d(ah, bl) + d(al, bh))


def _mm_exact_lhs(a01, b):
    a = a01.astype(BF16)
    b0, b1, b2 = _split(b, 3)
    return _dot(a, b0) + (_dot(a, b1) + _dot(a, b2))


def _mm_exact_rhs(a, b01):
    b = b01.astype(BF16)
    a0, a1, a2 = _split(a, 3)
    return _dot(a0, b) + (_dot(a1, b) + _dot(a2, b))


def _silu(x):
    return x * jax.nn.sigmoid(x)


def _softplus(x):
    return jnp.maximum(x, 0.0) + jnp.log1p(jnp.exp(-jnp.abs(x)))


def _lane_masks():
    lane = lax.broadcasted_iota(jnp.int32, (1, LANES), 1)
    m0 = (lane < 64).astype(F32)
    return m0, 1.0 - m0


def _stack2(x, m0, m1):
    return jnp.concatenate([x * m0, x * m1], axis=0)


def _tri_inv_cat(lcats, passes):
    m0, m1 = _lane_masks()
    i = lax.broadcasted_iota(jnp.int32, (CHUNK, LANES), 0)
    j = lax.broadcasted_iota(jnp.int32, (CHUNK, LANES), 1) & (CHUNK - 1)
    eye = (i == j).astype(F32)
    blk16 = (i >> 4) == (j >> 4)
    st = lambda x: _stack2(x, m0, m1)
    mm = lambda a, b: _mm(a, b, passes)
    each = lambda f, *ls: [f(*xs) for xs in zip(*ls)]
    ld = each(lambda l: jnp.where(blk16, l, 0.0), lcats)
    off = each(lambda l: jnp.where(blk16, 0.0, l), lcats)
    sld = each(st, ld)
    l2 = each(mm, ld, sld)
    sl2 = each(st, l2)
    n1 = each(lambda d, l, s: ((eye - d) + l) - mm(d, s), ld, l2, sl2)
    l4 = each(mm, l2, sl2)
    sl4 = each(st, l4)
    n2 = each(lambda a, s: a + mm(a, s), n1, sl4)
    l8 = each(mm, l4, sl4)
    x = each(lambda a, l: a + mm(a, st(l)), n2, l8)
    sx = each(st, x)
    mt = each(lambda a, o: mm(a, st(o)), x, off)
    m2 = each(lambda a: mm(a, st(a)), mt)
    mx = each(mm, mt, sx)
    m2x = each(mm, m2, sx)
    m3x = each(lambda a, b: mm(a, st(b)), m2, mx)
    return each(lambda a, b, c, d: (a - b) + (c - d), x, mx, m2x, m3x)


def _tri_ones(n, lower, block=None):
    i = lax.broadcasted_iota(jnp.int32, (n, n), 0)
    j = lax.broadcasted_iota(jnp.int32, (n, n), 1)
    m = (i >= j) if lower else (i <= j)
    if block is not None:
        sh = block.bit_length() - 1
        m = jnp.logical_and(m, (i >> sh) == (j >> sh))
    return m.astype(F32)


def _norm_linear_kernel(x_ref, nw_ref, *refs, n_out):
    w_refs, o_refs = refs[:n_out], refs[n_out:]
    x = x_ref[...]
    h = x * lax.rsqrt(jnp.mean(x * x, axis=-1, keepdims=True) + RMS_EPS) * nw_ref[...]
    hb = h.astype(BF16)
    for w_ref, o_ref in zip(w_refs, o_refs):
        o_ref[...] = _dot(hb, w_ref[...]).astype(o_ref.dtype)


def norm_linear(x, nw, weights, out_dtypes, tm=512):
    m, k = x.shape
    n_out = len(weights)
    in_specs = [pl.BlockSpec((tm, k), lambda i: (i, 0)), pl.BlockSpec((1, k), lambda i: (0, 0))]
    in_specs += [pl.BlockSpec(w.shape, lambda i: (0, 0)) for w in weights]
    out_specs = [pl.BlockSpec((tm, w.shape[1]), lambda i: (i, 0)) for w in weights]
    out_shape = [jax.ShapeDtypeStruct((m, w.shape[1]), dt) for w, dt in zip(weights, out_dtypes)]
    return pl.pallas_call(
        functools.partial(_norm_linear_kernel, n_out=n_out),
        grid=(m // tm,), in_specs=in_specs, out_specs=out_specs, out_shape=out_shape,
        compiler_params=_cparams(("parallel",)), name="norm_linear",
    )(x, nw.reshape(1, k), *weights)


def _linear_res_kernel(res_ref, *refs, n_in):
    a_refs, w_refs, o_ref = refs[:n_in], refs[n_in:2 * n_in], refs[2 * n_in]
    acc = res_ref[...]
    for a_ref, w_ref in zip(a_refs, w_refs):
        acc = acc + _dot(a_ref[...].astype(BF16), w_ref[...])
    o_ref[...] = acc


def linear_res(res, acts, weights, tm=512):
    m, n = res.shape
    n_in = len(acts)
    in_specs = [pl.BlockSpec((tm, n), lambda i: (i, 0))]
    in_specs += [pl.BlockSpec((tm, a.shape[1]), lambda i: (i, 0)) for a in acts]
    in_specs += [pl.BlockSpec(w.shape, lambda i: (0, 0)) for w in weights]
    return pl.pallas_call(
        functools.partial(_linear_res_kernel, n_in=n_in),
        grid=(m // tm,), in_specs=in_specs, out_specs=pl.BlockSpec((tm, n), lambda i: (i, 0)),
        out_shape=jax.ShapeDtypeStruct((m, n), F32),
        compiler_params=_cparams(("parallel",)), name="linear_res",
    )(res, *acts, *weights)


def _ffn_kernel(x_ref, nw_ref, wg_ref, wu_ref, wd_ref, fw_ref, o_ref, xn_scr, acc_scr, *, final_norm):
    f = pl.program_id(1)

    @pl.when(f == 0)
    def _():
        x = x_ref[...]
        h = x * lax.rsqrt(jnp.mean(x * x, axis=-1, keepdims=True) + RMS_EPS) * nw_ref[...]
        xn_scr[...] = h.astype(BF16)
        acc_scr[...] = jnp.zeros_like(acc_scr)

    xn = xn_scr[...]
    g = _dot(xn, wg_ref[...])
    u = _dot(xn, wu_ref[...])
    hid = (_silu(g) * u).astype(BF16)
    acc_scr[...] += _dot(hid, wd_ref[...])

    @pl.when(f == pl.num_programs(1) - 1)
    def _():
        y = x_ref[...] + acc_scr[...]
        if final_norm:
            y = y * lax.rsqrt(jnp.mean(y * y, axis=-1, keepdims=True) + RMS_EPS) * fw_ref[...]
        o_ref[...] = y


def ffn(x, nw, wg, wu, wd, fw, final_norm, tm=512, tf=1408):
    m, d = x.shape
    dff = wg.shape[1]
    return pl.pallas_call(
        functools.partial(_ffn_kernel, final_norm=final_norm),
        grid=(m // tm, dff // tf),
        in_specs=[pl.BlockSpec((tm, d), lambda i, f: (i, 0)),
                  pl.BlockSpec((1, d), lambda i, f: (0, 0)),
                  pl.BlockSpec((d, tf), lambda i, f: (0, f)),
                  pl.BlockSpec((d, tf), lambda i, f: (0, f)),
                  pl.BlockSpec((tf, d), lambda i, f: (f, 0)),
                  pl.BlockSpec((1, d), lambda i, f: (0, 0))],
        out_specs=pl.BlockSpec((tm, d), lambda i, f: (i, 0)),
        out_shape=jax.ShapeDtypeStruct((m, d), F32),
        scratch_shapes=[pltpu.VMEM((tm, d), BF16), pltpu.VMEM((tm, d), F32)],
        compiler_params=_cparams(("parallel", "arbitrary")), name="ffn",
    )(x, nw.reshape(1, d), wg, wu, wd, fw.reshape(1, d))


def _gdn_kernel(q_ref, k_ref, v_ref, qh_ref, kh_ref, vh_ref, cwq_ref, cwk_ref, cwv_ref,
                z_ref, ba_ref, bat_ref, hp_ref, nw_ref, o_ref, scr, state, *, tc, inv_passes):
    h = pl.program_id(1)
    t = pl.program_id(2)

    @pl.when(t == 0)
    def _():
        state[...] = jnp.zeros_like(state)

    not_first = (t > 0).astype(F32)

    def conv(x_ref, halo_ref, w_ref, slot):
        scr[slot, 0:8, :] = halo_ref[0] * not_first
        scr[slot, 8:8 + tc, :] = x_ref[0]
        w = w_ref[...]
        y = (w[3:4] * scr[slot, 8:8 + tc, :] + w[2:3] * scr[slot, 7:7 + tc, :]
             + w[1:2] * scr[slot, 6:6 + tc, :] + w[0:1] * scr[slot, 5:5 + tc, :])
        return _silu(y)

    q = conv(q_ref, qh_ref, cwq_ref, 0)
    k = conv(k_ref, kh_ref, cwk_ref, 1)
    v = conv(v_ref, vh_ref, cwv_ref, 2)
    qn = q * lax.rsqrt(jnp.sum(q * q, axis=-1, keepdims=True) + 1e-6)
    kn = k * lax.rsqrt(jnp.sum(k * k, axis=-1, keepdims=True) + 1e-6)

    lane = lax.broadcasted_iota(jnp.int32, (1, LANES), 1)
    ba = ba_ref[0]
    b_col = jnp.sum(jnp.where(lane == h, ba, 0.0), axis=-1, keepdims=True)
    a_col = jnp.sum(jnp.where(lane == h + GDN_HEADS, ba, 0.0), axis=-1, keepdims=True)
    beta = jax.nn.sigmoid(b_col)
    a_log = hp_ref[0, 0:1, :]
    dt_b = hp_ref[0, 1:2, :]
    g_rep = -jnp.exp(a_log) * _softplus(a_col + dt_b)

    m0, m1 = _lane_masks()
    i64 = lax.broadcasted_iota(jnp.int32, (CHUNK, LANES), 0)
    j64 = lax.broadcasted_iota(jnp.int32, (CHUNK, LANES), 1) & (CHUNK - 1)
    low_half = lax.broadcasted_iota(jnp.int32, (CHUNK, LANES), 1) < CHUNK
    ltri_bd = _tri_ones(2 * CHUNK, True, CHUNK)
    utri_bd = _tri_ones(2 * CHUNK, False, CHUNK)
    eye128 = (lax.broadcasted_iota(jnp.int32, (LANES, LANES), 0)
              == lax.broadcasted_iota(jnp.int32, (LANES, LANES), 1)).astype(F32)
    zeros64 = jnp.zeros((CHUNK, LANES), F32)
    scale = 1.0 / math.sqrt(LANES)
    nw = nw_ref[...]

    npair = tc // (2 * CHUNK)
    each = lambda f, *ls: [f(*xs) for xs in zip(*ls)]
    halves = lambda x: jnp.where(low_half, x[0:CHUNK], x[CHUNK:2 * CHUNK])
    prow = [slice(2 * CHUNK * p, 2 * CHUNK * (p + 1)) for p in range(npair)]
    kp = [kn[rs] for rs in prow]
    qp = [qn[rs] * scale for rs in prow]
    vp = [v[rs] for rs in prow]
    beta_p = [beta[rs] for rs in prow]
    gc_col = [_mm_exact_lhs(ltri_bd, g_rep[rs]) for rs in prow]
    g_row = [-jnp.exp(a_log[:, 0:1]) * _softplus(bat_ref[0, 0, p, 1:2, :] + dt_b[:, 0:1]) for p in range(npair)]
    gc_row = each(lambda x: _mm_exact_rhs(jnp.broadcast_to(x, (CHUNK, LANES)), utri_bd), g_row)
    dcat = each(lambda c_, r_: jnp.exp(jnp.where(i64 >= j64, halves(c_) - r_, -jnp.inf)), gc_col, gc_row)
    kb = each(lambda k_, b_: k_ * b_, kp, beta_p)
    gram = each(lambda x, y_: _mm(x, y_, nt=True), kb, kp)
    gq = each(lambda x, y_: _mm(x, y_, nt=True), qp, kp)
    strict = each(lambda x, d_: jnp.where(i64 > j64, halves(x) * d_, 0.0), gram, dcat)
    qk_bd = each(lambda x, d_: _stack2(jnp.where(i64 >= j64, halves(x) * d_, 0.0), m0, m1), gq, dcat)
    eg = each(jnp.exp, gc_col)
    g_last = each(lambda x: [x[CHUNK * (c + 1) - 1:CHUNK * (c + 1), :] for c in range(2)], gc_col)
    kdt = each(lambda k_, gl_, gc_: jnp.concatenate(
        [k_[CHUNK * c:CHUNK * (c + 1)] * jnp.exp(gl_[c] - gc_[CHUNK * c:CHUNK * (c + 1)]) for c in range(2)],
        axis=0).T, kp, g_last, gc_col)
    tinv = _tri_inv_cat(strict, inv_passes)
    uw = each(lambda t_, v_, b_, kb_, e_: _mm(_stack2(t_, m0, m1), jnp.concatenate([v_ * b_, kb_ * e_], axis=1)),
              tinv, vp, beta_p, kb, eg)
    u = each(lambda x: x[:, 0:LANES], uw)
    w = each(lambda x: x[:, LANES:2 * LANES], uw)
    q_hat = each(lambda q_, e_, m_, w_: q_ * e_ - _mm(m_, w_), qp, eg, qk_bd, w)
    o_hat = each(_mm, qk_bd, u)
    pad = lambda x, c: jnp.concatenate([x[0:CHUNK], zeros64] if c == 0 else [zeros64, x[CHUNK:2 * CHUNK]], axis=0)
    trans = [eye128 * jnp.exp(g_last[p][c]) - _mm(kdt[p], pad(w[p], c)) for p in range(npair) for c in range(2)]
    h_add = [_mm(kdt[p], pad(u[p], c)) for p in range(npair) for c in range(2)]
    s_state = state[...]
    for p in range(npair):
        for c in range(2):
            cr = slice(CHUNK * c, CHUNK * (c + 1))
            o = _mm(q_hat[p][cr], s_state) + o_hat[p][cr]
            s_state = _mm(trans[2 * p + c], s_state, passes=3) + h_add[2 * p + c]
            on = o * lax.rsqrt(jnp.mean(o * o, axis=-1, keepdims=True) + RMS_EPS) * nw
            r0 = 2 * CHUNK * p + CHUNK * c
            o_ref[0, r0:r0 + CHUNK, :] = on * _silu(z_ref[0, r0:r0 + CHUNK, :])
    state[...] = s_state


def gated_deltanet(qkv, z, ba, conv_w, a_log, dt_bias, norm_w, tc=512, inv_passes=3):
    b, s, _ = qkv.shape
    hh = GDN_HEADS
    bat = ba[..., :2 * hh].reshape(b, s // 128, 128, 2, hh).transpose(0, 4, 1, 3, 2)
    hp = jnp.zeros((hh, 8, LANES), F32)
    hp = hp.at[:, 0, :].set(a_log.astype(F32)[:, None]).at[:, 1, :].set(dt_bias.astype(F32)[:, None])
    cw = jnp.zeros((8, 3 * hh * LANES), F32).at[:GDN_CONV].set(conv_w.astype(F32))
    hb = tc // 8
    blk = lambda off: pl.BlockSpec((1, tc, LANES), lambda bi, hi, ti: (bi, ti, hi + off))
    halo = lambda off: pl.BlockSpec((1, 8, LANES), lambda bi, hi, ti: (bi, jnp.maximum(ti * hb - 1, 0), hi + off))
    cws = lambda off: pl.BlockSpec((8, LANES), lambda bi, hi, ti: (0, hi + off))
    return pl.pallas_call(
        functools.partial(_gdn_kernel, tc=tc, inv_passes=inv_passes),
        grid=(b, hh, s // tc),
        in_specs=[blk(0), blk(hh), blk(2 * hh), halo(0), halo(hh), halo(2 * hh), cws(0), cws(hh), cws(2 * hh),
                  blk(0),
                  pl.BlockSpec((1, tc, LANES), lambda bi, hi, ti: (bi, ti, 0)),
                  pl.BlockSpec((1, 1, tc // 128, 2, LANES), lambda bi, hi, ti: (bi, hi, ti, 0, 0)),
                  pl.BlockSpec((1, 8, LANES), lambda bi, hi, ti: (hi, 0, 0)),
                  pl.BlockSpec((1, LANES), lambda bi, hi, ti: (0, 0))],
        out_specs=blk(0),
        out_shape=jax.ShapeDtypeStruct((b, s, hh * LANES), F32),
        scratch_shapes=[pltpu.VMEM((3, tc + 8, LANES), F32), pltpu.VMEM((LANES, LANES), F32)],
        compiler_params=_cparams(("parallel", "parallel", "arbitrary")), name="gated_deltanet",
    )(qkv, qkv, qkv, qkv, qkv, qkv, cw, cw, cw, z, ba, bat, hp, norm_w.astype(F32).reshape(1, LANES))


def _compress_kernel(x_ref, pe_ref, w1_ref, w2_ref, o_ref):
    nb = x_ref.shape[3]
    acc = jnp.zeros((nb, LANES), F32)
    for g in range(NSA_GROUPS):
        x = x_ref[0, 0, g]
        first = _dot((x + pe_ref[0, 0]).astype(BF16), w1_ref[0, 0])
        second = _dot((x + pe_ref[0, 1]).astype(BF16), w1_ref[0, 1])
        hid = _silu(first + pltpu.roll(second, nb - 1, 0))
        acc = acc + _dot(hid.astype(BF16), w2_ref[0, g])
    o_ref[0, 0] = acc.astype(o_ref.dtype)


def compress(kvc, pe_k, w1_k, w2_k, pe_v, w1_v, w2_v):
    b, s, _ = kvc.shape
    nb = s // CMP_STRIDE
    width = CMP_STRIDE * NSA_DH
    x = kvc.reshape(b, nb, CMP_STRIDE, 2, NSA_GROUPS, NSA_DH).transpose(0, 3, 4, 1, 2, 5).reshape(b, 2, NSA_GROUPS, nb, width)
    pe = jnp.stack([pe_k, pe_v]).astype(F32).reshape(2, 2, 1, width)
    w1 = jnp.stack([w1_k, w1_v]).astype(BF16).reshape(2, 2, width, -1)
    hid = w1.shape[-1]
    w2 = jnp.stack([w2_k, w2_v]).astype(BF16)
    w2p = jnp.zeros((2, NSA_GROUPS, hid, LANES), BF16)
    for g in range(NSA_GROUPS):
        w2p = w2p.at[:, g, :, NSA_DH * g:NSA_DH * (g + 1)].set(w2)
    return pl.pallas_call(
        _compress_kernel,
        grid=(b, 2),
        in_specs=[pl.BlockSpec((1, 1, NSA_GROUPS, nb, width), lambda bi, j: (bi, j, 0, 0, 0)),
                  pl.BlockSpec((1, 2, 1, width), lambda bi, j: (j, 0, 0, 0)),
                  pl.BlockSpec((1, 2, width, hid), lambda bi, j: (j, 0, 0, 0)),
                  pl.BlockSpec((1, NSA_GROUPS, hid, LANES), lambda bi, j: (j, 0, 0, 0))],
        out_specs=pl.BlockSpec((1, 1, nb, LANES), lambda bi, j: (bi, j, 0, 0)),
        out_shape=jax.ShapeDtypeStruct((b, 2, nb, LANES), BF16),
        compiler_params=_cparams(("parallel", "parallel")), name="nsa_compress",
    )(x, pe, w1, w2p)


def _softmax_pv(s, v):
    m = jnp.max(s, axis=-1, keepdims=True)
    p = jnp.exp(s - m)
    l = jnp.sum(p, axis=-1, keepdims=True)
    return _dot(p.astype(BF16), v) / l


def _nsa_kernel(q_ref, gl_ref, kcvc_ref, kv_ref, bw_ref, bfar_ref, bc_ref, ovl_ref, o_ref):
    n = pl.program_id(1)
    qb_rows = q_ref.shape[1]
    ns = (kv_ref.shape[1] - KV_PAD) // SLC_LEN
    far_w = 8 * SLC_LEN
    lane = lax.broadcasted_iota(jnp.int32, (qb_rows, LANES), 1)
    lane_s = lane & (ns - 1)
    gates = jax.nn.sigmoid(gl_ref[0])
    kc = kcvc_ref[0, 0]
    vc = kcvc_ref[0, 1]
    ovl = ovl_ref[...]
    row_blk = lax.broadcasted_iota(jnp.int32, (LANES, far_w), 0)
    col_blk = lax.broadcasted_iota(jnp.int32, (LANES, far_w), 1) >> 6
    wcol = lax.broadcasted_iota(jnp.int32, (NSA_HPG * qb_rows, WIN_TILE), 1)
    groups = range(NSA_GROUPS)
    tile4 = lambda x: jnp.concatenate([x] * NSA_HPG, axis=0)

    qs = [jnp.concatenate([q_ref[0, :, LANES * (NSA_HPG * g + hh):LANES * (NSA_HPG * g + hh + 1)]
                           for hh in range(NSA_HPG)], axis=0) for g in groups]
    bc = [bc_ref[NSA_HPG * g:NSA_HPG * (g + 1)].reshape(NSA_HPG * qb_rows, LANES) for g in groups]
    start = pl.multiple_of(n * SLC_LEN + (KV_PAD - 3 * SLC_LEN), SLC_LEN)
    k_near = kv_ref[0, pl.ds(start, 4 * SLC_LEN), 0:LANES]
    v_near = kv_ref[0, pl.ds(start, 4 * SLC_LEN), LANES:2 * LANES]
    wst = pl.multiple_of(n * SLC_LEN + (KV_PAD + SLC_LEN - WIN_TILE), SLC_LEN)
    k_w = kv_ref[0, pl.ds(wst, WIN_TILE), 2 * LANES:3 * LANES]
    v_w = kv_ref[0, pl.ds(wst, WIN_TILE), 3 * LANES:4 * LANES]
    s_cmp = [_dot_nt(qs[g], kc) + bc[g] for g in groups]
    s_near = [_dot_nt(qs[g], k_near) + bw_ref[g, :, WIN_TILE - 4 * SLC_LEN:WIN_TILE] for g in groups]
    s_win = [jnp.where(wcol >= (WIN_TILE - SLC_LEN) - SLC_LEN * n, _dot_nt(qs[g], k_w) + bw_ref[g], NEG)
             for g in groups]

    p_cmp = []
    for g in groups:
        m = jnp.max(s_cmp[g], axis=-1, keepdims=True)
        e = jnp.where(bc[g] > 0.5 * NEG, jnp.exp(s_cmp[g] - m), 0.0)
        p_cmp.append(e / jnp.maximum(jnp.sum(e, axis=-1, keepdims=True), 1e-30))
    o_cmp = [_dot(p.astype(BF16), vc) for p in p_cmp]
    imp = [_mm_exact_rhs(p[0:qb_rows] + p[qb_rows:2 * qb_rows] + p[2 * qb_rows:3 * qb_rows] + p[3 * qb_rows:4 * qb_rows],
                         ovl) for p in p_cmp]

    forced = jnp.logical_or(jnp.logical_or(lane_s == 0, lane_s == n), lane_s == n - 1)
    val = [jnp.where(lane_s > n, -1.0, jnp.where(forced, 1e6, x)) for x in imp]
    rank = [jnp.zeros((qb_rows, LANES), F32) for _ in groups]
    for r in range(1, ns):
        tie = (lane_s >= r).astype(F32)
        for g in groups:
            other = pltpu.roll(val[g], r, 1)
            rank[g] = rank[g] + jnp.where(other > val[g], 1.0, jnp.where(other == val[g], tie, 0.0))
    sel = [jnp.where(jnp.logical_and(x < N_SELECT, lane < ns), 1.0, 0.0) for x in rank]

    o_win = [_softmax_pv(s_win[g], v_w) for g in groups]

    e_near = (lax.broadcasted_iota(jnp.int32, (LANES, 4 * SLC_LEN), 0)
              == (lax.broadcasted_iota(jnp.int32, (LANES, 4 * SLC_LEN), 1) >> 6) + (n - 3)).astype(BF16)
    carry = []
    for g in groups:
        sm = _dot(sel[g].astype(BF16), e_near)
        s = jnp.where(tile4(sm) > 0.5, s_near[g], NEG)
        m = jnp.max(s, axis=-1, keepdims=True)
        p = jnp.exp(s - m)
        carry += [m, jnp.sum(p, axis=-1, keepdims=True), _dot(p.astype(BF16), v_near)]
    sel_far = [jnp.where(lane <= n - 4, x, 0.0).astype(BF16) for x in sel]
    b_far = [jnp.concatenate([bfar_ref[g]] * (far_w // LANES), axis=1) for g in groups]

    def far_body(it, carry):
        st = pl.multiple_of(KV_PAD + it * far_w, math.gcd(KV_PAD, far_w))
        k_c = kv_ref[0, pl.ds(st, far_w), 0:LANES]
        v_c = kv_ref[0, pl.ds(st, far_w), LANES:2 * LANES]
        e_c = (row_blk == col_blk + 8 * it).astype(BF16)
        sc = [_dot_nt(qs[g], k_c) + b_far[g] for g in groups]
        smc = [_dot(sel_far[g], e_c) for g in groups]
        out = []
        for g in groups:
            m_i, l_i, acc_i = carry[3 * g:3 * g + 3]
            s_g = jnp.where(tile4(smc[g]) > 0.5, sc[g], NEG)
            m_n = jnp.maximum(m_i, jnp.max(s_g, axis=-1, keepdims=True))
            alpha = jnp.exp(m_i - m_n)
            pc = jnp.exp(s_g - m_n)
            out += [m_n, alpha * l_i + jnp.sum(pc, axis=-1, keepdims=True),
                    alpha * acc_i + _dot(pc.astype(BF16), v_c)]
        return tuple(out)

    carry = lax.fori_loop(0, (n // 4 + 1) // 2, far_body, tuple(carry))
    o_slc = [carry[3 * g + 2] / carry[3 * g + 1] for g in groups]

    for g in groups:
        pieces = []
        for hh in range(NSA_HPG):
            hd = NSA_HPG * g + hh
            rs = slice(qb_rows * hh, qb_rows * (hh + 1))
            pieces.append(gates[:, 3 * hd:3 * hd + 1] * o_cmp[g][rs] + gates[:, 3 * hd + 1:3 * hd + 2] * o_slc[g][rs]
                          + gates[:, 3 * hd + 2:3 * hd + 3] * o_win[g][rs])
        for hp in range(2):
            lo, hi = pieces[2 * hp], pieces[2 * hp + 1]
            if g == 0:
                hi = pltpu.roll(hi, NSA_DH, 1)
            else:
                lo = pltpu.roll(lo, NSA_DH, 1)
            o_ref[0, :, LANES * (2 * g + hp):LANES * (2 * g + hp + 1)] = jnp.where(lane < NSA_DH, lo, hi)


def _rel_bucket(dist):
    n = jnp.maximum(dist, 0)
    exact = NUM_BUCKETS // 2
    nf = jnp.maximum(n, 1).astype(F32)
    large = exact + (jnp.log(nf / exact) / math.log(REL_MAX_DIST / exact) * (NUM_BUCKETS - exact)).astype(jnp.int32)
    return jnp.where(n < exact, n, jnp.minimum(large, NUM_BUCKETS - 1))


def nsa_attention(qpad, gl, kcvc, kv4, rel_bias):
    b, s, _ = qpad.shape
    qb = SLC_LEN
    nq = s // qb
    nc = s // CMP_STRIDE - CMP_LEN // CMP_STRIDE + 1
    ns = s // SLC_LEN
    table = rel_bias.astype(F32)
    dist_w = jnp.arange(qb)[:, None] + (WIN_TILE - SLC_LEN) - jnp.arange(WIN_TILE)[None, :]
    ok_w = jnp.logical_and(dist_w >= 0, dist_w < WINDOW)
    bw = jnp.where(ok_w[None], table[_rel_bucket(dist_w)].transpose(2, 0, 1), NEG)
    bw = bw.reshape(NSA_GROUPS, NSA_HPG * qb, WIN_TILE)
    far_val = table[NUM_BUCKETS - 1]
    bfar = jnp.broadcast_to(far_val[:, None, None], (NSA_HEADS, qb, LANES)).reshape(NSA_GROUPS, NSA_HPG * qb, LANES)
    cmp_end = jnp.arange(LANES) * CMP_STRIDE + CMP_LEN - 1
    dist_c = jnp.arange(s)[:, None] - cmp_end[None, :]
    ok_c = jnp.logical_and(dist_c >= 0, jnp.arange(LANES)[None, :] < nc)
    bc = jnp.where(ok_c[None], table[_rel_bucket(dist_c)].transpose(2, 0, 1), NEG)
    cmp_start = jnp.arange(LANES) * CMP_STRIDE
    slc_j = jnp.arange(LANES) % ns
    ovl = jnp.logical_and(jnp.logical_and(cmp_start[:, None] < (slc_j[None, :] + 1) * SLC_LEN,
                                          cmp_end[:, None] >= slc_j[None, :] * SLC_LEN),
                          jnp.arange(LANES)[:, None] < nc).astype(F32)
    kvp = jnp.pad(kv4, ((0, 0), (KV_PAD, 0), (0, 0)))
    sp = s + KV_PAD
    return pl.pallas_call(
        _nsa_kernel,
        grid=(b, nq),
        in_specs=[pl.BlockSpec((1, qb, NSA_HEADS * LANES), lambda bi, n: (bi, n, 0)),
                  pl.BlockSpec((1, qb, LANES), lambda bi, n: (bi, n, 0)),
                  pl.BlockSpec((1, 2, s // CMP_STRIDE, LANES), lambda bi, n: (bi, 0, 0, 0)),
                  pl.BlockSpec((1, sp, 4 * LANES), lambda bi, n: (bi, 0, 0)),
                  pl.BlockSpec((NSA_GROUPS, NSA_HPG * qb, WIN_TILE), lambda bi, n: (0, 0, 0)),
                  pl.BlockSpec((NSA_GROUPS, NSA_HPG * qb, LANES), lambda bi, n: (0, 0, 0)),
                  pl.BlockSpec((NSA_HEADS, qb, LANES), lambda bi, n: (0, n, 0)),
                  pl.BlockSpec((LANES, LANES), lambda bi, n: (0, 0))],
        out_specs=pl.BlockSpec((1, qb, NSA_HEADS * NSA_DH), lambda bi, n: (bi, n, 0)),
        out_shape=jax.ShapeDtypeStruct((b, s, NSA_HEADS * NSA_DH), F32),
        compiler_params=_cparams(("parallel", "arbitrary")), name="nsa_attention",
    )(qpad, gl, kcvc, kvp, bw, bfar, bc, ovl)


def _rwkv_proj_kernel(x_ref, halo_ref, nw_ref, mu_ref, vec_ref, wr_ref, wk_ref, wv_ref,
                      w1_ref, w2_ref, a1_ref, a2_ref, g1_ref, g2_ref,
                      r_ref, k_ref, v_ref, lw_ref, kk_ref, a_ref, g_ref, scr, *, tm):
    t = pl.program_id(1)
    nw = nw_ref[...]

    def norm(x):
        return x * lax.rsqrt(jnp.mean(x * x, axis=-1, keepdims=True) + RMS_EPS) * nw

    h = norm(x_ref[0])
    scr[0:8, :] = norm(halo_ref[0]) * (t > 0).astype(F32)
    scr[8:8 + tm, :] = h
    xx = scr[7:7 + tm, :] - h
    mix = lambda j: (h + xx * mu_ref[j:j + 1, :]).astype(BF16)
    w0, a0, k_k, k_a = vec_ref[0:1, :], vec_ref[1:2, :], vec_ref[2:3, :], vec_ref[3:4, :]
    r = _dot(mix(0), wr_ref[...])
    wl = w0 + _dot(jnp.tanh(_dot(mix(1), w1_ref[...])).astype(BF16), w2_ref[...])
    k = _dot(mix(2), wk_ref[...])
    v = _dot(mix(3), wv_ref[...])
    a = jax.nn.sigmoid(a0 + _dot(_dot(mix(4), a1_ref[...]).astype(BF16), a2_ref[...]))
    g = _dot(jax.nn.sigmoid(_dot(mix(5), g1_ref[...])).astype(BF16), g2_ref[...])
    w_raw = -_softplus(-wl) - 0.5
    r_ref[0] = r
    k_ref[0] = k * (1.0 + (a - 1.0) * k_a)
    v_ref[0] = v
    lw_ref[0] = -jnp.exp(w_raw)
    kk_ref[0] = k * k_k
    a_ref[0] = a
    g_ref[0] = g


def rwkv_proj(x, nw, mu, w_r, w_k, w_v, w0, w1, w2, a0, a1, a2, g1, g2, k_k, k_a, tm=256):
    b, s, d = x.shape
    pad_to = lambda w, rows, cols: jnp.zeros((rows, cols), BF16).at[:w.shape[0], :w.shape[1]].set(w.astype(BF16))
    lora = lambda n: -(-n // LANES) * LANES
    mu8 = jnp.zeros((8, d), F32).at[:6].set(mu.astype(F32))
    vec = jnp.zeros((8, d), F32).at[0].set(w0).at[1].set(a0).at[2].set(k_k).at[3].set(k_a)
    big = [w.astype(BF16) for w in (w_r, w_k, w_v)]
    small = [pad_to(w1, d, lora(w1.shape[1])), pad_to(w2, lora(w2.shape[0]), d),
             pad_to(a1, d, lora(a1.shape[1])), pad_to(a2, lora(a2.shape[0]), d),
             pad_to(g1, d, lora(g1.shape[1])), pad_to(g2, lora(g2.shape[0]), d)]
    whole = lambda w: pl.BlockSpec(w.shape, lambda bi, ti: (0, 0))
    hb = tm // 8
    tile = pl.BlockSpec((1, tm, d), lambda bi, ti: (bi, ti, 0))
    return pl.pallas_call(
        functools.partial(_rwkv_proj_kernel, tm=tm),
        grid=(b, s // tm),
        in_specs=[tile, pl.BlockSpec((1, 8, d), lambda bi, ti: (bi, jnp.maximum(ti * hb - 1, 0), 0)),
                  pl.BlockSpec((1, d), lambda bi, ti: (0, 0)), whole(mu8), whole(vec)]
                 + [whole(w) for w in big] + [whole(w) for w in small],
        out_specs=[tile] * 7,
        out_shape=[jax.ShapeDtypeStruct((b, s, d), F32)] * 7,
        scratch_shapes=[pltpu.VMEM((tm + 8, d), F32)],
        compiler_params=_cparams(("parallel", "arbitrary")), name="rwkv_proj",
    )(x, x, nw.reshape(1, d), mu8, vec, *big, *small)


def _wkv_kernel(r_ref, k_ref, v_ref, lw_ref, kk_ref, a_ref, g_ref, lnw_ref, lnb_ref, rk_ref,
                o_ref, state, *, tc, inv_passes):
    t = pl.program_id(2)

    @pl.when(t == 0)
    def _():
        state[...] = jnp.zeros_like(state)

    m0, m1 = _lane_masks()
    lane = lax.broadcasted_iota(jnp.int32, (1, LANES), 1)
    low = lane < 64

    def seg_sum(x):
        return jnp.where(low, jnp.sum(x * m0, axis=-1, keepdims=True), jnp.sum(x * m1, axis=-1, keepdims=True))

    r_all, k_all, v_all = r_ref[0], k_ref[0], v_ref[0]
    kk = kk_ref[0]
    kk = kk * lax.rsqrt(seg_sum(kk * kk) + 1e-12)
    a_all = -kk
    b_all = kk * a_ref[0]

    i64 = lax.broadcasted_iota(jnp.int32, (CHUNK, LANES), 0)
    j64 = lax.broadcasted_iota(jnp.int32, (CHUNK, LANES), 1) & (CHUNK - 1)
    strict_m = i64 > j64
    incl_m = i64 >= j64
    ltri = _tri_ones(CHUNK, True)
    eye128 = (lax.broadcasted_iota(jnp.int32, (LANES, LANES), 0)
              == lax.broadcasted_iota(jnp.int32, (LANES, LANES), 1)).astype(F32)
    bd = ((lax.broadcasted_iota(jnp.int32, (LANES, LANES), 0) >> 6)
          == (lax.broadcasted_iota(jnp.int32, (LANES, LANES), 1) >> 6))
    zeros64 = jnp.zeros((CHUNK, LANES), F32)
    st = lambda x: _stack2(x, m0, m1)

    nch = tc // CHUNK
    each = lambda f, *ls: [f(*xs) for xs in zip(*ls)]
    rows = [slice(CHUNK * c, CHUNK * (c + 1)) for c in range(nch)]
    lw = [lw_ref[0, rs, :] for rs in rows]
    r, k, v = ([x[rs] for rs in rows] for x in (r_all, k_all, v_all))
    av, bv = ([x[rs] for rs in rows] for x in (a_all, b_all))
    cs = each(lambda x: _mm_exact_lhs(ltri, x), lw)
    cl = each(lambda x: x[CHUNK - 1:CHUNK, :], cs)
    a_t = each(lambda a_, c_, l_: a_ * jnp.exp(c_ - l_), av, cs, lw)
    r_t = each(lambda r_, c_: r_ * jnp.exp(c_), r, cs)
    inv_w = each(lambda c_: jnp.exp(-c_), cs)
    b_t = each(lambda x, w_: x * w_, bv, inv_w)
    k_t = each(lambda x, w_: x * w_, k, inv_w)
    to_end = each(lambda e_, c_: jnp.exp(e_ - c_), cl, cs)
    bkt = each(lambda b_, k_, e_: jnp.concatenate([b_ * e_, k_ * e_], axis=0).T, bv, k, to_end)
    ar = each(lambda a_, r_: jnp.concatenate([a_, r_], axis=0), a_t, r_t)
    gb = each(lambda x, y_: _mm(x, st(y_), nt=True), ar, b_t)
    gk = each(lambda x, y_: _mm(x, st(y_), nt=True), ar, k_t)
    l_ab = each(lambda x: jnp.where(strict_m, -x[0:CHUNK], 0.0), gb)
    l_ak = each(lambda x: jnp.where(strict_m, x[0:CHUNK], 0.0), gk)
    m_rb = each(lambda x: jnp.where(incl_m, x[CHUNK:2 * CHUNK], 0.0), gb)
    m_rk = each(lambda x: jnp.where(incl_m, x[CHUNK:2 * CHUNK], 0.0), gk)
    vst = each(st, v)
    lakv = each(_mm, l_ak, vst)
    tinv = _tri_inv_cat(l_ab, inv_passes)
    p_mat = each(lambda t_, a_: _mm(t_, st(a_)), tinv, a_t)
    q_mat = each(lambda t_, x: _mm(t_, st(x)), tinv, lakv)
    r_hat = each(lambda r_, m_, p_: r_ + _mm(m_, st(p_)), r_t, m_rb, p_mat)
    y_hat = each(lambda mb, q_, mk, vs: _mm(mb, st(q_)) + _mm(mk, vs), m_rb, q_mat, m_rk, vst)
    trans = each(lambda e_, bk, p_: eye128 * jnp.exp(e_)
                 + jnp.where(bd, _mm(bk, jnp.concatenate([p_, zeros64], axis=0)), 0.0), cl, bkt, p_mat)
    h_add = each(lambda bk, q_, v_: jnp.where(bd, _mm(bk, jnp.concatenate([q_, v_], axis=0)), 0.0), bkt, q_mat, v)
    hs = state[...]
    ys = []
    for c in range(nch):
        ys.append(_mm(r_hat[c], hs) + y_hat[c])
        hs = _mm(trans[c], hs, passes=3) + h_add[c]
    state[...] = hs

    y = jnp.concatenate(ys, axis=0)
    mean = seg_sum(y) * (1.0 / 64)
    dlt = y - mean
    var = seg_sum(dlt * dlt) * (1.0 / 64)
    yn = dlt * lax.rsqrt(var + RWKV_GN_EPS) * lnw_ref[...] + lnb_ref[...]
    yn = yn + seg_sum(r_all * k_all * rk_ref[...]) * v_all
    o_ref[0] = yn * g_ref[0]


def wkv7(r, k, v, lw, kk, a, g, lnx_w, lnx_b, r_k, tc=512, inv_passes=3):
    b, s, d = r.shape
    tile = pl.BlockSpec((1, tc, LANES), lambda bi, pi, ti: (bi, ti, pi))
    vec = pl.BlockSpec((1, LANES), lambda bi, pi, ti: (0, pi))
    return pl.pallas_call(
        functools.partial(_wkv_kernel, tc=tc, inv_passes=inv_passes),
        grid=(b, d // LANES, s // tc),
        in_specs=[tile] * 7 + [vec] * 3,
        out_specs=tile,
        out_shape=jax.ShapeDtypeStruct((b, s, d), F32),
        scratch_shapes=[pltpu.VMEM((LANES, LANES), F32)],
        compiler_params=_cparams(("parallel", "parallel", "arbitrary")), name="wkv7",
    )(r, k, v, lw, kk, a, g, lnx_w.reshape(1, d), lnx_b.reshape(1, d), r_k.reshape(1, d))


def _in_proj_weights(w_in):
    gw = GDN_HEADS * LANES
    c = 4 * gw
    w_qkv, w_z = w_in[:, 0:3 * gw], w_in[:, 3 * gw:c]
    d = w_in.shape[0]
    w_ba = jnp.zeros((d, LANES), w_in.dtype).at[:, :2 * GDN_HEADS].set(w_in[:, c:c + 2 * GDN_HEADS])
    c += 2 * GDN_HEADS
    nw = NSA_HEADS * NSA_DH
    w_q = w_in[:, c:c + nw].reshape(d, NSA_GROUPS, NSA_HPG, NSA_DH) * (NSA_DH ** -0.5)
    c += nw
    w_qpad = jnp.zeros((d, NSA_GROUPS, NSA_HPG, NSA_GROUPS, NSA_DH), w_in.dtype)
    for g in range(NSA_GROUPS):
        w_qpad = w_qpad.at[:, g, :, g, :].set(w_q[:, g])
    w_qpad = w_qpad.reshape(d, NSA_HEADS * LANES)
    kvw = NSA_GROUPS * NSA_DH
    w_cmp = w_in[:, c:c + 2 * kvw]
    w_kv4 = w_in[:, c + 2 * kvw:c + 6 * kvw]
    c += 6 * kvw
    w_gate = jnp.zeros((d, LANES), w_in.dtype).at[:, :3 * NSA_HEADS].set(w_in[:, c:c + 3 * NSA_HEADS])
    ws = [w_qkv, w_z, w_ba, w_qpad, w_cmp, w_kv4, w_gate]
    dts = [F32, F32, F32, BF16, F32, BF16, F32]
    return [w.astype(BF16) for w in ws], dts


def kernel(x, rel_bias, final_norm, e_attn_norm, e_w_in, e_conv_w, e_a_log, e_dt_bias, e_gdn_norm, e_cmpk_pe, e_cmpk_w1, e_cmpk_w2, e_cmpv_pe, e_cmpv_w1, e_cmpv_w2, e_w_out, e_ffn_norm, e_ffn_gate, e_ffn_up, e_ffn_down, o_attn_norm, o_mu, o_w_r, o_w_k, o_w_v, o_w_o, o_w0, o_w1, o_w2, o_a0, o_a1, o_a2, o_g1, o_g2, o_k_k, o_k_a, o_r_k, o_lnx_w, o_lnx_b, o_ffn_norm, o_ffn_gate, o_ffn_up, o_ffn_down):
    b, s, d = x.shape
    m = b * s
    xf = x.reshape(m, d)
    bf = lambda w: w.astype(BF16)

    ws, dts = _in_proj_weights(e_w_in[0])
    qkv, z, ba, qpad, kvc, kv4, gl = norm_linear(xf, e_attn_norm[0], ws, dts)
    sh = lambda t: t.reshape(b, s, t.shape[-1])
    o_gdn = gated_deltanet(sh(qkv), sh(z), sh(ba), e_conv_w[0], e_a_log[0], e_dt_bias[0], e_gdn_norm[0])
    kcvc = compress(sh(kvc), e_cmpk_pe[0], e_cmpk_w1[0], e_cmpk_w2[0], e_cmpv_pe[0], e_cmpv_w1[0], e_cmpv_w2[0])
    o_nsa = nsa_attention(sh(qpad), sh(gl), kcvc, sh(kv4), rel_bias)
    gw = GDN_HEADS * LANES
    xf = linear_res(xf, [o_gdn.reshape(m, -1), o_nsa.reshape(m, -1)], [bf(e_w_out[0][:gw]), bf(e_w_out[0][gw:])])
    xf = ffn(xf, e_ffn_norm[0], bf(e_ffn_gate[0]), bf(e_ffn_up[0]), bf(e_ffn_down[0]), final_norm, False)

    r, k, v, lw, kk, a, g = rwkv_proj(xf.reshape(b, s, d), o_attn_norm[0], o_mu[0], o_w_r[0], o_w_k[0], o_w_v[0],
                                      o_w0[0], o_w1[0], o_w2[0], o_a0[0], o_a1[0], o_a2[0], o_g1[0], o_g2[0],
                                      o_k_k[0], o_k_a[0])
    yg = wkv7(r, k, v, lw, kk, a, g, o_lnx_w[0], o_lnx_b[0], o_r_k[0].reshape(-1))
    xf = linear_res(xf, [yg.reshape(m, d)], [bf(o_w_o[0])])
    xf = ffn(xf, o_ffn_norm[0], bf(o_ffn_gate[0]), bf(o_ffn_up[0]), bf(o_ffn_down[0]), final_norm, True)
    return xf.reshape(b, s, d)
```

```python
import functools
import math

import jax
import jax.numpy as jnp
from jax import lax
from jax.experimental import pallas as pl
from jax.experimental.pallas import tpu as pltpu

F32 = jnp.float32
BF16 = jnp.bfloat16

VMEM_LIMIT_BYTES = 52 * 1024 * 1024
LANES = 128
CHUNK = 64

RMS_EPS = 1e-6
GDN_HEADS = 4
GDN_CONV = 4
NSA_HEADS = 8
NSA_GROUPS = 2
NSA_HPG = 4
NSA_DH = 64
CMP_LEN = 32
CMP_STRIDE = 16
SLC_LEN = 64
N_SELECT = 8
WINDOW = 512
NUM_BUCKETS = 32
REL_MAX_DIST = 128
NEG = -1e30
KV_PAD = 640
WIN_TILE = 640
RWKV_GN_EPS = 64e-5


def _cparams(sem):
    return pltpu.CompilerParams(dimension_semantics=sem, vmem_limit_bytes=VMEM_LIMIT_BYTES)


def _dot(a, b):
    return jnp.dot(a, b, preferred_element_type=F32)


def _dot_nt(a, b):
    return lax.dot_general(a, b, (((1,), (1,)), ((), ())), preferred_element_type=F32)


def _split(x, terms):
    out = []
    rem = x
    for _ in range(terms - 1):
        p = rem.astype(BF16)
        out.append(p)
        rem = rem - p.astype(F32)
    out.append(rem.astype(BF16))
    return out


def _mm(a, b, passes=1, nt=False):
    d = _dot_nt if nt else _dot
    if passes == 1:
        return d(a.astype(BF16), b.astype(BF16))
    ah, al = _split(a, 2)
    bh, bl = _split(b, 2)
    return d(ah, bh) + (d(ah, bl) + d(al, bh))


def _mm_exact_lhs(a01, b):
    a = a01.astype(BF16)
    b0, b1, b2 = _split(b, 3)
    return _dot(a, b0) + (_dot(a, b1) + _dot(a, b2))


def _mm_exact_rhs(a, b01):
    b = b01.astype(BF16)
    a0, a1, a2 = _split(a, 3)
    return _dot(a0, b) + (_dot(a1, b) + _dot(a2, b))


def _silu(x):
    return x * jax.nn.sigmoid(x)


def _softplus(x):
    return jnp.maximum(x, 0.0) + jnp.log1p(jnp.exp(-jnp.abs(x)))


def _lane_masks():
    lane = lax.broadcasted_iota(jnp.int32, (1, LANES), 1)
    m0 = (lane < 64).astype(F32)
    return m0, 1.0 - m0


def _stack2(x, m0, m1):
    return jnp.concatenate([x * m0, x * m1], axis=0)


def _tri_inv_cat(lcats, passes):
    m0, m1 = _lane_masks()
    i = lax.broadcasted_iota(jnp.int32, (CHUNK, LANES), 0)
    j = lax.broadcasted_iota(jnp.int32, (CHUNK, LANES), 1) & (CHUNK - 1)
    eye = (i == j).astype(F32)
    blk16 = (i >> 4) == (j >> 4)
    st = lambda x: _stack2(x, m0, m1)
    mm = lambda a, b: _mm(a, b, passes)
    each = lambda f, *ls: [f(*xs) for xs in zip(*ls)]
    ld = each(lambda l: jnp.where(blk16, l, 0.0), lcats)
    off = each(lambda l: jnp.where(blk16, 0.0, l), lcats)
    sld = each(st, ld)
    l2 = each(mm, ld, sld)
    sl2 = each(st, l2)
    n1 = each(lambda d, l, s: ((eye - d) + l) - mm(d, s), ld, l2, sl2)
    l4 = each(mm, l2, sl2)
    sl4 = each(st, l4)
    n2 = each(lambda a, s: a + mm(a, s), n1, sl4)
    l8 = each(mm, l4, sl4)
    x = each(lambda a, l: a + mm(a, st(l)), n2, l8)
    sx = each(st, x)
    mt = each(lambda a, o: mm(a, st(o)), x, off)
    m2 = each(lambda a: mm(a, st(a)), mt)
    mx = each(mm, mt, sx)
    m2x = each(mm, m2, sx)
    m3x = each(lambda a, b: mm(a, st(b)), m2, mx)
    return each(lambda a, b, c, d: (a - b) + (c - d), x, mx, m2x, m3x)


def _tri_ones(n, lower, block=None):
    i = lax.broadcasted_iota(jnp.int32, (n, n), 0)
    j = lax.broadcasted_iota(jnp.int32, (n, n), 1)
    m = (i >= j) if lower else (i <= j)
    if block is not None:
        sh = block.bit_length() - 1
        m = jnp.logical_and(m, (i >> sh) == (j >> sh))
    return m.astype(F32)


def _norm_linear_kernel(x_ref, nw_ref, *refs, n_out):
    w_refs, o_refs = refs[:n_out], refs[n_out:]
    x = x_ref[...]
    h = x * lax.rsqrt(jnp.mean(x * x, axis=-1, keepdims=True) + RMS_EPS) * nw_ref[...]
    hb = h.astype(BF16)
    for w_ref, o_ref in zip(w_refs, o_refs):
        o_ref[...] = _dot(hb, w_ref[...]).astype(o_ref.dtype)


def norm_linear(x, nw, weights, out_dtypes, tm=512):
    m, k = x.shape
    n_out = len(weights)
    in_specs = [pl.BlockSpec((tm, k), lambda i: (i, 0)), pl.BlockSpec((1, k), lambda i: (0, 0))]
    in_specs += [pl.BlockSpec(w.shape, lambda i: (0, 0)) for w in weights]
    out_specs = [pl.BlockSpec((tm, w.shape[1]), lambda i: (i, 0)) for w in weights]
    out_shape = [jax.ShapeDtypeStruct((m, w.shape[1]), dt) for w, dt in zip(weights, out_dtypes)]
    return pl.pallas_call(
        functools.partial(_norm_linear_kernel, n_out=n_out),
        grid=(m // tm,), in_specs=in_specs, out_specs=out_specs, out_shape=out_shape,
        compiler_params=_cparams(("parallel",)), name="norm_linear",
    )(x, nw.reshape(1, k), *weights)


def _linear_res_kernel(res_ref, *refs, n_in):
    a_refs, w_refs, o_ref = refs[:n_in], refs[n_in:2 * n_in], refs[2 * n_in]
    acc = res_ref[...]
    for a_ref, w_ref in zip(a_refs, w_refs):
        acc = acc + _dot(a_ref[...].astype(BF16), w_ref[...])
    o_ref[...] = acc


def linear_res(res, acts, weights, tm=512):
    m, n = res.shape
    n_in = len(acts)
    in_specs = [pl.BlockSpec((tm, n), lambda i: (i, 0))]
    in_specs += [pl.BlockSpec((tm, a.shape[1]), lambda i: (i, 0)) for a in acts]
    in_specs += [pl.BlockSpec(w.shape, lambda i: (0, 0)) for w in weights]
    return pl.pallas_call(
        functools.partial(_linear_res_kernel, n_in=n_in),
        grid=(m // tm,), in_specs=in_specs, out_specs=pl.BlockSpec((tm, n), lambda i: (i, 0)),
        out_shape=jax.ShapeDtypeStruct((m, n), F32),
        compiler_params=_cparams(("parallel",)), name="linear_res",
    )(res, *acts, *weights)


def _ffn_kernel(x_ref, nw_ref, wg_ref, wu_ref, wd_ref, fw_ref, o_ref, xn_scr, acc_scr, *, final_norm):
    f = pl.program_id(1)

    @pl.when(f == 0)
    def _():
        x = x_ref[...]
        h = x * lax.rsqrt(jnp.mean(x * x, axis=-1, keepdims=True) + RMS_EPS) * nw_ref[...]
        xn_scr[...] = h.astype(BF16)
        acc_scr[...] = jnp.zeros_like(acc_scr)

    xn = xn_scr[...]
    g = _dot(xn, wg_ref[...])
    u = _dot(xn, wu_ref[...])
    hid = (_silu(g) * u).astype(BF16)
    acc_scr[...] += _dot(hid, wd_ref[...])

    @pl.when(f == pl.num_programs(1) - 1)
    def _():
        y = x_ref[...] + acc_scr[...]
        if final_norm:
            y = y * lax.rsqrt(jnp.mean(y * y, axis=-1, keepdims=True) + RMS_EPS) * fw_ref[...]
        o_ref[...] = y


def ffn(x, nw, wg, wu, wd, fw, final_norm, tm=512, tf=1408):
    m, d = x.shape
    dff = wg.shape[1]
    return pl.pallas_call(
        functools.partial(_ffn_kernel, final_norm=final_norm),
        grid=(m // tm, dff // tf),
        in_specs=[pl.BlockSpec((tm, d), lambda i, f: (i, 0)),
                  pl.BlockSpec((1, d), lambda i, f: (0, 0)),
                  pl.BlockSpec((d, tf), lambda i, f: (0, f)),
                  pl.BlockSpec((d, tf), lambda i, f: (0, f)),
                  pl.BlockSpec((tf, d), lambda i, f: (f, 0)),
                  pl.BlockSpec((1, d), lambda i, f: (0, 0))],
        out_specs=pl.BlockSpec((tm, d), lambda i, f: (i, 0)),
        out_shape=jax.ShapeDtypeStruct((m, d), F32),
        scratch_shapes=[pltpu.VMEM((tm, d), BF16), pltpu.VMEM((tm, d), F32)],
        compiler_params=_cparams(("parallel", "arbitrary")), name="ffn",
    )(x, nw.reshape(1, d), wg, wu, wd, fw.reshape(1, d))


def _gdn_kernel(q_ref, k_ref, v_ref, qh_ref, kh_ref, vh_ref, cwq_ref, cwk_ref, cwv_ref,
                z_ref, ba_ref, bat_ref, hp_ref, nw_ref, o_ref, scr, state, *, tc, inv_passes):
    h = pl.program_id(1)
    t = pl.program_id(2)

    @pl.when(t == 0)
    def _():
        state[...] = jnp.zeros_like(state)

    not_first = (t > 0).astype(F32)

    def conv(x_ref, halo_ref, w_ref, slot):
        scr[slot, 0:8, :] = halo_ref[0] * not_first
        scr[slot, 8:8 + tc, :] = x_ref[0]
        w = w_ref[...]
        y = (w[3:4] * scr[slot, 8:8 + tc, :] + w[2:3] * scr[slot, 7:7 + tc, :]
             + w[1:2] * scr[slot, 6:6 + tc, :] + w[0:1] * scr[slot, 5:5 + tc, :])
        return _silu(y)

    q = conv(q_ref, qh_ref, cwq_ref, 0)
    k = conv(k_ref, kh_ref, cwk_ref, 1)
    v = conv(v_ref, vh_ref, cwv_ref, 2)
    qn = q * lax.rsqrt(jnp.sum(q * q, axis=-1, keepdims=True) + 1e-6)
    kn = k * lax.rsqrt(jnp.sum(k * k, axis=-1, keepdims=True) + 1e-6)

    lane = lax.broadcasted_iota(jnp.int32, (1, LANES), 1)
    ba = ba_ref[0]
    b_col = jnp.sum(jnp.where(lane == h, ba, 0.0), axis=-1, keepdims=True)
    a_col = jnp.sum(jnp.where(lane == h + GDN_HEADS, ba, 0.0), axis=-1, keepdims=True)
    beta = jax.nn.sigmoid(b_col)
    a_log = hp_ref[0, 0:1, :]
    dt_b = hp_ref[0, 1:2, :]
    g_rep = -jnp.exp(a_log) * _softplus(a_col + dt_b)

    m0, m1 = _lane_masks()
    i64 = lax.broadcasted_iota(jnp.int32, (CHUNK, LANES), 0)
    j64 = lax.broadcasted_iota(jnp.int32, (CHUNK, LANES), 1) & (CHUNK - 1)
    low_half = lax.broadcasted_iota(jnp.int32, (CHUNK, LANES), 1) < CHUNK
    ltri_bd = _tri_ones(2 * CHUNK, True, CHUNK)
    utri_bd = _tri_ones(2 * CHUNK, False, CHUNK)
    eye128 = (lax.broadcasted_iota(jnp.int32, (LANES, LANES), 0)
              == lax.broadcasted_iota(jnp.int32, (LANES, LANES), 1)).astype(F32)
    zeros64 = jnp.zeros((CHUNK, LANES), F32)
    scale = 1.0 / math.sqrt(LANES)
    nw = nw_ref[...]

    npair = tc // (2 * CHUNK)
    each = lambda f, *ls: [f(*xs) for xs in zip(*ls)]
    halves = lambda x: jnp.where(low_half, x[0:CHUNK], x[CHUNK:2 * CHUNK])
    prow = [slice(2 * CHUNK * p, 2 * CHUNK * (p + 1)) for p in range(npair)]
    kp = [kn[rs] for rs in prow]
    qp = [qn[rs] * scale for rs in prow]
    vp = [v[rs] for rs in prow]
    beta_p = [beta[rs] for rs in prow]
    gc_col = [_mm_exact_lhs(ltri_bd, g_rep[rs]) for rs in prow]
    g_row = [-jnp.exp(a_log[:, 0:1]) * _softplus(bat_ref[0, 0, p, 1:2, :] + dt_b[:, 0:1]) for p in range(npair)]
    gc_row = each(lambda x: _mm_exact_rhs(jnp.broadcast_to(x, (CHUNK, LANES)), utri_bd), g_row)
    dcat = each(lambda c_, r_: jnp.exp(jnp.where(i64 >= j64, halves(c_) - r_, -jnp.inf)), gc_col, gc_row)
    kb = each(lambda k_, b_: k_ * b_, kp, beta_p)
    gram = each(lambda x, y_: _mm(x, y_, nt=True), kb, kp)
    gq = each(lambda x, y_: _mm(x, y_, nt=True), qp, kp)
    strict = each(lambda x, d_: jnp.where(i64 > j64, halves(x) * d_, 0.0), gram, dcat)
    qk_bd = each(lambda x, d_: _stack2(jnp.where(i64 >= j64, halves(x) * d_, 0.0), m0, m1), gq, dcat)
    eg = each(jnp.exp, gc_col)
    g_last = each(lambda x: [x[CHUNK * (c + 1) - 1:CHUNK * (c + 1), :] for c in range(2)], gc_col)
    kdt = each(lambda k_, gl_, gc_: jnp.concatenate(
        [k_[CHUNK * c:CHUNK * (c + 1)] * jnp.exp(gl_[c] - gc_[CHUNK * c:CHUNK * (c + 1)]) for c in range(2)],
        axis=0).T, kp, g_last, gc_col)
    tinv = _tri_inv_cat(strict, inv_passes)
    uw = each(lambda t_, v_, b_, kb_, e_: _mm(_stack2(t_, m0, m1), jnp.concatenate([v_ * b_, kb_ * e_], axis=1)),
              tinv, vp, beta_p, kb, eg)
    u = each(lambda x: x[:, 0:LANES], uw)
    w = each(lambda x: x[:, LANES:2 * LANES], uw)
    q_hat = each(lambda q_, e_, m_, w_: q_ * e_ - _mm(m_, w_), qp, eg, qk_bd, w)
    o_hat = each(_mm, qk_bd, u)
    pad = lambda x, c: jnp.concatenate([x[0:CHUNK], zeros64] if c == 0 else [zeros64, x[CHUNK:2 * CHUNK]], axis=0)
    trans = [eye128 * jnp.exp(g_last[p][c]) - _mm(kdt[p], pad(w[p], c)) for p in range(npair) for c in range(2)]
    h_add = [_mm(kdt[p], pad(u[p], c)) for p in range(npair) for c in range(2)]
    s_state = state[...]
    for p in range(npair):
        for c in range(2):
            cr = slice(CHUNK * c, CHUNK * (c + 1))
            o = _mm(q_hat[p][cr], s_state) + o_hat[p][cr]
            s_state = _mm(trans[2 * p + c], s_state, passes=3) + h_add[2 * p + c]
            on = o * lax.rsqrt(jnp.mean(o * o, axis=-1, keepdims=True) + RMS_EPS) * nw
            r0 = 2 * CHUNK * p + CHUNK * c
            o_ref[0, r0:r0 + CHUNK, :] = (on * _silu(z_ref[0, r0:r0 + CHUNK, :])).astype(o_ref.dtype)
    state[...] = s_state


def gated_deltanet(qkv, z, ba, conv_w, a_log, dt_bias, norm_w, tc=512, inv_passes=1):
    b, s, _ = qkv.shape
    hh = GDN_HEADS
    bat = ba[..., :2 * hh].reshape(b, s // 128, 128, 2, hh).transpose(0, 4, 1, 3, 2)
    hp = jnp.zeros((hh, 8, LANES), F32)
    hp = hp.at[:, 0, :].set(a_log.astype(F32)[:, None]).at[:, 1, :].set(dt_bias.astype(F32)[:, None])
    cw = jnp.zeros((8, 3 * hh * LANES), F32).at[:GDN_CONV].set(conv_w.astype(F32))
    hb = tc // 8
    blk = lambda off: pl.BlockSpec((1, tc, LANES), lambda bi, hi, ti: (bi, ti, hi + off))
    halo = lambda off: pl.BlockSpec((1, 8, LANES), lambda bi, hi, ti: (bi, jnp.maximum(ti * hb - 1, 0), hi + off))
    cws = lambda off: pl.BlockSpec((8, LANES), lambda bi, hi, ti: (0, hi + off))
    return pl.pallas_call(
        functools.partial(_gdn_kernel, tc=tc, inv_passes=inv_passes),
        grid=(b, hh, s // tc),
        in_specs=[blk(0), blk(hh), blk(2 * hh), halo(0), halo(hh), halo(2 * hh), cws(0), cws(hh), cws(2 * hh),
                  blk(0),
                  pl.BlockSpec((1, tc, LANES), lambda bi, hi, ti: (bi, ti, 0)),
                  pl.BlockSpec((1, 1, tc // 128, 2, LANES), lambda bi, hi, ti: (bi, hi, ti, 0, 0)),
                  pl.BlockSpec((1, 8, LANES), lambda bi, hi, ti: (hi, 0, 0)),
                  pl.BlockSpec((1, LANES), lambda bi, hi, ti: (0, 0))],
        out_specs=blk(0),
        out_shape=jax.ShapeDtypeStruct((b, s, hh * LANES), BF16),
        scratch_shapes=[pltpu.VMEM((3, tc + 8, LANES), F32), pltpu.VMEM((LANES, LANES), F32)],
        compiler_params=_cparams(("parallel", "parallel", "arbitrary")), name="gated_deltanet",
    )(qkv, qkv, qkv, qkv, qkv, qkv, cw, cw, cw, z, ba, bat, hp, norm_w.astype(F32).reshape(1, LANES))


def _compress_kernel(x_ref, pe_ref, w1_ref, w2_ref, o_ref):
    nb = x_ref.shape[3]
    acc = jnp.zeros((nb, LANES), F32)
    for g in range(NSA_GROUPS):
        x = x_ref[0, 0, g]
        first = _dot((x + pe_ref[0, 0]).astype(BF16), w1_ref[0, 0])
        second = _dot((x + pe_ref[0, 1]).astype(BF16), w1_ref[0, 1])
        hid = _silu(first + pltpu.roll(second, nb - 1, 0))
        acc = acc + _dot(hid.astype(BF16), w2_ref[0, g])
    o_ref[0, 0] = acc.astype(o_ref.dtype)


def compress(kvc, pe_k, w1_k, w2_k, pe_v, w1_v, w2_v):
    b, s, _ = kvc.shape
    nb = s // CMP_STRIDE
    width = CMP_STRIDE * NSA_DH
    x = kvc.reshape(b, nb, CMP_STRIDE, 2, NSA_GROUPS, NSA_DH).transpose(0, 3, 4, 1, 2, 5).reshape(b, 2, NSA_GROUPS, nb, width)
    pe = jnp.stack([pe_k, pe_v]).astype(F32).reshape(2, 2, 1, width)
    w1 = jnp.stack([w1_k, w1_v]).astype(BF16).reshape(2, 2, width, -1)
    hid = w1.shape[-1]
    w2 = jnp.stack([w2_k, w2_v]).astype(BF16)
    w2p = jnp.zeros((2, NSA_GROUPS, hid, LANES), BF16)
    for g in range(NSA_GROUPS):
        w2p = w2p.at[:, g, :, NSA_DH * g:NSA_DH * (g + 1)].set(w2)
    return pl.pallas_call(
        _compress_kernel,
        grid=(b, 2),
        in_specs=[pl.BlockSpec((1, 1, NSA_GROUPS, nb, width), lambda bi, j: (bi, j, 0, 0, 0)),
                  pl.BlockSpec((1, 2, 1, width), lambda bi, j: (j, 0, 0, 0)),
                  pl.BlockSpec((1, 2, width, hid), lambda bi, j: (j, 0, 0, 0)),
                  pl.BlockSpec((1, NSA_GROUPS, hid, LANES), lambda bi, j: (j, 0, 0, 0))],
        out_specs=pl.BlockSpec((1, 1, nb, LANES), lambda bi, j: (bi, j, 0, 0)),
        out_shape=jax.ShapeDtypeStruct((b, 2, nb, LANES), BF16),
        compiler_params=_cparams(("parallel", "parallel")), name="nsa_compress",
    )(x, pe, w1, w2p)


def _softmax_pv(s, v):
    m = jnp.max(s, axis=-1, keepdims=True)
    p = jnp.exp(s - m)
    l = jnp.sum(p, axis=-1, keepdims=True)
    return _dot(p.astype(BF16), v) / l


def _nsa_kernel(q_ref, gl_ref, kcvc_ref, kv_ref, bw_ref, bfar_ref, bc_ref, ovl_ref, o_ref):
    n = pl.program_id(1)
    qb_rows = q_ref.shape[1]
    ns = (kv_ref.shape[1] - KV_PAD) // SLC_LEN
    far_w = 8 * SLC_LEN
    lane = lax.broadcasted_iota(jnp.int32, (qb_rows, LANES), 1)
    lane_s = lane & (ns - 1)
    gates = jax.nn.sigmoid(gl_ref[0])
    kc = kcvc_ref[0, 0]
    vc = kcvc_ref[0, 1]
    ovl = ovl_ref[...]
    row_blk = lax.broadcasted_iota(jnp.int32, (LANES, far_w), 0)
    col_blk = lax.broadcasted_iota(jnp.int32, (LANES, far_w), 1) >> 6
    wcol = lax.broadcasted_iota(jnp.int32, (NSA_HPG * qb_rows, WIN_TILE), 1)
    groups = range(NSA_GROUPS)
    tile4 = lambda x: jnp.concatenate([x] * NSA_HPG, axis=0)

    qs = [jnp.concatenate([q_ref[0, :, LANES * (NSA_HPG * g + hh):LANES * (NSA_HPG * g + hh + 1)]
                           for hh in range(NSA_HPG)], axis=0) for g in groups]
    bc = [bc_ref[NSA_HPG * g:NSA_HPG * (g + 1)].reshape(NSA_HPG * qb_rows, LANES) for g in groups]
    start = pl.multiple_of(n * SLC_LEN + (KV_PAD - 3 * SLC_LEN), SLC_LEN)
    k_near = kv_ref[0, pl.ds(start, 4 * SLC_LEN), 0:LANES]
    v_near = kv_ref[0, pl.ds(start, 4 * SLC_LEN), LANES:2 * LANES]
    wst = pl.multiple_of(n * SLC_LEN + (KV_PAD + SLC_LEN - WIN_TILE), SLC_LEN)
    k_w = kv_ref[0, pl.ds(wst, WIN_TILE), 2 * LANES:3 * LANES]
    v_w = kv_ref[0, pl.ds(wst, WIN_TILE), 3 * LANES:4 * LANES]
    s_cmp = [_dot_nt(qs[g], kc) + bc[g] for g in groups]
    s_near = [_dot_nt(qs[g], k_near) + bw_ref[g, :, WIN_TILE - 4 * SLC_LEN:WIN_TILE] for g in groups]
    s_win = [jnp.where(wcol >= (WIN_TILE - SLC_LEN) - SLC_LEN * n, _dot_nt(qs[g], k_w) + bw_ref[g], NEG)
             for g in groups]

    p_cmp = []
    for g in groups:
        m = jnp.max(s_cmp[g], axis=-1, keepdims=True)
        e = jnp.where(bc[g] > 0.5 * NEG, jnp.exp(s_cmp[g] - m), 0.0)
        p_cmp.append(e / jnp.maximum(jnp.sum(e, axis=-1, keepdims=True), 1e-30))
    o_cmp = [_dot(p.astype(BF16), vc) for p in p_cmp]
    imp = [_mm_exact_rhs(p[0:qb_rows] + p[qb_rows:2 * qb_rows] + p[2 * qb_rows:3 * qb_rows] + p[3 * qb_rows:4 * qb_rows],
                         ovl) for p in p_cmp]

    forced = jnp.logical_or(jnp.logical_or(lane_s == 0, lane_s == n), lane_s == n - 1)
    val = [jnp.where(lane_s > n, -1.0, jnp.where(forced, 1e6, x)) for x in imp]
    rank = [jnp.zeros((qb_rows, LANES), F32) for _ in groups]
    for r in range(1, ns):
        tie = (lane_s >= r).astype(F32)
        for g in groups:
            other = pltpu.roll(val[g], r, 1)
            rank[g] = rank[g] + jnp.where(other > val[g], 1.0, jnp.where(other == val[g], tie, 0.0))
    sel = [jnp.where(jnp.logical_and(x < N_SELECT, lane < ns), 1.0, 0.0) for x in rank]

    o_win = [_softmax_pv(s_win[g], v_w) for g in groups]

    e_near = (lax.broadcasted_iota(jnp.int32, (LANES, 4 * SLC_LEN), 0)
              == (lax.broadcasted_iota(jnp.int32, (LANES, 4 * SLC_LEN), 1) >> 6) + (n - 3)).astype(BF16)
    carry = []
    for g in groups:
        sm = _dot(sel[g].astype(BF16), e_near)
        s = jnp.where(tile4(sm) > 0.5, s_near[g], NEG)
        m = jnp.max(s, axis=-1, keepdims=True)
        p = jnp.exp(s - m)
        carry += [m, jnp.sum(p, axis=-1, keepdims=True), _dot(p.astype(BF16), v_near)]
    sel_far = [jnp.where(lane <= n - 4, x, 0.0).astype(BF16) for x in sel]
    b_far = [jnp.concatenate([bfar_ref[g]] * (far_w // LANES), axis=1) for g in groups]

    def far_body(it, carry):
        st = pl.multiple_of(KV_PAD + it * far_w, math.gcd(KV_PAD, far_w))
        k_c = kv_ref[0, pl.ds(st, far_w), 0:LANES]
        v_c = kv_ref[0, pl.ds(st, far_w), LANES:2 * LANES]
        e_c = (row_blk == col_blk + 8 * it).astype(BF16)
        sc = [_dot_nt(qs[g], k_c) + b_far[g] for g in groups]
        smc = [_dot(sel_far[g], e_c) for g in groups]
        out = []
        for g in groups:
            m_i, l_i, acc_i = carry[3 * g:3 * g + 3]
            s_g = jnp.where(tile4(smc[g]) > 0.5, sc[g], NEG)
            m_n = jnp.maximum(m_i, jnp.max(s_g, axis=-1, keepdims=True))
            alpha = jnp.exp(m_i - m_n)
            pc = jnp.exp(s_g - m_n)
            out += [m_n, alpha * l_i + jnp.sum(pc, axis=-1, keepdims=True),
                    alpha * acc_i + _dot(pc.astype(BF16), v_c)]
        return tuple(out)

    carry = lax.fori_loop(0, (n // 4 + 1) // 2, far_body, tuple(carry))
    o_slc = [carry[3 * g + 2] / carry[3 * g + 1] for g in groups]

    for g in groups:
        pieces = []
        for hh in range(NSA_HPG):
            hd = NSA_HPG * g + hh
            rs = slice(qb_rows * hh, qb_rows * (hh + 1))
            pieces.append(gates[:, 3 * hd:3 * hd + 1] * o_cmp[g][rs] + gates[:, 3 * hd + 1:3 * hd + 2] * o_slc[g][rs]
                          + gates[:, 3 * hd + 2:3 * hd + 3] * o_win[g][rs])
        for hp in range(2):
            lo, hi = pieces[2 * hp], pieces[2 * hp + 1]
            if g == 0:
                hi = pltpu.roll(hi, NSA_DH, 1)
            else:
                lo = pltpu.roll(lo, NSA_DH, 1)
            o_ref[0, :, LANES * (2 * g + hp):LANES * (2 * g + hp + 1)] = jnp.where(lane < NSA_DH, lo, hi).astype(o_ref.dtype)


def _rel_bucket(dist):
    n = jnp.maximum(dist, 0)
    exact = NUM_BUCKETS // 2
    nf = jnp.maximum(n, 1).astype(F32)
    large = exact + (jnp.log(nf / exact) / math.log(REL_MAX_DIST / exact) * (NUM_BUCKETS - exact)).astype(jnp.int32)
    return jnp.where(n < exact, n, jnp.minimum(large, NUM_BUCKETS - 1))


def nsa_attention(qpad, gl, kcvc, kv4, rel_bias):
    b, s, _ = qpad.shape
    qb = SLC_LEN
    nq = s // qb
    nc = s // CMP_STRIDE - CMP_LEN // CMP_STRIDE + 1
    ns = s // SLC_LEN
    table = rel_bias.astype(F32)
    d_rev = (WIN_TILE - 1) - jnp.arange(WIN_TILE + qb - 1)
    rev = jnp.where(jnp.logical_and(d_rev >= 0, d_rev < WINDOW)[None], table[_rel_bucket(d_rev)].T, NEG)
    bw = jnp.stack([rev[:, qb - 1 - i:qb - 1 - i + WIN_TILE] for i in range(qb)], axis=1)
    bw = bw.reshape(NSA_GROUPS, NSA_HPG * qb, WIN_TILE)
    far_val = table[NUM_BUCKETS - 1]
    bfar = jnp.broadcast_to(far_val[:, None, None], (NSA_HEADS, qb, LANES)).reshape(NSA_GROUPS, NSA_HPG * qb, LANES)
    cmp_end = jnp.arange(LANES) * CMP_STRIDE + CMP_LEN - 1
    by_dist = jnp.concatenate([jnp.full((NSA_HEADS, s), NEG, F32), table[_rel_bucket(jnp.arange(s))].T], axis=1)
    neg_col = jnp.full((NSA_HEADS, s), NEG, F32)
    bc = jnp.stack([by_dist[:, s - (CMP_STRIDE * c + CMP_LEN - 1):2 * s - (CMP_STRIDE * c + CMP_LEN - 1)]
                    if c < nc else neg_col for c in range(LANES)], axis=-1)
    cmp_start = jnp.arange(LANES) * CMP_STRIDE
    slc_j = jnp.arange(LANES) % ns
    ovl = jnp.logical_and(jnp.logical_and(cmp_start[:, None] < (slc_j[None, :] + 1) * SLC_LEN,
                                          cmp_end[:, None] >= slc_j[None, :] * SLC_LEN),
                          jnp.arange(LANES)[:, None] < nc).astype(F32)
    kvp = jnp.pad(kv4, ((0, 0), (KV_PAD, 0), (0, 0)))
    sp = s + KV_PAD
    return pl.pallas_call(
        _nsa_kernel,
        grid=(b, nq),
        in_specs=[pl.BlockSpec((1, qb, NSA_HEADS * LANES), lambda bi, n: (bi, n, 0)),
                  pl.BlockSpec((1, qb, LANES), lambda bi, n: (bi, n, 0)),
                  pl.BlockSpec((1, 2, s // CMP_STRIDE, LANES), lambda bi, n: (bi, 0, 0, 0)),
                  pl.BlockSpec((1, sp, 4 * LANES), lambda bi, n: (bi, 0, 0)),
                  pl.BlockSpec((NSA_GROUPS, NSA_HPG * qb, WIN_TILE), lambda bi, n: (0, 0, 0)),
                  pl.BlockSpec((NSA_GROUPS, NSA_HPG * qb, LANES), lambda bi, n: (0, 0, 0)),
                  pl.BlockSpec((NSA_HEADS, qb, LANES), lambda bi, n: (0, n, 0)),
                  pl.BlockSpec((LANES, LANES), lambda bi, n: (0, 0))],
        out_specs=pl.BlockSpec((1, qb, NSA_HEADS * NSA_DH), lambda bi, n: (bi, n, 0)),
        out_shape=jax.ShapeDtypeStruct((b, s, NSA_HEADS * NSA_DH), BF16),
        compiler_params=_cparams(("parallel", "arbitrary")), name="nsa_attention",
    )(qpad, gl, kcvc, kvp, bw, bfar, bc, ovl)


def _rwkv_proj_kernel(x_ref, halo_ref, nw_ref, mu_ref, vec_ref, wr_ref, wk_ref, wv_ref,
                      w1_ref, w2_ref, a1_ref, a2_ref, g1_ref, g2_ref,
                      r_ref, k_ref, v_ref, lw_ref, kk_ref, a_ref, g_ref, scr, *, tm):
    t = pl.program_id(1)
    nw = nw_ref[...]

    def norm(x):
        return x * lax.rsqrt(jnp.mean(x * x, axis=-1, keepdims=True) + RMS_EPS) * nw

    h = norm(x_ref[0])
    scr[0:8, :] = norm(halo_ref[0]) * (t > 0).astype(F32)
    scr[8:8 + tm, :] = h
    xx = scr[7:7 + tm, :] - h
    mix = lambda j: (h + xx * mu_ref[j:j + 1, :]).astype(BF16)
    w0, a0, k_k, k_a = vec_ref[0:1, :], vec_ref[1:2, :], vec_ref[2:3, :], vec_ref[3:4, :]
    r = _dot(mix(0), wr_ref[...])
    wl = w0 + _dot(jnp.tanh(_dot(mix(1), w1_ref[...])).astype(BF16), w2_ref[...])
    k = _dot(mix(2), wk_ref[...])
    v = _dot(mix(3), wv_ref[...])
    a = jax.nn.sigmoid(a0 + _dot(_dot(mix(4), a1_ref[...]).astype(BF16), a2_ref[...]))
    g = _dot(jax.nn.sigmoid(_dot(mix(5), g1_ref[...])).astype(BF16), g2_ref[...])
    w_raw = -_softplus(-wl) - 0.5
    r_ref[0] = r.astype(r_ref.dtype)
    k_ref[0] = (k * (1.0 + (a - 1.0) * k_a)).astype(k_ref.dtype)
    v_ref[0] = v.astype(v_ref.dtype)
    lw_ref[0] = -jnp.exp(w_raw)
    kk_ref[0] = (k * k_k).astype(kk_ref.dtype)
    a_ref[0] = a.astype(a_ref.dtype)
    g_ref[0] = g.astype(g_ref.dtype)


def rwkv_proj(x, nw, mu, w_r, w_k, w_v, w0, w1, w2, a0, a1, a2, g1, g2, k_k, k_a, tm=256):
    b, s, d = x.shape
    pad_to = lambda w, rows, cols: jnp.zeros((rows, cols), BF16).at[:w.shape[0], :w.shape[1]].set(w.astype(BF16))
    lora = lambda n: -(-n // LANES) * LANES
    mu8 = jnp.zeros((8, d), F32).at[:6].set(mu.astype(F32))
    vec = jnp.zeros((8, d), F32).at[0].set(w0).at[1].set(a0).at[2].set(k_k).at[3].set(k_a)
    big = [w.astype(BF16) for w in (w_r, w_k, w_v)]
    small = [pad_to(w1, d, lora(w1.shape[1])), pad_to(w2, lora(w2.shape[0]), d),
             pad_to(a1, d, lora(a1.shape[1])), pad_to(a2, lora(a2.shape[0]), d),
             pad_to(g1, d, lora(g1.shape[1])), pad_to(g2, lora(g2.shape[0]), d)]
    whole = lambda w: pl.BlockSpec(w.shape, lambda bi, ti: (0, 0))
    hb = tm // 8
    tile = pl.BlockSpec((1, tm, d), lambda bi, ti: (bi, ti, 0))
    return pl.pallas_call(
        functools.partial(_rwkv_proj_kernel, tm=tm),
        grid=(b, s // tm),
        in_specs=[tile, pl.BlockSpec((1, 8, d), lambda bi, ti: (bi, jnp.maximum(ti * hb - 1, 0), 0)),
                  pl.BlockSpec((1, d), lambda bi, ti: (0, 0)), whole(mu8), whole(vec)]
                 + [whole(w) for w in big] + [whole(w) for w in small],
        out_specs=[tile] * 7,
        out_shape=[jax.ShapeDtypeStruct((b, s, d), dt) for dt in (BF16, BF16, BF16, F32, BF16, BF16, BF16)],
        scratch_shapes=[pltpu.VMEM((tm + 8, d), F32)],
        compiler_params=_cparams(("parallel", "arbitrary")), name="rwkv_proj",
    )(x, x, nw.reshape(1, d), mu8, vec, *big, *small)


def _wkv_kernel(r_ref, k_ref, v_ref, lw_ref, kk_ref, a_ref, g_ref, lnw_ref, lnb_ref, rk_ref,
                o_ref, state, *, tc, inv_passes):
    t = pl.program_id(2)

    @pl.when(t == 0)
    def _():
        state[...] = jnp.zeros_like(state)

    m0, m1 = _lane_masks()
    lane = lax.broadcasted_iota(jnp.int32, (1, LANES), 1)
    low = lane < 64

    def seg_sum(x):
        return jnp.where(low, jnp.sum(x * m0, axis=-1, keepdims=True), jnp.sum(x * m1, axis=-1, keepdims=True))

    r_all, k_all, v_all = (x[0].astype(F32) for x in (r_ref, k_ref, v_ref))
    kk = kk_ref[0].astype(F32)
    kk = kk * lax.rsqrt(seg_sum(kk * kk) + 1e-12)
    a_all = -kk
    b_all = kk * a_ref[0].astype(F32)

    i64 = lax.broadcasted_iota(jnp.int32, (CHUNK, LANES), 0)
    j64 = lax.broadcasted_iota(jnp.int32, (CHUNK, LANES), 1) & (CHUNK - 1)
    strict_m = i64 > j64
    incl_m = i64 >= j64
    ltri = _tri_ones(CHUNK, True)
    eye128 = (lax.broadcasted_iota(jnp.int32, (LANES, LANES), 0)
              == lax.broadcasted_iota(jnp.int32, (LANES, LANES), 1)).astype(F32)
    bd = ((lax.broadcasted_iota(jnp.int32, (LANES, LANES), 0) >> 6)
          == (lax.broadcasted_iota(jnp.int32, (LANES, LANES), 1) >> 6))
    zeros64 = jnp.zeros((CHUNK, LANES), F32)
    st = lambda x: _stack2(x, m0, m1)

    nch = tc // CHUNK
    each = lambda f, *ls: [f(*xs) for xs in zip(*ls)]
    rows = [slice(CHUNK * c, CHUNK * (c + 1)) for c in range(nch)]
    lw = [lw_ref[0, rs, :] for rs in rows]
    r, k, v = ([x[rs] for rs in rows] for x in (r_all, k_all, v_all))
    av, bv = ([x[rs] for rs in rows] for x in (a_all, b_all))
    cs = each(lambda x: _mm_exact_lhs(ltri, x), lw)
    cl = each(lambda x: x[CHUNK - 1:CHUNK, :], cs)
    a_t = each(lambda a_, c_, l_: a_ * jnp.exp(c_ - l_), av, cs, lw)
    r_t = each(lambda r_, c_: r_ * jnp.exp(c_), r, cs)
    inv_w = each(lambda c_: jnp.exp(-c_), cs)
    b_t = each(lambda x, w_: x * w_, bv, inv_w)
    k_t = each(lambda x, w_: x * w_, k, inv_w)
    to_end = each(lambda e_, c_: jnp.exp(e_ - c_), cl, cs)
    bkt = each(lambda b_, k_, e_: jnp.concatenate([b_ * e_, k_ * e_], axis=0).T, bv, k, to_end)
    ar = each(lambda a_, r_: jnp.concatenate([a_, r_], axis=0), a_t, r_t)
    gb = each(lambda x, y_: _mm(x, st(y_), nt=True), ar, b_t)
    gk = each(lambda x, y_: _mm(x, st(y_), nt=True), ar, k_t)
    l_ab = each(lambda x: jnp.where(strict_m, -x[0:CHUNK], 0.0), gb)
    l_ak = each(lambda x: jnp.where(strict_m, x[0:CHUNK], 0.0), gk)
    m_rb = each(lambda x: jnp.where(incl_m, x[CHUNK:2 * CHUNK], 0.0), gb)
    m_rk = each(lambda x: jnp.where(incl_m, x[CHUNK:2 * CHUNK], 0.0), gk)
    vst = each(st, v)
    lakv = each(_mm, l_ak, vst)
    tinv = _tri_inv_cat(l_ab, inv_passes)
    p_mat = each(lambda t_, a_: _mm(t_, st(a_)), tinv, a_t)
    q_mat = each(lambda t_, x: _mm(t_, st(x)), tinv, lakv)
    r_hat = each(lambda r_, m_, p_: r_ + _mm(m_, st(p_)), r_t, m_rb, p_mat)
    y_hat = each(lambda mb, q_, mk, vs: _mm(mb, st(q_)) + _mm(mk, vs), m_rb, q_mat, m_rk, vst)
    trans = each(lambda e_, bk, p_: eye128 * jnp.exp(e_)
                 + jnp.where(bd, _mm(bk, jnp.concatenate([p_, zeros64], axis=0)), 0.0), cl, bkt, p_mat)
    h_add = each(lambda bk, q_, v_: jnp.where(bd, _mm(bk, jnp.concatenate([q_, v_], axis=0)), 0.0), bkt, q_mat, v)
    hs = state[...]
    ys = []
    for c in range(nch):
        ys.append(_mm(r_hat[c], hs) + y_hat[c])
        hs = _mm(trans[c], hs, passes=3) + h_add[c]
    state[...] = hs

    y = jnp.concatenate(ys, axis=0)
    mean = seg_sum(y) * (1.0 / 64)
    dlt = y - mean
    var = seg_sum(dlt * dlt) * (1.0 / 64)
    yn = dlt * lax.rsqrt(var + RWKV_GN_EPS) * lnw_ref[...] + lnb_ref[...]
    yn = yn + seg_sum(r_all * k_all * rk_ref[...]) * v_all
    o_ref[0] = (yn * g_ref[0].astype(F32)).astype(o_ref.dtype)


def wkv7(r, k, v, lw, kk, a, g, lnx_w, lnx_b, r_k, tc=512, inv_passes=1):
    b, s, d = r.shape
    tile = pl.BlockSpec((1, tc, LANES), lambda bi, pi, ti: (bi, ti, pi))
    vec = pl.BlockSpec((1, LANES), lambda bi, pi, ti: (0, pi))
    return pl.pallas_call(
        functools.partial(_wkv_kernel, tc=tc, inv_passes=inv_passes),
        grid=(b, d // LANES, s // tc),
        in_specs=[tile] * 7 + [vec] * 3,
        out_specs=tile,
        out_shape=jax.ShapeDtypeStruct((b, s, d), BF16),
        scratch_shapes=[pltpu.VMEM((LANES, LANES), F32)],
        compiler_params=_cparams(("parallel", "parallel", "arbitrary")), name="wkv7",
    )(r, k, v, lw, kk, a, g, lnx_w.reshape(1, d), lnx_b.reshape(1, d), r_k.reshape(1, d))


def _in_proj_weights(w_in):
    gw = GDN_HEADS * LANES
    c = 4 * gw
    w_qkv, w_z = w_in[:, 0:3 * gw], w_in[:, 3 * gw:c]
    d = w_in.shape[0]
    w_ba = jnp.zeros((d, LANES), w_in.dtype).at[:, :2 * GDN_HEADS].set(w_in[:, c:c + 2 * GDN_HEADS])
    c += 2 * GDN_HEADS
    nw = NSA_HEADS * NSA_DH
    w_q = w_in[:, c:c + nw].reshape(d, NSA_GROUPS, NSA_HPG, NSA_DH) * (NSA_DH ** -0.5)
    c += nw
    w_qpad = jnp.zeros((d, NSA_GROUPS, NSA_HPG, NSA_GROUPS, NSA_DH), w_in.dtype)
    for g in range(NSA_GROUPS):
        w_qpad = w_qpad.at[:, g, :, g, :].set(w_q[:, g])
    w_qpad = w_qpad.reshape(d, NSA_HEADS * LANES)
    kvw = NSA_GROUPS * NSA_DH
    w_cmp = w_in[:, c:c + 2 * kvw]
    w_kv4 = w_in[:, c + 2 * kvw:c + 6 * kvw]
    c += 6 * kvw
    w_gate = jnp.zeros((d, LANES), w_in.dtype).at[:, :3 * NSA_HEADS].set(w_in[:, c:c + 3 * NSA_HEADS])
    ws = [w_qkv, w_z, w_ba, w_qpad, w_cmp, w_kv4, w_gate]
    dts = [F32, F32, F32, BF16, F32, BF16, F32]
    return [w.astype(BF16) for w in ws], dts


def kernel(x, rel_bias, final_norm, e_attn_norm, e_w_in, e_conv_w, e_a_log, e_dt_bias, e_gdn_norm, e_cmpk_pe, e_cmpk_w1, e_cmpk_w2, e_cmpv_pe, e_cmpv_w1, e_cmpv_w2, e_w_out, e_ffn_norm, e_ffn_gate, e_ffn_up, e_ffn_down, o_attn_norm, o_mu, o_w_r, o_w_k, o_w_v, o_w_o, o_w0, o_w1, o_w2, o_a0, o_a1, o_a2, o_g1, o_g2, o_k_k, o_k_a, o_r_k, o_lnx_w, o_lnx_b, o_ffn_norm, o_ffn_gate, o_ffn_up, o_ffn_down):
    b, s, d = x.shape
    m = b * s
    xf = x.reshape(m, d)
    bf = lambda w: w.astype(BF16)

    ws, dts = _in_proj_weights(e_w_in[0])
    qkv, z, ba, qpad, kvc, kv4, gl = norm_linear(xf, e_attn_norm[0], ws, dts)
    sh = lambda t: t.reshape(b, s, t.shape[-1])
    o_gdn = gated_deltanet(sh(qkv), sh(z), sh(ba), e_conv_w[0], e_a_log[0], e_dt_bias[0], e_gdn_norm[0])
    kcvc = compress(sh(kvc), e_cmpk_pe[0], e_cmpk_w1[0], e_cmpk_w2[0], e_cmpv_pe[0], e_cmpv_w1[0], e_cmpv_w2[0])
    o_nsa = nsa_attention(sh(qpad), sh(gl), kcvc, sh(kv4), rel_bias)
    gw = GDN_HEADS * LANES
    xf = linear_res(xf, [o_gdn.reshape(m, -1), o_nsa.reshape(m, -1)], [bf(e_w_out[0][:gw]), bf(e_w_out[0][gw:])])
    xf = ffn(xf, e_ffn_norm[0], bf(e_ffn_gate[0]), bf(e_ffn_up[0]), bf(e_ffn_down[0]), final_norm, False)

    r, k, v, lw, kk, a, g = rwkv_proj(xf.reshape(b, s, d), o_attn_norm[0], o_mu[0], o_w_r[0], o_w_k[0], o_w_v[0],
                                      o_w0[0], o_w1[0], o_w2[0], o_a0[0], o_a1[0], o_a2[0], o_g1[0], o_g2[0],
                                      o_k_k[0], o_k_a[0])
    yg = wkv7(r, k, v, lw, kk, a, g, o_lnx_w[0], o_lnx_b[0], o_r_k[0].reshape(-1))
    xf = linear_res(xf, [yg.reshape(m, d)], [bf(o_w_o[0])])
    xf = ffn(xf, o_ffn_norm[0], bf(o_ffn_gate[0]), bf(o_ffn_up[0]), bf(o_ffn_down[0]), final_norm, True)
    return xf.reshape(b, s, d)
```

```python
import functools
import math

import jax
import jax.numpy as jnp
from jax import lax
from jax.experimental import pallas as pl
from jax.experimental.pallas import tpu as pltpu

F32 = jnp.float32
BF16 = jnp.bfloat16

VMEM_LIMIT_BYTES = 52 * 1024 * 1024
LANES = 128
CHUNK = 64

RMS_EPS = 1e-6
GDN_HEADS = 4
GDN_CONV = 4
NSA_HEADS = 8
NSA_GROUPS = 2
NSA_HPG = 4
NSA_DH = 64
CMP_LEN = 32
CMP_STRIDE = 16
SLC_LEN = 64
N_SELECT = 8
WINDOW = 512
NUM_BUCKETS = 32
REL_MAX_DIST = 128
NEG = -1e30
KV_PAD = 640
WIN_TILE = 640
RWKV_GN_EPS = 64e-5
BIAS_FAR_DIST = math.ceil((NUM_BUCKETS // 2) * (REL_MAX_DIST / (NUM_BUCKETS // 2))
                          ** ((NUM_BUCKETS // 2 - 1) / (NUM_BUCKETS // 2)))
CMP_TILE_ORIGIN = 64
CMP_PAST_BLOCKS = -(-(BIAS_FAR_DIST + CMP_LEN - 1) // CMP_STRIDE)
CMP_FUTURE_BLOCK = (SLC_LEN - CMP_LEN) // CMP_STRIDE + 1


def _cparams(sem):
    return pltpu.CompilerParams(dimension_semantics=sem, vmem_limit_bytes=VMEM_LIMIT_BYTES)


def _dot(a, b):
    return jnp.dot(a, b, preferred_element_type=F32)


def _dot_nt(a, b):
    return lax.dot_general(a, b, (((1,), (1,)), ((), ())), preferred_element_type=F32)


def _split(x, terms):
    out = []
    rem = x
    for _ in range(terms - 1):
        p = rem.astype(BF16)
        out.append(p)
        rem = rem - p.astype(F32)
    out.append(rem.astype(BF16))
    return out


def _mm(a, b, passes=1, nt=False):
    d = _dot_nt if nt else _dot
    if passes == 1:
        return d(a.astype(BF16), b.astype(BF16))
    ah, al = _split(a, 2)
    bh, bl = _split(b, 2)
    return d(ah, bh) + (d(ah, bl) + d(al, bh))


def _mm_exact_lhs(a01, b):
    a = a01.astype(BF16)
    b0, b1, b2 = _split(b, 3)
    return _dot(a, b0) + (_dot(a, b1) + _dot(a, b2))


def _mm_exact_rhs(a, b01):
    b = b01.astype(BF16)
    a0, a1, a2 = _split(a, 3)
    return _dot(a0, b) + (_dot(a1, b) + _dot(a2, b))


def _silu(x):
    return x * jax.nn.sigmoid(x)


def _softplus(x):
    return jnp.maximum(x, 0.0) + jnp.log1p(jnp.exp(-jnp.abs(x)))


def _lane_masks():
    lane = lax.broadcasted_iota(jnp.int32, (1, LANES), 1)
    m0 = (lane < 64).astype(F32)
    return m0, 1.0 - m0


def _stack2(x, m0, m1):
    return jnp.concatenate([x * m0, x * m1], axis=0)


def _tri_inv_cat(lcats, passes):
    m0, m1 = _lane_masks()
    i = lax.broadcasted_iota(jnp.int32, (CHUNK, LANES), 0)
    j = lax.broadcasted_iota(jnp.int32, (CHUNK, LANES), 1) & (CHUNK - 1)
    eye = (i == j).astype(F32)
    blk16 = (i >> 4) == (j >> 4)
    st = lambda x: _stack2(x, m0, m1)
    mm = lambda a, b: _mm(a, b, passes)
    each = lambda f, *ls: [f(*xs) for xs in zip(*ls)]
    ld = each(lambda l: jnp.where(blk16, l, 0.0), lcats)
    off = each(lambda l: jnp.where(blk16, 0.0, l), lcats)
    sld = each(st, ld)
    l2 = each(mm, ld, sld)
    sl2 = each(st, l2)
    n1 = each(lambda d, l, s: ((eye - d) + l) - mm(d, s), ld, l2, sl2)
    l4 = each(mm, l2, sl2)
    sl4 = each(st, l4)
    n2 = each(lambda a, s: a + mm(a, s), n1, sl4)
    l8 = each(mm, l4, sl4)
    x = each(lambda a, l: a + mm(a, st(l)), n2, l8)
    sx = each(st, x)
    mt = each(lambda a, o: mm(a, st(o)), x, off)
    m2 = each(lambda a: mm(a, st(a)), mt)
    mx = each(mm, mt, sx)
    m2x = each(mm, m2, sx)
    m3x = each(lambda a, b: mm(a, st(b)), m2, mx)
    return each(lambda a, b, c, d: (a - b) + (c - d), x, mx, m2x, m3x)


def _tri_ones(n, lower, block=None):
    i = lax.broadcasted_iota(jnp.int32, (n, n), 0)
    j = lax.broadcasted_iota(jnp.int32, (n, n), 1)
    m = (i >= j) if lower else (i <= j)
    if block is not None:
        sh = block.bit_length() - 1
        m = jnp.logical_and(m, (i >> sh) == (j >> sh))
    return m.astype(F32)


def _norm_linear_kernel(x_ref, nw_ref, *refs, n_out):
    w_refs, o_refs = refs[:n_out], refs[n_out:]
    x = x_ref[...]
    h = x * lax.rsqrt(jnp.mean(x * x, axis=-1, keepdims=True) + RMS_EPS) * nw_ref[...]
    hb = h.astype(BF16)
    for w_ref, o_ref in zip(w_refs, o_refs):
        o_ref[...] = _dot(hb, w_ref[...]).astype(o_ref.dtype)


def norm_linear(x, nw, weights, out_dtypes, tm=512):
    m, k = x.shape
    n_out = len(weights)
    in_specs = [pl.BlockSpec((tm, k), lambda i: (i, 0)), pl.BlockSpec((1, k), lambda i: (0, 0))]
    in_specs += [pl.BlockSpec(w.shape, lambda i: (0, 0)) for w in weights]
    out_specs = [pl.BlockSpec((tm, w.shape[1]), lambda i: (i, 0)) for w in weights]
    out_shape = [jax.ShapeDtypeStruct((m, w.shape[1]), dt) for w, dt in zip(weights, out_dtypes)]
    return pl.pallas_call(
        functools.partial(_norm_linear_kernel, n_out=n_out),
        grid=(m // tm,), in_specs=in_specs, out_specs=out_specs, out_shape=out_shape,
        compiler_params=_cparams(("parallel",)), name="norm_linear",
    )(x, nw.reshape(1, k), *weights)


def _linear_res_kernel(res_ref, *refs, n_in):
    a_refs, w_refs, o_ref = refs[:n_in], refs[n_in:2 * n_in], refs[2 * n_in]
    acc = res_ref[...]
    for a_ref, w_ref in zip(a_refs, w_refs):
        acc = acc + _dot(a_ref[...].astype(BF16), w_ref[...])
    o_ref[...] = acc


def linear_res(res, acts, weights, tm=512):
    m, n = res.shape
    n_in = len(acts)
    in_specs = [pl.BlockSpec((tm, n), lambda i: (i, 0))]
    in_specs += [pl.BlockSpec((tm, a.shape[1]), lambda i: (i, 0)) for a in acts]
    in_specs += [pl.BlockSpec(w.shape, lambda i: (0, 0)) for w in weights]
    return pl.pallas_call(
        functools.partial(_linear_res_kernel, n_in=n_in),
        grid=(m // tm,), in_specs=in_specs, out_specs=pl.BlockSpec((tm, n), lambda i: (i, 0)),
        out_shape=jax.ShapeDtypeStruct((m, n), F32),
        compiler_params=_cparams(("parallel",)), name="linear_res",
    )(res, *acts, *weights)


def _ffn_kernel(x_ref, nw_ref, wg_ref, wu_ref, wd_ref, fw_ref, o_ref, xn_scr, acc_scr, *, final_norm):
    f = pl.program_id(1)

    @pl.when(f == 0)
    def _():
        x = x_ref[...]
        h = x * lax.rsqrt(jnp.mean(x * x, axis=-1, keepdims=True) + RMS_EPS) * nw_ref[...]
        xn_scr[...] = h.astype(BF16)
        acc_scr[...] = jnp.zeros_like(acc_scr)

    xn = xn_scr[...]
    g = _dot(xn, wg_ref[...])
    u = _dot(xn, wu_ref[...])
    hid = (_silu(g) * u).astype(BF16)
    acc_scr[...] += _dot(hid, wd_ref[...])

    @pl.when(f == pl.num_programs(1) - 1)
    def _():
        y = x_ref[...] + acc_scr[...]
        if final_norm:
            y = y * lax.rsqrt(jnp.mean(y * y, axis=-1, keepdims=True) + RMS_EPS) * fw_ref[...]
        o_ref[...] = y


def ffn(x, nw, wg, wu, wd, fw, final_norm, tm=512, tf=1408):
    m, d = x.shape
    dff = wg.shape[1]
    return pl.pallas_call(
        functools.partial(_ffn_kernel, final_norm=final_norm),
        grid=(m // tm, dff // tf),
        in_specs=[pl.BlockSpec((tm, d), lambda i, f: (i, 0)),
                  pl.BlockSpec((1, d), lambda i, f: (0, 0)),
                  pl.BlockSpec((d, tf), lambda i, f: (0, f)),
                  pl.BlockSpec((d, tf), lambda i, f: (0, f)),
                  pl.BlockSpec((tf, d), lambda i, f: (f, 0)),
                  pl.BlockSpec((1, d), lambda i, f: (0, 0))],
        out_specs=pl.BlockSpec((tm, d), lambda i, f: (i, 0)),
        out_shape=jax.ShapeDtypeStruct((m, d), F32),
        scratch_shapes=[pltpu.VMEM((tm, d), BF16), pltpu.VMEM((tm, d), F32)],
        compiler_params=_cparams(("parallel", "arbitrary")), name="ffn",
    )(x, nw.reshape(1, d), wg, wu, wd, fw.reshape(1, d))


def _gdn_kernel(q_ref, k_ref, v_ref, qh_ref, kh_ref, vh_ref, cwq_ref, cwk_ref, cwv_ref,
                z_ref, ba_ref, bat_ref, hp_ref, nw_ref, o_ref, scr, state, *, tc, inv_passes):
    h = pl.program_id(1)
    t = pl.program_id(2)

    @pl.when(t == 0)
    def _():
        state[...] = jnp.zeros_like(state)

    not_first = (t > 0).astype(F32)

    def conv(x_ref, halo_ref, w_ref, slot):
        scr[slot, 0:8, :] = halo_ref[0] * not_first
        scr[slot, 8:8 + tc, :] = x_ref[0]
        w = w_ref[...]
        y = (w[3:4] * scr[slot, 8:8 + tc, :] + w[2:3] * scr[slot, 7:7 + tc, :]
             + w[1:2] * scr[slot, 6:6 + tc, :] + w[0:1] * scr[slot, 5:5 + tc, :])
        return _silu(y)

    q = conv(q_ref, qh_ref, cwq_ref, 0)
    k = conv(k_ref, kh_ref, cwk_ref, 1)
    v = conv(v_ref, vh_ref, cwv_ref, 2)
    qn = q * lax.rsqrt(jnp.sum(q * q, axis=-1, keepdims=True) + 1e-6)
    kn = k * lax.rsqrt(jnp.sum(k * k, axis=-1, keepdims=True) + 1e-6)

    lane = lax.broadcasted_iota(jnp.int32, (1, LANES), 1)
    ba = ba_ref[0]
    b_col = jnp.sum(jnp.where(lane == h, ba, 0.0), axis=-1, keepdims=True)
    a_col = jnp.sum(jnp.where(lane == h + GDN_HEADS, ba, 0.0), axis=-1, keepdims=True)
    beta = jax.nn.sigmoid(b_col)
    a_log = hp_ref[0, 0:1, :]
    dt_b = hp_ref[0, 1:2, :]
    g_rep = -jnp.exp(a_log) * _softplus(a_col + dt_b)

    m0, m1 = _lane_masks()
    i64 = lax.broadcasted_iota(jnp.int32, (CHUNK, LANES), 0)
    j64 = lax.broadcasted_iota(jnp.int32, (CHUNK, LANES), 1) & (CHUNK - 1)
    low_half = lax.broadcasted_iota(jnp.int32, (CHUNK, LANES), 1) < CHUNK
    ltri_bd = _tri_ones(2 * CHUNK, True, CHUNK)
    utri_bd = _tri_ones(2 * CHUNK, False, CHUNK)
    eye128 = (lax.broadcasted_iota(jnp.int32, (LANES, LANES), 0)
              == lax.broadcasted_iota(jnp.int32, (LANES, LANES), 1)).astype(F32)
    zeros64 = jnp.zeros((CHUNK, LANES), F32)
    scale = 1.0 / math.sqrt(LANES)
    nw = nw_ref[...]

    npair = tc // (2 * CHUNK)
    each = lambda f, *ls: [f(*xs) for xs in zip(*ls)]
    halves = lambda x: jnp.where(low_half, x[0:CHUNK], x[CHUNK:2 * CHUNK])
    prow = [slice(2 * CHUNK * p, 2 * CHUNK * (p + 1)) for p in range(npair)]
    kp = [kn[rs] for rs in prow]
    qp = [qn[rs] * scale for rs in prow]
    vp = [v[rs] for rs in prow]
    beta_p = [beta[rs] for rs in prow]
    gc_col = [_mm_exact_lhs(ltri_bd, g_rep[rs]) for rs in prow]
    g_row = [-jnp.exp(a_log[:, 0:1]) * _softplus(bat_ref[0, 0, p, 1:2, :] + dt_b[:, 0:1]) for p in range(npair)]
    gc_row = each(lambda x: _mm_exact_rhs(jnp.broadcast_to(x, (CHUNK, LANES)), utri_bd), g_row)
    dcat = each(lambda c_, r_: jnp.exp(jnp.where(i64 >= j64, halves(c_) - r_, -jnp.inf)), gc_col, gc_row)
    kb = each(lambda k_, b_: k_ * b_, kp, beta_p)
    gram = each(lambda x, y_: _mm(x, y_, nt=True), kb, kp)
    gq = each(lambda x, y_: _mm(x, y_, nt=True), qp, kp)
    strict = each(lambda x, d_: jnp.where(i64 > j64, halves(x) * d_, 0.0), gram, dcat)
    qk_bd = each(lambda x, d_: _stack2(jnp.where(i64 >= j64, halves(x) * d_, 0.0), m0, m1), gq, dcat)
    eg = each(jnp.exp, gc_col)
    g_last = each(lambda x: [x[CHUNK * (c + 1) - 1:CHUNK * (c + 1), :] for c in range(2)], gc_col)
    kdt = each(lambda k_, gl_, gc_: jnp.concatenate(
        [k_[CHUNK * c:CHUNK * (c + 1)] * jnp.exp(gl_[c] - gc_[CHUNK * c:CHUNK * (c + 1)]) for c in range(2)],
        axis=0).T, kp, g_last, gc_col)
    tinv = _tri_inv_cat(strict, inv_passes)
    uw = each(lambda t_, v_, b_, kb_, e_: _mm(_stack2(t_, m0, m1), jnp.concatenate([v_ * b_, kb_ * e_], axis=1)),
              tinv, vp, beta_p, kb, eg)
    u = each(lambda x: x[:, 0:LANES], uw)
    w = each(lambda x: x[:, LANES:2 * LANES], uw)
    q_hat = each(lambda q_, e_, m_, w_: q_ * e_ - _mm(m_, w_), qp, eg, qk_bd, w)
    o_hat = each(_mm, qk_bd, u)
    pad = lambda x, c: jnp.concatenate([x[0:CHUNK], zeros64] if c == 0 else [zeros64, x[CHUNK:2 * CHUNK]], axis=0)
    trans = [eye128 * jnp.exp(g_last[p][c]) - _mm(kdt[p], pad(w[p], c)) for p in range(npair) for c in range(2)]
    h_add = [_mm(kdt[p], pad(u[p], c)) for p in range(npair) for c in range(2)]
    s_state = state[...]
    for p in range(npair):
        for c in range(2):
            cr = slice(CHUNK * c, CHUNK * (c + 1))
            o = _mm(q_hat[p][cr], s_state) + o_hat[p][cr]
            s_state = _mm(trans[2 * p + c], s_state, passes=3) + h_add[2 * p + c]
            on = o * lax.rsqrt(jnp.mean(o * o, axis=-1, keepdims=True) + RMS_EPS) * nw
            r0 = 2 * CHUNK * p + CHUNK * c
            o_ref[0, r0:r0 + CHUNK, :] = (on * _silu(z_ref[0, r0:r0 + CHUNK, :])).astype(o_ref.dtype)
    state[...] = s_state


def gated_deltanet(qkv, z, ba, conv_w, a_log, dt_bias, norm_w, tc=512, inv_passes=1):
    b, s, _ = qkv.shape
    hh = GDN_HEADS
    bat = ba[..., :2 * hh].reshape(b, s // 128, 128, 2, hh).transpose(0, 4, 1, 3, 2)
    hp = jnp.zeros((hh, 8, LANES), F32)
    hp = hp.at[:, 0, :].set(a_log.astype(F32)[:, None]).at[:, 1, :].set(dt_bias.astype(F32)[:, None])
    cw = jnp.zeros((8, 3 * hh * LANES), F32).at[:GDN_CONV].set(conv_w.astype(F32))
    hb = tc // 8
    blk = lambda off: pl.BlockSpec((1, tc, LANES), lambda bi, hi, ti: (bi, ti, hi + off))
    halo = lambda off: pl.BlockSpec((1, 8, LANES), lambda bi, hi, ti: (bi, jnp.maximum(ti * hb - 1, 0), hi + off))
    cws = lambda off: pl.BlockSpec((8, LANES), lambda bi, hi, ti: (0, hi + off))
    return pl.pallas_call(
        functools.partial(_gdn_kernel, tc=tc, inv_passes=inv_passes),
        grid=(b, hh, s // tc),
        in_specs=[blk(0), blk(hh), blk(2 * hh), halo(0), halo(hh), halo(2 * hh), cws(0), cws(hh), cws(2 * hh),
                  blk(0),
                  pl.BlockSpec((1, tc, LANES), lambda bi, hi, ti: (bi, ti, 0)),
                  pl.BlockSpec((1, 1, tc // 128, 2, LANES), lambda bi, hi, ti: (bi, hi, ti, 0, 0)),
                  pl.BlockSpec((1, 8, LANES), lambda bi, hi, ti: (hi, 0, 0)),
                  pl.BlockSpec((1, LANES), lambda bi, hi, ti: (0, 0))],
        out_specs=blk(0),
        out_shape=jax.ShapeDtypeStruct((b, s, hh * LANES), BF16),
        scratch_shapes=[pltpu.VMEM((3, tc + 8, LANES), F32), pltpu.VMEM((LANES, LANES), F32)],
        compiler_params=_cparams(("parallel", "parallel", "arbitrary")), name="gated_deltanet",
    )(qkv, qkv, qkv, qkv, qkv, qkv, cw, cw, cw, z, ba, bat, hp, norm_w.astype(F32).reshape(1, LANES))


def _compress_kernel(x_ref, pe_ref, w1_ref, w2_ref, o_ref):
    nb = x_ref.shape[3]
    acc = jnp.zeros((nb, LANES), F32)
    for g in range(NSA_GROUPS):
        x = x_ref[0, 0, g]
        first = _dot((x + pe_ref[0, 0]).astype(BF16), w1_ref[0, 0])
        second = _dot((x + pe_ref[0, 1]).astype(BF16), w1_ref[0, 1])
        hid = _silu(first + pltpu.roll(second, nb - 1, 0))
        acc = acc + _dot(hid.astype(BF16), w2_ref[0, g])
    o_ref[0, 0] = acc.astype(o_ref.dtype)


def compress(kvc, pe_k, w1_k, w2_k, pe_v, w1_v, w2_v):
    b, s, _ = kvc.shape
    nb = s // CMP_STRIDE
    width = CMP_STRIDE * NSA_DH
    x = kvc.reshape(b, nb, CMP_STRIDE, 2, NSA_GROUPS, NSA_DH).transpose(0, 3, 4, 1, 2, 5).reshape(b, 2, NSA_GROUPS, nb, width)
    pe = jnp.stack([pe_k, pe_v]).astype(F32).reshape(2, 2, 1, width)
    w1 = jnp.stack([w1_k, w1_v]).astype(BF16).reshape(2, 2, width, -1)
    hid = w1.shape[-1]
    w2 = jnp.stack([w2_k, w2_v]).astype(BF16)
    w2p = jnp.zeros((2, NSA_GROUPS, hid, LANES), BF16)
    for g in range(NSA_GROUPS):
        w2p = w2p.at[:, g, :, NSA_DH * g:NSA_DH * (g + 1)].set(w2)
    return pl.pallas_call(
        _compress_kernel,
        grid=(b, 2),
        in_specs=[pl.BlockSpec((1, 1, NSA_GROUPS, nb, width), lambda bi, j: (bi, j, 0, 0, 0)),
                  pl.BlockSpec((1, 2, 1, width), lambda bi, j: (j, 0, 0, 0)),
                  pl.BlockSpec((1, 2, width, hid), lambda bi, j: (j, 0, 0, 0)),
                  pl.BlockSpec((1, NSA_GROUPS, hid, LANES), lambda bi, j: (j, 0, 0, 0))],
        out_specs=pl.BlockSpec((1, 1, nb, LANES), lambda bi, j: (bi, j, 0, 0)),
        out_shape=jax.ShapeDtypeStruct((b, 2, nb, LANES), BF16),
        compiler_params=_cparams(("parallel", "parallel")), name="nsa_compress",
    )(x, pe, w1, w2p)


def _softmax_pv(s, v):
    m = jnp.max(s, axis=-1, keepdims=True)
    p = jnp.exp(s - m)
    l = jnp.sum(p, axis=-1, keepdims=True)
    return _dot(p.astype(BF16), v) / l


def _nsa_kernel(q_ref, gl_ref, kcvc_ref, kv_ref, bw_ref, bfar_ref, bc_ref, ovl_ref, o_ref):
    n = pl.program_id(1)
    qb_rows = q_ref.shape[1]
    ns = (kv_ref.shape[1] - KV_PAD) // SLC_LEN
    far_w = 8 * SLC_LEN
    lane = lax.broadcasted_iota(jnp.int32, (qb_rows, LANES), 1)
    lane_s = lane & (ns - 1)
    gates = jax.nn.sigmoid(gl_ref[0])
    kc = kcvc_ref[0, 0]
    vc = kcvc_ref[0, 1]
    ovl = ovl_ref[...]
    row_blk = lax.broadcasted_iota(jnp.int32, (LANES, far_w), 0)
    col_blk = lax.broadcasted_iota(jnp.int32, (LANES, far_w), 1) >> 6
    wcol = lax.broadcasted_iota(jnp.int32, (NSA_HPG * qb_rows, WIN_TILE), 1)
    groups = range(NSA_GROUPS)
    tile4 = lambda x: jnp.concatenate([x] * NSA_HPG, axis=0)

    qs = [jnp.concatenate([q_ref[0, :, LANES * (NSA_HPG * g + hh):LANES * (NSA_HPG * g + hh + 1)]
                           for hh in range(NSA_HPG)], axis=0) for g in groups]
    c0 = (qb_rows // CMP_STRIDE) * n
    lane4 = lax.broadcasted_iota(jnp.int32, (NSA_HPG * qb_rows, LANES), 1)
    shift = (c0 + (LANES - CMP_TILE_ORIGIN)) % LANES
    bc = [jnp.where(lane4 < c0 - (CMP_PAST_BLOCKS - 1), bfar_ref[g],
                    jnp.where(lane4 >= c0 + CMP_FUTURE_BLOCK, NEG, pltpu.roll(bc_ref[g], shift, 1))) for g in groups]
    start = pl.multiple_of(n * SLC_LEN + (KV_PAD - 3 * SLC_LEN), SLC_LEN)
    k_near = kv_ref[0, pl.ds(start, 4 * SLC_LEN), 0:LANES]
    v_near = kv_ref[0, pl.ds(start, 4 * SLC_LEN), LANES:2 * LANES]
    wst = pl.multiple_of(n * SLC_LEN + (KV_PAD + SLC_LEN - WIN_TILE), SLC_LEN)
    k_w = kv_ref[0, pl.ds(wst, WIN_TILE), 2 * LANES:3 * LANES]
    v_w = kv_ref[0, pl.ds(wst, WIN_TILE), 3 * LANES:4 * LANES]
    s_cmp = [_dot_nt(qs[g], kc) + bc[g] for g in groups]
    s_near = [_dot_nt(qs[g], k_near) + bw_ref[g, :, WIN_TILE - 4 * SLC_LEN:WIN_TILE] for g in groups]
    s_win = [jnp.where(wcol >= (WIN_TILE - SLC_LEN) - SLC_LEN * n, _dot_nt(qs[g], k_w) + bw_ref[g], NEG)
             for g in groups]

    p_cmp = []
    for g in groups:
        m = jnp.max(s_cmp[g], axis=-1, keepdims=True)
        e = jnp.where(bc[g] > 0.5 * NEG, jnp.exp(s_cmp[g] - m), 0.0)
        p_cmp.append(e / jnp.maximum(jnp.sum(e, axis=-1, keepdims=True), 1e-30))
    o_cmp = [_dot(p.astype(BF16), vc) for p in p_cmp]
    imp = [_mm_exact_rhs(p[0:qb_rows] + p[qb_rows:2 * qb_rows] + p[2 * qb_rows:3 * qb_rows] + p[3 * qb_rows:4 * qb_rows],
                         ovl) for p in p_cmp]

    forced = jnp.logical_or(jnp.logical_or(lane_s == 0, lane_s == n), lane_s == n - 1)
    val = [jnp.where(lane_s > n, -1.0, jnp.where(forced, 1e6, x)) for x in imp]
    rank = [jnp.zeros((qb_rows, LANES), F32) for _ in groups]
    for r in range(1, ns):
        tie = (lane_s >= r).astype(F32)
        for g in groups:
            other = pltpu.roll(val[g], r, 1)
            rank[g] = rank[g] + jnp.where(other > val[g], 1.0, jnp.where(other == val[g], tie, 0.0))
    sel = [jnp.where(jnp.logical_and(x < N_SELECT, lane < ns), 1.0, 0.0) for x in rank]

    o_win = [_softmax_pv(s_win[g], v_w) for g in groups]

    e_near = (lax.broadcasted_iota(jnp.int32, (LANES, 4 * SLC_LEN), 0)
              == (lax.broadcasted_iota(jnp.int32, (LANES, 4 * SLC_LEN), 1) >> 6) + (n - 3)).astype(BF16)
    carry = []
    for g in groups:
        sm = _dot(sel[g].astype(BF16), e_near)
        s = jnp.where(tile4(sm) > 0.5, s_near[g], NEG)
        m = jnp.max(s, axis=-1, keepdims=True)
        p = jnp.exp(s - m)
        carry += [m, jnp.sum(p, axis=-1, keepdims=True), _dot(p.astype(BF16), v_near)]
    sel_far = [jnp.where(lane <= n - 4, x, 0.0).astype(BF16) for x in sel]
    b_far = [jnp.concatenate([bfar_ref[g]] * (far_w // LANES), axis=1) for g in groups]

    def far_body(it, carry):
        st = pl.multiple_of(KV_PAD + it * far_w, math.gcd(KV_PAD, far_w))
        k_c = kv_ref[0, pl.ds(st, far_w), 0:LANES]
        v_c = kv_ref[0, pl.ds(st, far_w), LANES:2 * LANES]
        e_c = (row_blk == col_blk + 8 * it).astype(BF16)
        sc = [_dot_nt(qs[g], k_c) + b_far[g] for g in groups]
        smc = [_dot(sel_far[g], e_c) for g in groups]
        out = []
        for g in groups:
            m_i, l_i, acc_i = carry[3 * g:3 * g + 3]
            s_g = jnp.where(tile4(smc[g]) > 0.5, sc[g], NEG)
            m_n = jnp.maximum(m_i, jnp.max(s_g, axis=-1, keepdims=True))
            alpha = jnp.exp(m_i - m_n)
            pc = jnp.exp(s_g - m_n)
            out += [m_n, alpha * l_i + jnp.sum(pc, axis=-1, keepdims=True),
                    alpha * acc_i + _dot(pc.astype(BF16), v_c)]
        return tuple(out)

    carry = lax.fori_loop(0, (n // 4 + 1) // 2, far_body, tuple(carry))
    o_slc = [carry[3 * g + 2] / carry[3 * g + 1] for g in groups]

    for g in groups:
        pieces = []
        for hh in range(NSA_HPG):
            hd = NSA_HPG * g + hh
            rs = slice(qb_rows * hh, qb_rows * (hh + 1))
            pieces.append(gates[:, 3 * hd:3 * hd + 1] * o_cmp[g][rs] + gates[:, 3 * hd + 1:3 * hd + 2] * o_slc[g][rs]
                          + gates[:, 3 * hd + 2:3 * hd + 3] * o_win[g][rs])
        for hp in range(2):
            lo, hi = pieces[2 * hp], pieces[2 * hp + 1]
            if g == 0:
                hi = pltpu.roll(hi, NSA_DH, 1)
            else:
                lo = pltpu.roll(lo, NSA_DH, 1)
            o_ref[0, :, LANES * (2 * g + hp):LANES * (2 * g + hp + 1)] = jnp.where(lane < NSA_DH, lo, hi).astype(o_ref.dtype)


def _rel_bucket(dist):
    n = jnp.maximum(dist, 0)
    exact = NUM_BUCKETS // 2
    nf = jnp.maximum(n, 1).astype(F32)
    large = exact + (jnp.log(nf / exact) / math.log(REL_MAX_DIST / exact) * (NUM_BUCKETS - exact)).astype(jnp.int32)
    return jnp.where(n < exact, n, jnp.minimum(large, NUM_BUCKETS - 1))


def nsa_attention(qpad, gl, kcvc, kv4, rel_bias):
    b, s, _ = qpad.shape
    qb = SLC_LEN
    nq = s // qb
    nc = s // CMP_STRIDE - CMP_LEN // CMP_STRIDE + 1
    ns = s // SLC_LEN
    table = rel_bias.astype(F32)
    dist_w = jnp.arange(qb)[:, None] + (WIN_TILE - SLC_LEN) - jnp.arange(WIN_TILE)[None, :]
    ok_w = jnp.logical_and(dist_w >= 0, dist_w < WINDOW)
    bw = jnp.where(ok_w[None], table[_rel_bucket(dist_w)].transpose(2, 0, 1), NEG)
    bw = bw.reshape(NSA_GROUPS, NSA_HPG * qb, WIN_TILE)
    far_val = table[NUM_BUCKETS - 1]
    bfar = jnp.broadcast_to(far_val[:, None, None], (NSA_HEADS, qb, LANES)).reshape(NSA_GROUPS, NSA_HPG * qb, LANES)
    cmp_end = jnp.arange(LANES) * CMP_STRIDE + CMP_LEN - 1
    dist_c = jnp.arange(qb)[:, None] - CMP_STRIDE * (jnp.arange(LANES)[None, :] - CMP_TILE_ORIGIN) - (CMP_LEN - 1)
    bc = jnp.where((dist_c >= 0)[None], table[_rel_bucket(dist_c)].transpose(2, 0, 1), NEG)
    bc = bc.reshape(NSA_GROUPS, NSA_HPG * qb, LANES)
    cmp_start = jnp.arange(LANES) * CMP_STRIDE
    slc_j = jnp.arange(LANES) % ns
    ovl = jnp.logical_and(jnp.logical_and(cmp_start[:, None] < (slc_j[None, :] + 1) * SLC_LEN,
                                          cmp_end[:, None] >= slc_j[None, :] * SLC_LEN),
                          jnp.arange(LANES)[:, None] < nc).astype(F32)
    kvp = jnp.pad(kv4, ((0, 0), (KV_PAD, 0), (0, 0)))
    sp = s + KV_PAD
    return pl.pallas_call(
        _nsa_kernel,
        grid=(b, nq),
        in_specs=[pl.BlockSpec((1, qb, NSA_HEADS * LANES), lambda bi, n: (bi, n, 0)),
                  pl.BlockSpec((1, qb, LANES), lambda bi, n: (bi, n, 0)),
                  pl.BlockSpec((1, 2, s // CMP_STRIDE, LANES), lambda bi, n: (bi, 0, 0, 0)),
                  pl.BlockSpec((1, sp, 4 * LANES), lambda bi, n: (bi, 0, 0)),
                  pl.BlockSpec((NSA_GROUPS, NSA_HPG * qb, WIN_TILE), lambda bi, n: (0, 0, 0)),
                  pl.BlockSpec((NSA_GROUPS, NSA_HPG * qb, LANES), lambda bi, n: (0, 0, 0)),
                  pl.BlockSpec((NSA_GROUPS, NSA_HPG * qb, LANES), lambda bi, n: (0, 0, 0)),
                  pl.BlockSpec((LANES, LANES), lambda bi, n: (0, 0))],
        out_specs=pl.BlockSpec((1, qb, NSA_HEADS * NSA_DH), lambda bi, n: (bi, n, 0)),
        out_shape=jax.ShapeDtypeStruct((b, s, NSA_HEADS * NSA_DH), BF16),
        compiler_params=_cparams(("parallel", "arbitrary")), name="nsa_attention",
    )(qpad, gl, kcvc, kvp, bw, bfar, bc, ovl)


def _rwkv_proj_kernel(x_ref, halo_ref, nw_ref, mu_ref, vec_ref, wr_ref, wk_ref, wv_ref,
                      w1_ref, w2_ref, a1_ref, a2_ref, g1_ref, g2_ref,
                      r_ref, k_ref, v_ref, lw_ref, kk_ref, a_ref, g_ref, scr, *, tm):
    t = pl.program_id(1)
    nw = nw_ref[...]

    def norm(x):
        return x * lax.rsqrt(jnp.mean(x * x, axis=-1, keepdims=True) + RMS_EPS) * nw

    h = norm(x_ref[0])
    scr[0:8, :] = norm(halo_ref[0]) * (t > 0).astype(F32)
    scr[8:8 + tm, :] = h
    xx = scr[7:7 + tm, :] - h
    mix = lambda j: (h + xx * mu_ref[j:j + 1, :]).astype(BF16)
    w0, a0, k_k, k_a = vec_ref[0:1, :], vec_ref[1:2, :], vec_ref[2:3, :], vec_ref[3:4, :]
    r = _dot(mix(0), wr_ref[...])
    wl = w0 + _dot(jnp.tanh(_dot(mix(1), w1_ref[...])).astype(BF16), w2_ref[...])
    k = _dot(mix(2), wk_ref[...])
    v = _dot(mix(3), wv_ref[...])
    a = jax.nn.sigmoid(a0 + _dot(_dot(mix(4), a1_ref[...]).astype(BF16), a2_ref[...]))
    g = _dot(jax.nn.sigmoid(_dot(mix(5), g1_ref[...])).astype(BF16), g2_ref[...])
    w_raw = -_softplus(-wl) - 0.5
    r_ref[0] = r.astype(r_ref.dtype)
    k_ref[0] = (k * (1.0 + (a - 1.0) * k_a)).astype(k_ref.dtype)
    v_ref[0] = v.astype(v_ref.dtype)
    lw_ref[0] = -jnp.exp(w_raw)
    kk_ref[0] = (k * k_k).astype(kk_ref.dtype)
    a_ref[0] = a.astype(a_ref.dtype)
    g_ref[0] = g.astype(g_ref.dtype)


def rwkv_proj(x, nw, mu, w_r, w_k, w_v, w0, w1, w2, a0, a1, a2, g1, g2, k_k, k_a, tm=256):
    b, s, d = x.shape
    pad_to = lambda w, rows, cols: jnp.zeros((rows, cols), BF16).at[:w.shape[0], :w.shape[1]].set(w.astype(BF16))
    lora = lambda n: -(-n // LANES) * LANES
    mu8 = jnp.zeros((8, d), F32).at[:6].set(mu.astype(F32))
    vec = jnp.zeros((8, d), F32).at[0].set(w0).at[1].set(a0).at[2].set(k_k).at[3].set(k_a)
    big = [w.astype(BF16) for w in (w_r, w_k, w_v)]
    small = [pad_to(w1, d, lora(w1.shape[1])), pad_to(w2, lora(w2.shape[0]), d),
             pad_to(a1, d, lora(a1.shape[1])), pad_to(a2, lora(a2.shape[0]), d),
             pad_to(g1, d, lora(g1.shape[1])), pad_to(g2, lora(g2.shape[0]), d)]
    whole = lambda w: pl.BlockSpec(w.shape, lambda bi, ti: (0, 0))
    hb = tm // 8
    tile = pl.BlockSpec((1, tm, d), lambda bi, ti: (bi, ti, 0))
    return pl.pallas_call(
        functools.partial(_rwkv_proj_kernel, tm=tm),
        grid=(b, s // tm),
        in_specs=[tile, pl.BlockSpec((1, 8, d), lambda bi, ti: (bi, jnp.maximum(ti * hb - 1, 0), 0)),
                  pl.BlockSpec((1, d), lambda bi, ti: (0, 0)), whole(mu8), whole(vec)]
                 + [whole(w) for w in big] + [whole(w) for w in small],
        out_specs=[tile] * 7,
        out_shape=[jax.ShapeDtypeStruct((b, s, d), dt) for dt in (BF16, BF16, BF16, F32, BF16, BF16, BF16)],
        scratch_shapes=[pltpu.VMEM((tm + 8, d), F32)],
        compiler_params=_cparams(("parallel", "arbitrary")), name="rwkv_proj",
    )(x, x, nw.reshape(1, d), mu8, vec, *big, *small)


def _wkv_kernel(r_ref, k_ref, v_ref, lw_ref, kk_ref, a_ref, g_ref, lnw_ref, lnb_ref, rk_ref,
                o_ref, state, *, tc, inv_passes):
    t = pl.program_id(2)

    @pl.when(t == 0)
    def _():
        state[...] = jnp.zeros_like(state)

    m0, m1 = _lane_masks()
    lane = lax.broadcasted_iota(jnp.int32, (1, LANES), 1)
    low = lane < 64

    def seg_sum(x):
        return jnp.where(low, jnp.sum(x * m0, axis=-1, keepdims=True), jnp.sum(x * m1, axis=-1, keepdims=True))

    r_all, k_all, v_all = (x[0].astype(F32) for x in (r_ref, k_ref, v_ref))
    kk = kk_ref[0].astype(F32)
    kk = kk * lax.rsqrt(seg_sum(kk * kk) + 1e-12)
    a_all = -kk
    b_all = kk * a_ref[0].astype(F32)

    i64 = lax.broadcasted_iota(jnp.int32, (CHUNK, LANES), 0)
    j64 = lax.broadcasted_iota(jnp.int32, (CHUNK, LANES), 1) & (CHUNK - 1)
    strict_m = i64 > j64
    incl_m = i64 >= j64
    ltri = _tri_ones(CHUNK, True)
    eye128 = (lax.broadcasted_iota(jnp.int32, (LANES, LANES), 0)
              == lax.broadcasted_iota(jnp.int32, (LANES, LANES), 1)).astype(F32)
    bd = ((lax.broadcasted_iota(jnp.int32, (LANES, LANES), 0) >> 6)
          == (lax.broadcasted_iota(jnp.int32, (LANES, LANES), 1) >> 6))
    zeros64 = jnp.zeros((CHUNK, LANES), F32)
    st = lambda x: _stack2(x, m0, m1)

    nch = tc // CHUNK
    each = lambda f, *ls: [f(*xs) for xs in zip(*ls)]
    rows = [slice(CHUNK * c, CHUNK * (c + 1)) for c in range(nch)]
    lw = [lw_ref[0, rs, :] for rs in rows]
    r, k, v = ([x[rs] for rs in rows] for x in (r_all, k_all, v_all))
    av, bv = ([x[rs] for rs in rows] for x in (a_all, b_all))
    cs = each(lambda x: _mm_exact_lhs(ltri, x), lw)
    cl = each(lambda x: x[CHUNK - 1:CHUNK, :], cs)
    a_t = each(lambda a_, c_, l_: a_ * jnp.exp(c_ - l_), av, cs, lw)
    r_t = each(lambda r_, c_: r_ * jnp.exp(c_), r, cs)
    inv_w = each(lambda c_: jnp.exp(-c_), cs)
    b_t = each(lambda x, w_: x * w_, bv, inv_w)
    k_t = each(lambda x, w_: x * w_, k, inv_w)
    to_end = each(lambda e_, c_: jnp.exp(e_ - c_), cl, cs)
    bkt = each(lambda b_, k_, e_: jnp.concatenate([b_ * e_, k_ * e_], axis=0).T, bv, k, to_end)
    ar = each(lambda a_, r_: jnp.concatenate([a_, r_], axis=0), a_t, r_t)
    gb = each(lambda x, y_: _mm(x, st(y_), nt=True), ar, b_t)
    gk = each(lambda x, y_: _mm(x, st(y_), nt=True), ar, k_t)
    l_ab = each(lambda x: jnp.where(strict_m, -x[0:CHUNK], 0.0), gb)
    l_ak = each(lambda x: jnp.where(strict_m, x[0:CHUNK], 0.0), gk)
    m_rb = each(lambda x: jnp.where(incl_m, x[CHUNK:2 * CHUNK], 0.0), gb)
    m_rk = each(lambda x: jnp.where(incl_m, x[CHUNK:2 * CHUNK], 0.0), gk)
    vst = each(st, v)
    lakv = each(_mm, l_ak, vst)
    tinv = _tri_inv_cat(l_ab, inv_passes)
    p_mat = each(lambda t_, a_: _mm(t_, st(a_)), tinv, a_t)
    q_mat = each(lambda t_, x: _mm(t_, st(x)), tinv, lakv)
    r_hat = each(lambda r_, m_, p_: r_ + _mm(m_, st(p_)), r_t, m_rb, p_mat)
    y_hat = each(lambda mb, q_, mk, vs: _mm(mb, st(q_)) + _mm(mk, vs), m_rb, q_mat, m_rk, vst)
    trans = each(lambda e_, bk, p_: eye128 * jnp.exp(e_)
                 + jnp.where(bd, _mm(bk, jnp.concatenate([p_, zeros64], axis=0)), 0.0), cl, bkt, p_mat)
    h_add = each(lambda bk, q_, v_: jnp.where(bd, _mm(bk, jnp.concatenate([q_, v_], axis=0)), 0.0), bkt, q_mat, v)
    hs = state[...]
    ys = []
    for c in range(nch):
        ys.append(_mm(r_hat[c], hs) + y_hat[c])
        hs = _mm(trans[c], hs, passes=3) + h_add[c]
    state[...] = hs

    y = jnp.concatenate(ys, axis=0)
    mean = seg_sum(y) * (1.0 / 64)
    dlt = y - mean
    var = seg_sum(dlt * dlt) * (1.0 / 64)
    yn = dlt * lax.rsqrt(var + RWKV_GN_EPS) * lnw_ref[...] + lnb_ref[...]
    yn = yn + seg_sum(r_all * k_all * rk_ref[...]) * v_all
    o_ref[0] = (yn * g_ref[0].astype(F32)).astype(o_ref.dtype)


def wkv7(r, k, v, lw, kk, a, g, lnx_w, lnx_b, r_k, tc=512, inv_passes=1):
    b, s, d = r.shape
    tile = pl.BlockSpec((1, tc, LANES), lambda bi, pi, ti: (bi, ti, pi))
    vec = pl.BlockSpec((1, LANES), lambda bi, pi, ti: (0, pi))
    return pl.pallas_call(
        functools.partial(_wkv_kernel, tc=tc, inv_passes=inv_passes),
        grid=(b, d // LANES, s // tc),
        in_specs=[tile] * 7 + [vec] * 3,
        out_specs=tile,
        out_shape=jax.ShapeDtypeStruct((b, s, d), BF16),
        scratch_shapes=[pltpu.VMEM((LANES, LANES), F32)],
        compiler_params=_cparams(("parallel", "parallel", "arbitrary")), name="wkv7",
    )(r, k, v, lw, kk, a, g, lnx_w.reshape(1, d), lnx_b.reshape(1, d), r_k.reshape(1, d))


def _in_proj_weights(w_in):
    gw = GDN_HEADS * LANES
    c = 4 * gw
    w_qkv, w_z = w_in[:, 0:3 * gw], w_in[:, 3 * gw:c]
    d = w_in.shape[0]
    w_ba = jnp.zeros((d, LANES), w_in.dtype).at[:, :2 * GDN_HEADS].set(w_in[:, c:c + 2 * GDN_HEADS])
    c += 2 * GDN_HEADS
    nw = NSA_HEADS * NSA_DH
    w_q = w_in[:, c:c + nw].reshape(d, NSA_GROUPS, NSA_HPG, NSA_DH) * (NSA_DH ** -0.5)
    c += nw
    w_qpad = jnp.zeros((d, NSA_GROUPS, NSA_HPG, NSA_GROUPS, NSA_DH), w_in.dtype)
    for g in range(NSA_GROUPS):
        w_qpad = w_qpad.at[:, g, :, g, :].set(w_q[:, g])
    w_qpad = w_qpad.reshape(d, NSA_HEADS * LANES)
    kvw = NSA_GROUPS * NSA_DH
    w_cmp = w_in[:, c:c + 2 * kvw]
    w_kv4 = w_in[:, c + 2 * kvw:c + 6 * kvw]
    c += 6 * kvw
    w_gate = jnp.zeros((d, LANES), w_in.dtype).at[:, :3 * NSA_HEADS].set(w_in[:, c:c + 3 * NSA_HEADS])
    ws = [w_qkv, w_z, w_ba, w_qpad, w_cmp, w_kv4, w_gate]
    dts = [F32, F32, F32, BF16, F32, BF16, F32]
    return [w.astype(BF16) for w in ws], dts


def kernel(x, rel_bias, final_norm, e_attn_norm, e_w_in, e_conv_w, e_a_log, e_dt_bias, e_gdn_norm, e_cmpk_pe, e_cmpk_w1, e_cmpk_w2, e_cmpv_pe, e_cmpv_w1, e_cmpv_w2, e_w_out, e_ffn_norm, e_ffn_gate, e_ffn_up, e_ffn_down, o_attn_norm, o_mu, o_w_r, o_w_k, o_w_v, o_w_o, o_w0, o_w1, o_w2, o_a0, o_a1, o_a2, o_g1, o_g2, o_k_k, o_k_a, o_r_k, o_lnx_w, o_lnx_b, o_ffn_norm, o_ffn_gate, o_ffn_up, o_ffn_down):
    b, s, d = x.shape
    m = b * s
    xf = x.reshape(m, d)
    bf = lambda w: w.astype(BF16)

    ws, dts = _in_proj_weights(e_w_in[0])
    qkv, z, ba, qpad, kvc, kv4, gl = norm_linear(xf, e_attn_norm[0], ws, dts)
    sh = lambda t: t.reshape(b, s, t.shape[-1])
    o_gdn = gated_deltanet(sh(qkv), sh(z), sh(ba), e_conv_w[0], e_a_log[0], e_dt_bias[0], e_gdn_norm[0])
    kcvc = compress(sh(kvc), e_cmpk_pe[0], e_cmpk_w1[0], e_cmpk_w2[0], e_cmpv_pe[0], e_cmpv_w1[0], e_cmpv_w2[0])
    o_nsa = nsa_attention(sh(qpad), sh(gl), kcvc, sh(kv4), rel_bias)
    gw = GDN_HEADS * LANES
    xf = linear_res(xf, [o_gdn.reshape(m, -1), o_nsa.reshape(m, -1)], [bf(e_w_out[0][:gw]), bf(e_w_out[0][gw:])])
    xf = ffn(xf, e_ffn_norm[0], bf(e_ffn_gate[0]), bf(e_ffn_up[0]), bf(e_ffn_down[0]), final_norm, False)

    r, k, v, lw, kk, a, g = rwkv_proj(xf.reshape(b, s, d), o_attn_norm[0], o_mu[0], o_w_r[0], o_w_k[0], o_w_v[0],
                                      o_w0[0], o_w1[0], o_w2[0], o_a0[0], o_a1[0], o_a2[0], o_g1[0], o_g2[0],
                                      o_k_k[0], o_k_a[0])
    yg = wkv7(r, k, v, lw, kk, a, g, o_lnx_w[0], o_lnx_b[0], o_r_k[0].reshape(-1))
    xf = linear_res(xf, [yg.reshape(m, d)], [bf(o_w_o[0])])
    xf = ffn(xf, o_ffn_norm[0], bf(o_ffn_gate[0]), bf(o_ffn_up[0]), bf(o_ffn_down[0]), final_norm, True)
    return xf.reshape(b, s, d)
```

```python
import functools
import math

import jax
import jax.numpy as jnp
from jax import lax
from jax.experimental import pallas as pl
from jax.experimental.pallas import tpu as pltpu

F32 = jnp.float32
BF16 = jnp.bfloat16

VMEM_LIMIT_BYTES = 52 * 1024 * 1024
LANES = 128
CHUNK = 64

RMS_EPS = 1e-6
GDN_HEADS = 4
GDN_CONV = 4
NSA_HEADS = 8
NSA_GROUPS = 2
NSA_HPG = 4
NSA_DH = 64
CMP_LEN = 32
CMP_STRIDE = 16
SLC_LEN = 64
N_SELECT = 8
WINDOW = 512
NUM_BUCKETS = 32
REL_MAX_DIST = 128
NEG = -1e30
KV_PAD = 640
WIN_TILE = 640
RWKV_GN_EPS = 64e-5
BIAS_FAR_DIST = math.ceil((NUM_BUCKETS // 2) * (REL_MAX_DIST / (NUM_BUCKETS // 2))
                          ** ((NUM_BUCKETS // 2 - 1) / (NUM_BUCKETS // 2)))
CMP_TILE_ORIGIN = 64
CMP_PAST_BLOCKS = -(-(BIAS_FAR_DIST + CMP_LEN - 1) // CMP_STRIDE)
CMP_FUTURE_BLOCK = (SLC_LEN - CMP_LEN) // CMP_STRIDE + 1


def _cparams(sem):
    return pltpu.CompilerParams(dimension_semantics=sem, vmem_limit_bytes=VMEM_LIMIT_BYTES)


def _dot(a, b):
    return jnp.dot(a, b, preferred_element_type=F32)


def _dot_nt(a, b):
    return lax.dot_general(a, b, (((1,), (1,)), ((), ())), preferred_element_type=F32)


def _split(x, terms):
    out = []
    rem = x
    for _ in range(terms - 1):
        p = rem.astype(BF16)
        out.append(p)
        rem = rem - p.astype(F32)
    out.append(rem.astype(BF16))
    return out


def _mm(a, b, passes=1, nt=False):
    d = _dot_nt if nt else _dot
    if passes == 1:
        return d(a.astype(BF16), b.astype(BF16))
    ah, al = _split(a, 2)
    bh, bl = _split(b, 2)
    return d(ah, bh) + (d(ah, bl) + d(al, bh))


def _mm_exact_lhs(a01, b):
    a = a01.astype(BF16)
    b0, b1, b2 = _split(b, 3)
    return _dot(a, b0) + (_dot(a, b1) + _dot(a, b2))


def _mm_exact_rhs(a, b01):
    b = b01.astype(BF16)
    a0, a1, a2 = _split(a, 3)
    return _dot(a0, b) + (_dot(a1, b) + _dot(a2, b))


def _silu(x):
    return x * jax.nn.sigmoid(x)


def _softplus(x):
    return jnp.maximum(x, 0.0) + jnp.log1p(jnp.exp(-jnp.abs(x)))


def _lane_masks():
    lane = lax.broadcasted_iota(jnp.int32, (1, LANES), 1)
    m0 = (lane < 64).astype(F32)
    return m0, 1.0 - m0


def _stack2(x, m0, m1):
    return jnp.concatenate([x * m0, x * m1], axis=0)


def _tri_inv_cat(lcats, passes):
    m0, m1 = _lane_masks()
    i = lax.broadcasted_iota(jnp.int32, (CHUNK, LANES), 0)
    j = lax.broadcasted_iota(jnp.int32, (CHUNK, LANES), 1) & (CHUNK - 1)
    eye = (i == j).astype(F32)
    blk16 = (i >> 4) == (j >> 4)
    st = lambda x: _stack2(x, m0, m1)
    mm = lambda a, b: _mm(a, b, passes)
    each = lambda f, *ls: [f(*xs) for xs in zip(*ls)]
    ld = each(lambda l: jnp.where(blk16, l, 0.0), lcats)
    off = each(lambda l: jnp.where(blk16, 0.0, l), lcats)
    sld = each(st, ld)
    l2 = each(mm, ld, sld)
    sl2 = each(st, l2)
    n1 = each(lambda d, l, s: ((eye - d) + l) - mm(d, s), ld, l2, sl2)
    l4 = each(mm, l2, sl2)
    sl4 = each(st, l4)
    n2 = each(lambda a, s: a + mm(a, s), n1, sl4)
    l8 = each(mm, l4, sl4)
    x = each(lambda a, l: a + mm(a, st(l)), n2, l8)
    sx = each(st, x)
    mt = each(lambda a, o: mm(a, st(o)), x, off)
    m2 = each(lambda a: mm(a, st(a)), mt)
    mx = each(mm, mt, sx)
    m2x = each(mm, m2, sx)
    m3x = each(lambda a, b: mm(a, st(b)), m2, mx)
    return each(lambda a, b, c, d: (a - b) + (c - d), x, mx, m2x, m3x)


def _tri_ones(n, lower, block=None):
    i = lax.broadcasted_iota(jnp.int32, (n, n), 0)
    j = lax.broadcasted_iota(jnp.int32, (n, n), 1)
    m = (i >= j) if lower else (i <= j)
    if block is not None:
        sh = block.bit_length() - 1
        m = jnp.logical_and(m, (i >> sh) == (j >> sh))
    return m.astype(F32)


def _norm_linear_kernel(x_ref, nw_ref, *refs, n_out):
    w_refs, o_refs = refs[:n_out], refs[n_out:]
    x = x_ref[...]
    h = x * lax.rsqrt(jnp.mean(x * x, axis=-1, keepdims=True) + RMS_EPS) * nw_ref[...]
    hb = h.astype(BF16)
    for w_ref, o_ref in zip(w_refs, o_refs):
        o_ref[...] = _dot(hb, w_ref[...]).astype(o_ref.dtype)


def norm_linear(x, nw, weights, out_dtypes, tm=512):
    m, k = x.shape
    n_out = len(weights)
    in_specs = [pl.BlockSpec((tm, k), lambda i: (i, 0)), pl.BlockSpec((1, k), lambda i: (0, 0))]
    in_specs += [pl.BlockSpec(w.shape, lambda i: (0, 0)) for w in weights]
    out_specs = [pl.BlockSpec((tm, w.shape[1]), lambda i: (i, 0)) for w in weights]
    out_shape = [jax.ShapeDtypeStruct((m, w.shape[1]), dt) for w, dt in zip(weights, out_dtypes)]
    return pl.pallas_call(
        functools.partial(_norm_linear_kernel, n_out=n_out),
        grid=(m // tm,), in_specs=in_specs, out_specs=out_specs, out_shape=out_shape,
        compiler_params=_cparams(("parallel",)), name="norm_linear",
    )(x, nw.reshape(1, k), *weights)


def _linear_res_kernel(res_ref, *refs, n_in):
    a_refs, w_refs, o_ref = refs[:n_in], refs[n_in:2 * n_in], refs[2 * n_in]
    acc = res_ref[...]
    for a_ref, w_ref in zip(a_refs, w_refs):
        acc = acc + _dot(a_ref[...].astype(BF16), w_ref[...])
    o_ref[...] = acc


def linear_res(res, acts, weights, tm=512):
    m, n = res.shape
    n_in = len(acts)
    in_specs = [pl.BlockSpec((tm, n), lambda i: (i, 0))]
    in_specs += [pl.BlockSpec((tm, a.shape[1]), lambda i: (i, 0)) for a in acts]
    in_specs += [pl.BlockSpec(w.shape, lambda i: (0, 0)) for w in weights]
    return pl.pallas_call(
        functools.partial(_linear_res_kernel, n_in=n_in),
        grid=(m // tm,), in_specs=in_specs, out_specs=pl.BlockSpec((tm, n), lambda i: (i, 0)),
        out_shape=jax.ShapeDtypeStruct((m, n), F32),
        compiler_params=_cparams(("parallel",)), name="linear_res",
    )(res, *acts, *weights)


def _ffn_kernel(x_ref, nw_ref, wg_ref, wu_ref, wd_ref, fw_ref, o_ref, xn_scr, acc_scr, *, final_norm):
    f = pl.program_id(1)

    @pl.when(f == 0)
    def _():
        x = x_ref[...]
        h = x * lax.rsqrt(jnp.mean(x * x, axis=-1, keepdims=True) + RMS_EPS) * nw_ref[...]
        xn_scr[...] = h.astype(BF16)
        acc_scr[...] = jnp.zeros_like(acc_scr)

    xn = xn_scr[...]
    g = _dot(xn, wg_ref[...])
    u = _dot(xn, wu_ref[...])
    hid = (_silu(g) * u).astype(BF16)
    acc_scr[...] += _dot(hid, wd_ref[...])

    @pl.when(f == pl.num_programs(1) - 1)
    def _():
        y = x_ref[...] + acc_scr[...]
        if final_norm:
            y = y * lax.rsqrt(jnp.mean(y * y, axis=-1, keepdims=True) + RMS_EPS) * fw_ref[...]
        o_ref[...] = y


def ffn(x, nw, wg, wu, wd, fw, final_norm, tm=512, tf=1408):
    m, d = x.shape
    dff = wg.shape[1]
    return pl.pallas_call(
        functools.partial(_ffn_kernel, final_norm=final_norm),
        grid=(m // tm, dff // tf),
        in_specs=[pl.BlockSpec((tm, d), lambda i, f: (i, 0)),
                  pl.BlockSpec((1, d), lambda i, f: (0, 0)),
                  pl.BlockSpec((d, tf), lambda i, f: (0, f)),
                  pl.BlockSpec((d, tf), lambda i, f: (0, f)),
                  pl.BlockSpec((tf, d), lambda i, f: (f, 0)),
                  pl.BlockSpec((1, d), lambda i, f: (0, 0))],
        out_specs=pl.BlockSpec((tm, d), lambda i, f: (i, 0)),
        out_shape=jax.ShapeDtypeStruct((m, d), F32),
        scratch_shapes=[pltpu.VMEM((tm, d), BF16), pltpu.VMEM((tm, d), F32)],
        compiler_params=_cparams(("parallel", "arbitrary")), name="ffn",
    )(x, nw.reshape(1, d), wg, wu, wd, fw.reshape(1, d))


def _gdn_kernel(q_ref, k_ref, v_ref, qh_ref, kh_ref, vh_ref, cwq_ref, cwk_ref, cwv_ref,
                z_ref, ba_ref, bat_ref, hp_ref, nw_ref, o_ref, scr, state, *, tc, inv_passes):
    h = pl.program_id(1)
    t = pl.program_id(2)

    @pl.when(t == 0)
    def _():
        state[...] = jnp.zeros_like(state)

    not_first = (t > 0).astype(F32)

    def conv(x_ref, halo_ref, w_ref, slot):
        scr[slot, 0:8, :] = halo_ref[0] * not_first
        scr[slot, 8:8 + tc, :] = x_ref[0]
        w = w_ref[...]
        y = (w[3:4] * scr[slot, 8:8 + tc, :] + w[2:3] * scr[slot, 7:7 + tc, :]
             + w[1:2] * scr[slot, 6:6 + tc, :] + w[0:1] * scr[slot, 5:5 + tc, :])
        return _silu(y)

    q = conv(q_ref, qh_ref, cwq_ref, 0)
    k = conv(k_ref, kh_ref, cwk_ref, 1)
    v = conv(v_ref, vh_ref, cwv_ref, 2)
    qn = q * lax.rsqrt(jnp.sum(q * q, axis=-1, keepdims=True) + 1e-6)
    kn = k * lax.rsqrt(jnp.sum(k * k, axis=-1, keepdims=True) + 1e-6)

    lane = lax.broadcasted_iota(jnp.int32, (1, LANES), 1)
    ba = ba_ref[0]
    b_col = jnp.sum(jnp.where(lane == h, ba, 0.0), axis=-1, keepdims=True)
    a_col = jnp.sum(jnp.where(lane == h + GDN_HEADS, ba, 0.0), axis=-1, keepdims=True)
    beta = jax.nn.sigmoid(b_col)
    a_log = hp_ref[0, 0:1, :]
    dt_b = hp_ref[0, 1:2, :]
    g_rep = -jnp.exp(a_log) * _softplus(a_col + dt_b)

    m0, m1 = _lane_masks()
    i64 = lax.broadcasted_iota(jnp.int32, (CHUNK, LANES), 0)
    j64 = lax.broadcasted_iota(jnp.int32, (CHUNK, LANES), 1) & (CHUNK - 1)
    low_half = lax.broadcasted_iota(jnp.int32, (CHUNK, LANES), 1) < CHUNK
    ltri_bd = _tri_ones(2 * CHUNK, True, CHUNK)
    utri_bd = _tri_ones(2 * CHUNK, False, CHUNK)
    eye128 = (lax.broadcasted_iota(jnp.int32, (LANES, LANES), 0)
              == lax.broadcasted_iota(jnp.int32, (LANES, LANES), 1)).astype(F32)
    zeros64 = jnp.zeros((CHUNK, LANES), F32)
    scale = 1.0 / math.sqrt(LANES)
    nw = nw_ref[...]

    npair = tc // (2 * CHUNK)
    each = lambda f, *ls: [f(*xs) for xs in zip(*ls)]
    halves = lambda x: jnp.where(low_half, x[0:CHUNK], x[CHUNK:2 * CHUNK])
    prow = [slice(2 * CHUNK * p, 2 * CHUNK * (p + 1)) for p in range(npair)]
    kp = [kn[rs] for rs in prow]
    qp = [qn[rs] * scale for rs in prow]
    vp = [v[rs] for rs in prow]
    beta_p = [beta[rs] for rs in prow]
    gc_col = [_mm_exact_lhs(ltri_bd, g_rep[rs]) for rs in prow]
    g_row = [-jnp.exp(a_log[:, 0:1]) * _softplus(bat_ref[0, 0, p, 1:2, :] + dt_b[:, 0:1]) for p in range(npair)]
    gc_row = each(lambda x: _mm_exact_rhs(jnp.broadcast_to(x, (CHUNK, LANES)), utri_bd), g_row)
    dcat = each(lambda c_, r_: jnp.exp(jnp.where(i64 >= j64, halves(c_) - r_, -jnp.inf)), gc_col, gc_row)
    kb = each(lambda k_, b_: k_ * b_, kp, beta_p)
    gram = each(lambda x, y_: _mm(x, y_, nt=True), kb, kp)
    gq = each(lambda x, y_: _mm(x, y_, nt=True), qp, kp)
    strict = each(lambda x, d_: jnp.where(i64 > j64, halves(x) * d_, 0.0), gram, dcat)
    qk_bd = each(lambda x, d_: _stack2(jnp.where(i64 >= j64, halves(x) * d_, 0.0), m0, m1), gq, dcat)
    eg = each(jnp.exp, gc_col)
    g_last = each(lambda x: [x[CHUNK * (c + 1) - 1:CHUNK * (c + 1), :] for c in range(2)], gc_col)
    kdt = each(lambda k_, gl_, gc_: jnp.concatenate(
        [k_[CHUNK * c:CHUNK * (c + 1)] * jnp.exp(gl_[c] - gc_[CHUNK * c:CHUNK * (c + 1)]) for c in range(2)],
        axis=0).T, kp, g_last, gc_col)
    tinv = _tri_inv_cat(strict, inv_passes)
    uw = each(lambda t_, v_, b_, kb_, e_: _mm(_stack2(t_, m0, m1), jnp.concatenate([v_ * b_, kb_ * e_], axis=1)),
              tinv, vp, beta_p, kb, eg)
    u = each(lambda x: x[:, 0:LANES], uw)
    w = each(lambda x: x[:, LANES:2 * LANES], uw)
    q_hat = each(lambda q_, e_, m_, w_: q_ * e_ - _mm(m_, w_), qp, eg, qk_bd, w)
    o_hat = each(_mm, qk_bd, u)
    pad = lambda x, c: jnp.concatenate([x[0:CHUNK], zeros64] if c == 0 else [zeros64, x[CHUNK:2 * CHUNK]], axis=0)
    trans = [eye128 * jnp.exp(g_last[p][c]) - _mm(kdt[p], pad(w[p], c)) for p in range(npair) for c in range(2)]
    h_add = [_mm(kdt[p], pad(u[p], c)) for p in range(npair) for c in range(2)]
    s_state = state[...]
    for p in range(npair):
        for c in range(2):
            cr = slice(CHUNK * c, CHUNK * (c + 1))
            o = _mm(q_hat[p][cr], s_state) + o_hat[p][cr]
            s_state = _mm(trans[2 * p + c], s_state, passes=3) + h_add[2 * p + c]
            on = o * lax.rsqrt(jnp.mean(o * o, axis=-1, keepdims=True) + RMS_EPS) * nw
            r0 = 2 * CHUNK * p + CHUNK * c
            o_ref[0, r0:r0 + CHUNK, :] = (on * _silu(z_ref[0, r0:r0 + CHUNK, :])).astype(o_ref.dtype)
    state[...] = s_state


def gated_deltanet(qkv, z, ba, conv_w, a_log, dt_bias, norm_w, tc=512, inv_passes=1):
    b, s, _ = qkv.shape
    hh = GDN_HEADS
    bat = ba[..., :2 * hh].reshape(b, s // 128, 128, 2, hh).transpose(0, 4, 1, 3, 2)
    hp = jnp.zeros((hh, 8, LANES), F32)
    hp = hp.at[:, 0, :].set(a_log.astype(F32)[:, None]).at[:, 1, :].set(dt_bias.astype(F32)[:, None])
    cw = jnp.zeros((8, 3 * hh * LANES), F32).at[:GDN_CONV].set(conv_w.astype(F32))
    hb = tc // 8
    blk = lambda off: pl.BlockSpec((1, tc, LANES), lambda bi, hi, ti: (bi, ti, hi + off))
    halo = lambda off: pl.BlockSpec((1, 8, LANES), lambda bi, hi, ti: (bi, jnp.maximum(ti * hb - 1, 0), hi + off))
    cws = lambda off: pl.BlockSpec((8, LANES), lambda bi, hi, ti: (0, hi + off))
    return pl.pallas_call(
        functools.partial(_gdn_kernel, tc=tc, inv_passes=inv_passes),
        grid=(b, hh, s // tc),
        in_specs=[blk(0), blk(hh), blk(2 * hh), halo(0), halo(hh), halo(2 * hh), cws(0), cws(hh), cws(2 * hh),
                  blk(0),
                  pl.BlockSpec((1, tc, LANES), lambda bi, hi, ti: (bi, ti, 0)),
                  pl.BlockSpec((1, 1, tc // 128, 2, LANES), lambda bi, hi, ti: (bi, hi, ti, 0, 0)),
                  pl.BlockSpec((1, 8, LANES), lambda bi, hi, ti: (hi, 0, 0)),
                  pl.BlockSpec((1, LANES), lambda bi, hi, ti: (0, 0))],
        out_specs=blk(0),
        out_shape=jax.ShapeDtypeStruct((b, s, hh * LANES), BF16),
        scratch_shapes=[pltpu.VMEM((3, tc + 8, LANES), F32), pltpu.VMEM((LANES, LANES), F32)],
        compiler_params=_cparams(("parallel", "parallel", "arbitrary")), name="gated_deltanet",
    )(qkv, qkv, qkv, qkv, qkv, qkv, cw, cw, cw, z, ba, bat, hp, norm_w.astype(F32).reshape(1, LANES))


def _compress_kernel(x_ref, pe_ref, w1_ref, w2_ref, o_ref):
    nb = x_ref.shape[3]
    acc = jnp.zeros((nb, LANES), F32)
    for g in range(NSA_GROUPS):
        x = x_ref[0, 0, g]
        first = _dot((x + pe_ref[0, 0]).astype(BF16), w1_ref[0, 0])
        second = _dot((x + pe_ref[0, 1]).astype(BF16), w1_ref[0, 1])
        hid = _silu(first + pltpu.roll(second, nb - 1, 0))
        acc = acc + _dot(hid.astype(BF16), w2_ref[0, g])
    o_ref[0, 0] = acc.astype(o_ref.dtype)


def compress(kvc, pe_k, w1_k, w2_k, pe_v, w1_v, w2_v):
    b, s, _ = kvc.shape
    nb = s // CMP_STRIDE
    width = CMP_STRIDE * NSA_DH
    x = kvc.reshape(b, nb, CMP_STRIDE, 2, NSA_GROUPS, NSA_DH).transpose(0, 3, 4, 1, 2, 5).reshape(b, 2, NSA_GROUPS, nb, width)
    pe = jnp.stack([pe_k, pe_v]).astype(F32).reshape(2, 2, 1, width)
    w1 = jnp.stack([w1_k, w1_v]).astype(BF16).reshape(2, 2, width, -1)
    hid = w1.shape[-1]
    w2 = jnp.stack([w2_k, w2_v]).astype(BF16)
    w2p = jnp.zeros((2, NSA_GROUPS, hid, LANES), BF16)
    for g in range(NSA_GROUPS):
        w2p = w2p.at[:, g, :, NSA_DH * g:NSA_DH * (g + 1)].set(w2)
    return pl.pallas_call(
        _compress_kernel,
        grid=(b, 2),
        in_specs=[pl.BlockSpec((1, 1, NSA_GROUPS, nb, width), lambda bi, j: (bi, j, 0, 0, 0)),
                  pl.BlockSpec((1, 2, 1, width), lambda bi, j: (j, 0, 0, 0)),
                  pl.BlockSpec((1, 2, width, hid), lambda bi, j: (j, 0, 0, 0)),
                  pl.BlockSpec((1, NSA_GROUPS, hid, LANES), lambda bi, j: (j, 0, 0, 0))],
        out_specs=pl.BlockSpec((1, 1, nb, LANES), lambda bi, j: (bi, j, 0, 0)),
        out_shape=jax.ShapeDtypeStruct((b, 2, nb, LANES), BF16),
        compiler_params=_cparams(("parallel", "parallel")), name="nsa_compress",
    )(x, pe, w1, w2p)


def _softmax_pv(s, v):
    m = jnp.max(s, axis=-1, keepdims=True)
    p = jnp.exp(s - m)
    l = jnp.sum(p, axis=-1, keepdims=True)
    return _dot(p.astype(BF16), v) / l


def _nsa_kernel(q_ref, gl_ref, kcvc_ref, kv_ref, bw_ref, bfar_ref, bc_ref, ovl_ref, o_ref):
    n = pl.program_id(1)
    qb_rows = q_ref.shape[1]
    ns = (kv_ref.shape[1] - KV_PAD) // SLC_LEN
    far_w = 8 * SLC_LEN
    lane = lax.broadcasted_iota(jnp.int32, (qb_rows, LANES), 1)
    lane_s = lane & (ns - 1)
    gates = jax.nn.sigmoid(gl_ref[0])
    kc = kcvc_ref[0, 0]
    vc = kcvc_ref[0, 1]
    ovl = ovl_ref[...]
    row_blk = lax.broadcasted_iota(jnp.int32, (LANES, far_w), 0)
    col_blk = lax.broadcasted_iota(jnp.int32, (LANES, far_w), 1) >> 6
    wcol = lax.broadcasted_iota(jnp.int32, (NSA_HPG * qb_rows, WIN_TILE), 1)
    groups = range(NSA_GROUPS)
    tile4 = lambda x: jnp.concatenate([x] * NSA_HPG, axis=0)

    qs = [jnp.concatenate([q_ref[0, :, LANES * (NSA_HPG * g + hh):LANES * (NSA_HPG * g + hh + 1)]
                           for hh in range(NSA_HPG)], axis=0) for g in groups]
    c0 = (qb_rows // CMP_STRIDE) * n
    lane4 = lax.broadcasted_iota(jnp.int32, (NSA_HPG * qb_rows, LANES), 1)
    shift = (c0 + (LANES - CMP_TILE_ORIGIN)) % LANES
    bc = [jnp.where(lane4 < c0 - (CMP_PAST_BLOCKS - 1), bfar_ref[g],
                    jnp.where(lane4 >= c0 + CMP_FUTURE_BLOCK, NEG, pltpu.roll(bc_ref[g], shift, 1))) for g in groups]
    start = pl.multiple_of(n * SLC_LEN + (KV_PAD - 3 * SLC_LEN), SLC_LEN)
    k_near = kv_ref[0, pl.ds(start, 4 * SLC_LEN), 0:LANES]
    v_near = kv_ref[0, pl.ds(start, 4 * SLC_LEN), LANES:2 * LANES]
    wst = pl.multiple_of(n * SLC_LEN + (KV_PAD + SLC_LEN - WIN_TILE), SLC_LEN)
    k_w = kv_ref[0, pl.ds(wst, WIN_TILE), 2 * LANES:3 * LANES]
    v_w = kv_ref[0, pl.ds(wst, WIN_TILE), 3 * LANES:4 * LANES]
    s_cmp = [_dot_nt(qs[g], kc) + bc[g] for g in groups]
    s_near = [_dot_nt(qs[g], k_near) + bw_ref[g, :, WIN_TILE - 4 * SLC_LEN:WIN_TILE] for g in groups]
    s_win = [jnp.where(wcol >= (WIN_TILE - SLC_LEN) - SLC_LEN * n, _dot_nt(qs[g], k_w) + bw_ref[g], NEG)
             for g in groups]

    p_cmp = []
    for g in groups:
        m = jnp.max(s_cmp[g], axis=-1, keepdims=True)
        e = jnp.where(bc[g] > 0.5 * NEG, jnp.exp(s_cmp[g] - m), 0.0)
        p_cmp.append(e / jnp.maximum(jnp.sum(e, axis=-1, keepdims=True), 1e-30))
    o_cmp = [_dot(p.astype(BF16), vc) for p in p_cmp]
    imp = [_mm_exact_rhs(p[0:qb_rows] + p[qb_rows:2 * qb_rows] + p[2 * qb_rows:3 * qb_rows] + p[3 * qb_rows:4 * qb_rows],
                         ovl) for p in p_cmp]

    forced = jnp.logical_or(jnp.logical_or(lane_s == 0, lane_s == n), lane_s == n - 1)
    val = [jnp.where(lane_s > n, -1.0, jnp.where(forced, 1e6, x)) for x in imp]
    rank = [jnp.zeros((qb_rows, LANES), F32) for _ in groups]
    for r in range(1, ns):
        tie = (lane_s >= r).astype(F32)
        for g in groups:
            other = pltpu.roll(val[g], r, 1)
            rank[g] = rank[g] + jnp.where(other > val[g], 1.0, jnp.where(other == val[g], tie, 0.0))
    sel = [jnp.where(jnp.logical_and(x < N_SELECT, lane < ns), 1.0, 0.0) for x in rank]

    o_win = [_softmax_pv(s_win[g], v_w) for g in groups]

    e_near = (lax.broadcasted_iota(jnp.int32, (LANES, 4 * SLC_LEN), 0)
              == (lax.broadcasted_iota(jnp.int32, (LANES, 4 * SLC_LEN), 1) >> 6) + (n - 3)).astype(BF16)
    carry = []
    for g in groups:
        sm = _dot(sel[g].astype(BF16), e_near)
        s = jnp.where(tile4(sm) > 0.5, s_near[g], NEG)
        m = jnp.max(s, axis=-1, keepdims=True)
        p = jnp.exp(s - m)
        carry += [m, jnp.sum(p, axis=-1, keepdims=True), _dot(p.astype(BF16), v_near)]
    sel_far = [jnp.where(lane <= n - 4, x, 0.0).astype(BF16) for x in sel]
    b_far = [jnp.concatenate([bfar_ref[g]] * (far_w // LANES), axis=1) for g in groups]

    def far_body(it, carry):
        st = pl.multiple_of(KV_PAD + it * far_w, math.gcd(KV_PAD, far_w))
        k_c = kv_ref[0, pl.ds(st, far_w), 0:LANES]
        v_c = kv_ref[0, pl.ds(st, far_w), LANES:2 * LANES]
        e_c = (row_blk == col_blk + 8 * it).astype(BF16)
        sc = [_dot_nt(qs[g], k_c) + b_far[g] for g in groups]
        smc = [_dot(sel_far[g], e_c) for g in groups]
        out = []
        for g in groups:
            m_i, l_i, acc_i = carry[3 * g:3 * g + 3]
            s_g = jnp.where(tile4(smc[g]) > 0.5, sc[g], NEG)
            m_n = jnp.maximum(m_i, jnp.max(s_g, axis=-1, keepdims=True))
            alpha = jnp.exp(m_i - m_n)
            pc = jnp.exp(s_g - m_n)
            out += [m_n, alpha * l_i + jnp.sum(pc, axis=-1, keepdims=True),
                    alpha * acc_i + _dot(pc.astype(BF16), v_c)]
        return tuple(out)

    carry = lax.fori_loop(0, (n // 4 + 1) // 2, far_body, tuple(carry))
    o_slc = [carry[3 * g + 2] / carry[3 * g + 1] for g in groups]

    for g in groups:
        pieces = []
        for hh in range(NSA_HPG):
            hd = NSA_HPG * g + hh
            rs = slice(qb_rows * hh, qb_rows * (hh + 1))
            pieces.append(gates[:, 3 * hd:3 * hd + 1] * o_cmp[g][rs] + gates[:, 3 * hd + 1:3 * hd + 2] * o_slc[g][rs]
                          + gates[:, 3 * hd + 2:3 * hd + 3] * o_win[g][rs])
        for hp in range(2):
            lo, hi = pieces[2 * hp], pieces[2 * hp + 1]
            if g == 0:
                hi = pltpu.roll(hi, NSA_DH, 1)
            else:
                lo = pltpu.roll(lo, NSA_DH, 1)
            o_ref[0, :, LANES * (2 * g + hp):LANES * (2 * g + hp + 1)] = jnp.where(lane < NSA_DH, lo, hi).astype(o_ref.dtype)


def _rel_bucket(dist):
    n = jnp.maximum(dist, 0)
    exact = NUM_BUCKETS // 2
    nf = jnp.maximum(n, 1).astype(F32)
    large = exact + (jnp.log(nf / exact) / math.log(REL_MAX_DIST / exact) * (NUM_BUCKETS - exact)).astype(jnp.int32)
    return jnp.where(n < exact, n, jnp.minimum(large, NUM_BUCKETS - 1))


def nsa_attention(qpad, gl, kcvc, kv4, rel_bias):
    b, s, _ = qpad.shape
    qb = SLC_LEN
    nq = s // qb
    nc = s // CMP_STRIDE - CMP_LEN // CMP_STRIDE + 1
    ns = s // SLC_LEN
    table = rel_bias.astype(F32)
    dist_w = jnp.arange(qb)[:, None] + (WIN_TILE - SLC_LEN) - jnp.arange(WIN_TILE)[None, :]
    ok_w = jnp.logical_and(dist_w >= 0, dist_w < WINDOW)
    bw = jnp.where(ok_w[None], table[_rel_bucket(dist_w)].transpose(2, 0, 1), NEG)
    bw = bw.reshape(NSA_GROUPS, NSA_HPG * qb, WIN_TILE)
    far_val = table[NUM_BUCKETS - 1]
    bfar = jnp.broadcast_to(far_val[:, None, None], (NSA_HEADS, qb, LANES)).reshape(NSA_GROUPS, NSA_HPG * qb, LANES)
    cmp_end = jnp.arange(LANES) * CMP_STRIDE + CMP_LEN - 1
    dist_c = jnp.arange(qb)[:, None] - CMP_STRIDE * (jnp.arange(LANES)[None, :] - CMP_TILE_ORIGIN) - (CMP_LEN - 1)
    bc = jnp.where((dist_c >= 0)[None], table[_rel_bucket(dist_c)].transpose(2, 0, 1), NEG)
    bc = bc.reshape(NSA_GROUPS, NSA_HPG * qb, LANES)
    cmp_start = jnp.arange(LANES) * CMP_STRIDE
    slc_j = jnp.arange(LANES) % ns
    ovl = jnp.logical_and(jnp.logical_and(cmp_start[:, None] < (slc_j[None, :] + 1) * SLC_LEN,
                                          cmp_end[:, None] >= slc_j[None, :] * SLC_LEN),
                          jnp.arange(LANES)[:, None] < nc).astype(F32)
    kvp = jnp.pad(kv4, ((0, 0), (KV_PAD, 0), (0, 0)))
    sp = s + KV_PAD
    return pl.pallas_call(
        _nsa_kernel,
        grid=(b, nq),
        in_specs=[pl.BlockSpec((1, qb, NSA_HEADS * LANES), lambda bi, n: (bi, n, 0)),
                  pl.BlockSpec((1, qb, LANES), lambda bi, n: (bi, n, 0)),
                  pl.BlockSpec((1, 2, s // CMP_STRIDE, LANES), lambda bi, n: (bi, 0, 0, 0)),
                  pl.BlockSpec((1, sp, 4 * LANES), lambda bi, n: (bi, 0, 0)),
                  pl.BlockSpec((NSA_GROUPS, NSA_HPG * qb, WIN_TILE), lambda bi, n: (0, 0, 0)),
                  pl.BlockSpec((NSA_GROUPS, NSA_HPG * qb, LANES), lambda bi, n: (0, 0, 0)),
                  pl.BlockSpec((NSA_GROUPS, NSA_HPG * qb, LANES), lambda bi, n: (0, 0, 0)),
                  pl.BlockSpec((LANES, LANES), lambda bi, n: (0, 0))],
        out_specs=pl.BlockSpec((1, qb, NSA_HEADS * NSA_DH), lambda bi, n: (bi, n, 0)),
        out_shape=jax.ShapeDtypeStruct((b, s, NSA_HEADS * NSA_DH), BF16),
        compiler_params=_cparams(("parallel", "arbitrary")), name="nsa_attention",
    )(qpad, gl, kcvc, kvp, bw, bfar, bc, ovl)


def _rwkv_proj_kernel(x_ref, halo_ref, nw_ref, mu_ref, vec_ref, wr_ref, wk_ref, wv_ref,
                      w1_ref, w2_ref, a1_ref, a2_ref, g1_ref, g2_ref,
                      r_ref, k_ref, v_ref, lw_ref, kk_ref, a_ref, g_ref, scr, *, tm):
    t = pl.program_id(1)
    nw = nw_ref[...]

    def norm(x):
        return x * lax.rsqrt(jnp.mean(x * x, axis=-1, keepdims=True) + RMS_EPS) * nw

    h = norm(x_ref[0])
    scr[0:8, :] = norm(halo_ref[0]) * (t > 0).astype(F32)
    scr[8:8 + tm, :] = h
    xx = scr[7:7 + tm, :] - h
    mix = lambda j: (h + xx * mu_ref[j:j + 1, :]).astype(BF16)
    w0, a0, k_k, k_a = vec_ref[0:1, :], vec_ref[1:2, :], vec_ref[2:3, :], vec_ref[3:4, :]
    r = _dot(mix(0), wr_ref[...])
    wl = w0 + _dot(jnp.tanh(_dot(mix(1), w1_ref[...])).astype(BF16), w2_ref[...])
    k = _dot(mix(2), wk_ref[...])
    v = _dot(mix(3), wv_ref[...])
    a = jax.nn.sigmoid(a0 + _dot(_dot(mix(4), a1_ref[...]).astype(BF16), a2_ref[...]))
    g = _dot(jax.nn.sigmoid(_dot(mix(5), g1_ref[...])).astype(BF16), g2_ref[...])
    w_raw = -_softplus(-wl) - 0.5
    r_ref[0] = r.astype(r_ref.dtype)
    k_ref[0] = (k * (1.0 + (a - 1.0) * k_a)).astype(k_ref.dtype)
    v_ref[0] = v.astype(v_ref.dtype)
    lw_ref[0] = -jnp.exp(w_raw)
    kk_ref[0] = (k * k_k).astype(kk_ref.dtype)
    a_ref[0] = a.astype(a_ref.dtype)
    g_ref[0] = g.astype(g_ref.dtype)


def rwkv_proj(x, nw, mu, w_r, w_k, w_v, w0, w1, w2, a0, a1, a2, g1, g2, k_k, k_a, tm=256):
    b, s, d = x.shape
    pad_to = lambda w, rows, cols: jnp.zeros((rows, cols), BF16).at[:w.shape[0], :w.shape[1]].set(w.astype(BF16))
    lora = lambda n: -(-n // LANES) * LANES
    mu8 = jnp.zeros((8, d), F32).at[:6].set(mu.astype(F32))
    vec = jnp.zeros((8, d), F32).at[0].set(w0).at[1].set(a0).at[2].set(k_k).at[3].set(k_a)
    big = [w.astype(BF16) for w in (w_r, w_k, w_v)]
    small = [pad_to(w1, d, lora(w1.shape[1])), pad_to(w2, lora(w2.shape[0]), d),
             pad_to(a1, d, lora(a1.shape[1])), pad_to(a2, lora(a2.shape[0]), d),
             pad_to(g1, d, lora(g1.shape[1])), pad_to(g2, lora(g2.shape[0]), d)]
    whole = lambda w: pl.BlockSpec(w.shape, lambda bi, ti: (0, 0))
    hb = tm // 8
    tile = pl.BlockSpec((1, tm, d), lambda bi, ti: (bi, ti, 0))
    return pl.pallas_call(
        functools.partial(_rwkv_proj_kernel, tm=tm),
        grid=(b, s // tm),
        in_specs=[tile, pl.BlockSpec((1, 8, d), lambda bi, ti: (bi, jnp.maximum(ti * hb - 1, 0), 0)),
                  pl.BlockSpec((1, d), lambda bi, ti: (0, 0)), whole(mu8), whole(vec)]
                 + [whole(w) for w in big] + [whole(w) for w in small],
        out_specs=[tile] * 7,
        out_shape=[jax.ShapeDtypeStruct((b, s, d), dt) for dt in (BF16, BF16, BF16, F32, BF16, BF16, BF16)],
        scratch_shapes=[pltpu.VMEM((tm + 8, d), F32)],
        compiler_params=_cparams(("parallel", "arbitrary")), name="rwkv_proj",
    )(x, x, nw.reshape(1, d), mu8, vec, *big, *small)


def _wkv_kernel(r_ref, k_ref, v_ref, lw_ref, kk_ref, a_ref, g_ref, lnw_ref, lnb_ref, rk_ref,
                o_ref, state, *, tc, inv_passes):
    t = pl.program_id(2)

    @pl.when(t == 0)
    def _():
        state[...] = jnp.zeros_like(state)

    m0, m1 = _lane_masks()
    lane = lax.broadcasted_iota(jnp.int32, (1, LANES), 1)
    low = lane < 64

    def seg_sum(x):
        return jnp.where(low, jnp.sum(x * m0, axis=-1, keepdims=True), jnp.sum(x * m1, axis=-1, keepdims=True))

    npair = r_ref.shape[2] // LANES
    pairs = range(npair)
    tile_of = lambda ref, pi: ref[0, :, LANES * pi:LANES * (pi + 1)].astype(F32)
    r_all, k_all, v_all = ([tile_of(ref, pi) for pi in pairs] for ref in (r_ref, k_ref, v_ref))
    kk = [tile_of(kk_ref, pi) for pi in pairs]
    kk = [x * lax.rsqrt(seg_sum(x * x) + 1e-12) for x in kk]
    a_all = [-x for x in kk]
    b_all = [x * tile_of(a_ref, pi) for pi, x in zip(pairs, kk)]

    i64 = lax.broadcasted_iota(jnp.int32, (CHUNK, LANES), 0)
    j64 = lax.broadcasted_iota(jnp.int32, (CHUNK, LANES), 1) & (CHUNK - 1)
    strict_m = i64 > j64
    incl_m = i64 >= j64
    ltri = _tri_ones(CHUNK, True)
    eye128 = (lax.broadcasted_iota(jnp.int32, (LANES, LANES), 0)
              == lax.broadcasted_iota(jnp.int32, (LANES, LANES), 1)).astype(F32)
    bd = ((lax.broadcasted_iota(jnp.int32, (LANES, LANES), 0) >> 6)
          == (lax.broadcasted_iota(jnp.int32, (LANES, LANES), 1) >> 6))
    zeros64 = jnp.zeros((CHUNK, LANES), F32)
    st = lambda x: _stack2(x, m0, m1)

    nch = tc // CHUNK
    each = lambda f, *ls: [f(*xs) for xs in zip(*ls)]
    items = [(pi, slice(CHUNK * c, CHUNK * (c + 1))) for c in range(nch) for pi in pairs]
    lw = [lw_ref[0, rs, LANES * pi:LANES * (pi + 1)] for pi, rs in items]
    r, k, v = ([x[pi][rs] for pi, rs in items] for x in (r_all, k_all, v_all))
    av, bv = ([x[pi][rs] for pi, rs in items] for x in (a_all, b_all))
    cs = each(lambda x: _mm_exact_lhs(ltri, x), lw)
    cl = each(lambda x: x[CHUNK - 1:CHUNK, :], cs)
    a_t = each(lambda a_, c_, l_: a_ * jnp.exp(c_ - l_), av, cs, lw)
    r_t = each(lambda r_, c_: r_ * jnp.exp(c_), r, cs)
    inv_w = each(lambda c_: jnp.exp(-c_), cs)
    b_t = each(lambda x, w_: x * w_, bv, inv_w)
    k_t = each(lambda x, w_: x * w_, k, inv_w)
    to_end = each(lambda e_, c_: jnp.exp(e_ - c_), cl, cs)
    bkt = each(lambda b_, k_, e_: jnp.concatenate([b_ * e_, k_ * e_], axis=0).T, bv, k, to_end)
    ar = each(lambda a_, r_: jnp.concatenate([a_, r_], axis=0), a_t, r_t)
    gb = each(lambda x, y_: _mm(x, st(y_), nt=True), ar, b_t)
    gk = each(lambda x, y_: _mm(x, st(y_), nt=True), ar, k_t)
    l_ab = each(lambda x: jnp.where(strict_m, -x[0:CHUNK], 0.0), gb)
    l_ak = each(lambda x: jnp.where(strict_m, x[0:CHUNK], 0.0), gk)
    m_rb = each(lambda x: jnp.where(incl_m, x[CHUNK:2 * CHUNK], 0.0), gb)
    m_rk = each(lambda x: jnp.where(incl_m, x[CHUNK:2 * CHUNK], 0.0), gk)
    vst = each(st, v)
    lakv = each(_mm, l_ak, vst)
    tinv = _tri_inv_cat(l_ab, inv_passes)
    p_mat = each(lambda t_, a_: _mm(t_, st(a_)), tinv, a_t)
    q_mat = each(lambda t_, x: _mm(t_, st(x)), tinv, lakv)
    r_hat = each(lambda r_, m_, p_: r_ + _mm(m_, st(p_)), r_t, m_rb, p_mat)
    y_hat = each(lambda mb, q_, mk, vs: _mm(mb, st(q_)) + _mm(mk, vs), m_rb, q_mat, m_rk, vst)
    trans = each(lambda e_, bk, p_: eye128 * jnp.exp(e_)
                 + jnp.where(bd, _mm(bk, jnp.concatenate([p_, zeros64], axis=0)), 0.0), cl, bkt, p_mat)
    h_add = each(lambda bk, q_, v_: jnp.where(bd, _mm(bk, jnp.concatenate([q_, v_], axis=0)), 0.0), bkt, q_mat, v)
    hs = [state[pi] for pi in pairs]
    ys = [[] for _ in pairs]
    for idx, (pi, _) in enumerate(items):
        ys[pi].append(_mm(r_hat[idx], hs[pi]) + y_hat[idx])
        hs[pi] = _mm(trans[idx], hs[pi], passes=3) + h_add[idx]
    for pi in pairs:
        state[pi] = hs[pi]
        lanes = slice(LANES * pi, LANES * (pi + 1))
        y = jnp.concatenate(ys[pi], axis=0)
        mean = seg_sum(y) * (1.0 / 64)
        dlt = y - mean
        var = seg_sum(dlt * dlt) * (1.0 / 64)
        yn = dlt * lax.rsqrt(var + RWKV_GN_EPS) * lnw_ref[:, lanes] + lnb_ref[:, lanes]
        yn = yn + seg_sum(r_all[pi] * k_all[pi] * rk_ref[:, lanes]) * v_all[pi]
        o_ref[0, :, lanes] = (yn * tile_of(g_ref, pi)).astype(o_ref.dtype)


def wkv7(r, k, v, lw, kk, a, g, lnx_w, lnx_b, r_k, tc=512, npair=2, inv_passes=1):
    b, s, d = r.shape
    width = npair * LANES
    tile = pl.BlockSpec((1, tc, width), lambda bi, pi, ti: (bi, ti, pi))
    vec = pl.BlockSpec((1, width), lambda bi, pi, ti: (0, pi))
    return pl.pallas_call(
        functools.partial(_wkv_kernel, tc=tc, inv_passes=inv_passes),
        grid=(b, d // width, s // tc),
        in_specs=[tile] * 7 + [vec] * 3,
        out_specs=tile,
        out_shape=jax.ShapeDtypeStruct((b, s, d), BF16),
        scratch_shapes=[pltpu.VMEM((npair, LANES, LANES), F32)],
        compiler_params=_cparams(("parallel", "parallel", "arbitrary")), name="wkv7",
    )(r, k, v, lw, kk, a, g, lnx_w.reshape(1, d), lnx_b.reshape(1, d), r_k.reshape(1, d))


def _in_proj_weights(w_in):
    gw = GDN_HEADS * LANES
    c = 4 * gw
    w_qkv, w_z = w_in[:, 0:3 * gw], w_in[:, 3 * gw:c]
    d = w_in.shape[0]
    w_ba = jnp.zeros((d, LANES), w_in.dtype).at[:, :2 * GDN_HEADS].set(w_in[:, c:c + 2 * GDN_HEADS])
    c += 2 * GDN_HEADS
    nw = NSA_HEADS * NSA_DH
    w_q = w_in[:, c:c + nw].reshape(d, NSA_GROUPS, NSA_HPG, NSA_DH) * (NSA_DH ** -0.5)
    c += nw
    w_qpad = jnp.zeros((d, NSA_GROUPS, NSA_HPG, NSA_GROUPS, NSA_DH), w_in.dtype)
    for g in range(NSA_GROUPS):
        w_qpad = w_qpad.at[:, g, :, g, :].set(w_q[:, g])
    w_qpad = w_qpad.reshape(d, NSA_HEADS * LANES)
    kvw = NSA_GROUPS * NSA_DH
    w_cmp = w_in[:, c:c + 2 * kvw]
    w_kv4 = w_in[:, c + 2 * kvw:c + 6 * kvw]
    c += 6 * kvw
    w_gate = jnp.zeros((d, LANES), w_in.dtype).at[:, :3 * NSA_HEADS].set(w_in[:, c:c + 3 * NSA_HEADS])
    ws = [w_qkv, w_z, w_ba, w_qpad, w_cmp, w_kv4, w_gate]
    dts = [F32, F32, F32, BF16, F32, BF16, F32]
    return [w.astype(BF16) for w in ws], dts


def kernel(x, rel_bias, final_norm, e_attn_norm, e_w_in, e_conv_w, e_a_log, e_dt_bias, e_gdn_norm, e_cmpk_pe, e_cmpk_w1, e_cmpk_w2, e_cmpv_pe, e_cmpv_w1, e_cmpv_w2, e_w_out, e_ffn_norm, e_ffn_gate, e_ffn_up, e_ffn_down, o_attn_norm, o_mu, o_w_r, o_w_k, o_w_v, o_w_o, o_w0, o_w1, o_w2, o_a0, o_a1, o_a2, o_g1, o_g2, o_k_k, o_k_a, o_r_k, o_lnx_w, o_lnx_b, o_ffn_norm, o_ffn_gate, o_ffn_up, o_ffn_down):
    b, s, d = x.shape
    m = b * s
    xf = x.reshape(m, d)
    bf = lambda w: w.astype(BF16)

    ws, dts = _in_proj_weights(e_w_in[0])
    qkv, z, ba, qpad, kvc, kv4, gl = norm_linear(xf, e_attn_norm[0], ws, dts)
    sh = lambda t: t.reshape(b, s, t.shape[-1])
    o_gdn = gated_deltanet(sh(qkv), sh(z), sh(ba), e_conv_w[0], e_a_log[0], e_dt_bias[0], e_gdn_norm[0])
    kcvc = compress(sh(kvc), e_cmpk_pe[0], e_cmpk_w1[0], e_cmpk_w2[0], e_cmpv_pe[0], e_cmpv_w1[0], e_cmpv_w2[0])
    o_nsa = nsa_attention(sh(qpad), sh(gl), kcvc, sh(kv4), rel_bias)
    gw = GDN_HEADS * LANES
    xf = linear_res(xf, [o_gdn.reshape(m, -1), o_nsa.reshape(m, -1)], [bf(e_w_out[0][:gw]), bf(e_w_out[0][gw:])])
    xf = ffn(xf, e_ffn_norm[0], bf(e_ffn_gate[0]), bf(e_ffn_up[0]), bf(e_ffn_down[0]), final_norm, False)

    r, k, v, lw, kk, a, g = rwkv_proj(xf.reshape(b, s, d), o_attn_norm[0], o_mu[0], o_w_r[0], o_w_k[0], o_w_v[0],
                                      o_w0[0], o_w1[0], o_w2[0], o_a0[0], o_a1[0], o_a2[0], o_g1[0], o_g2[0],
                                      o_k_k[0], o_k_a[0])
    yg = wkv7(r, k, v, lw, kk, a, g, o_lnx_w[0], o_lnx_b[0], o_r_k[0].reshape(-1))
    xf = linear_res(xf, [yg.reshape(m, d)], [bf(o_w_o[0])])
    xf = ffn(xf, o_ffn_norm[0], bf(o_ffn_gate[0]), bf(o_ffn_up[0]), bf(o_ffn_down[0]), final_norm, True)
    return xf.reshape(b, s, d)
```

```python
import functools
import math

import jax
import jax.numpy as jnp
from jax import lax
from jax.experimental import pallas as pl
from jax.experimental.pallas import tpu as pltpu

F32 = jnp.float32
BF16 = jnp.bfloat16

VMEM_LIMIT_BYTES = 52 * 1024 * 1024
LANES = 128
CHUNK = 64

RMS_EPS = 1e-6
GDN_HEADS = 4
GDN_CONV = 4
NSA_HEADS = 8
NSA_GROUPS = 2
NSA_HPG = 4
NSA_DH = 64
CMP_LEN = 32
CMP_STRIDE = 16
SLC_LEN = 64
N_SELECT = 8
WINDOW = 512
NUM_BUCKETS = 32
REL_MAX_DIST = 128
NEG = -1e30
KV_PAD = 640
WIN_TILE = 640
RWKV_GN_EPS = 64e-5
BIAS_FAR_DIST = math.ceil((NUM_BUCKETS // 2) * (REL_MAX_DIST / (NUM_BUCKETS // 2))
                          ** ((NUM_BUCKETS // 2 - 1) / (NUM_BUCKETS // 2)))
CMP_TILE_ORIGIN = 64
CMP_PAST_BLOCKS = -(-(BIAS_FAR_DIST + CMP_LEN - 1) // CMP_STRIDE)
CMP_FUTURE_BLOCK = (SLC_LEN - CMP_LEN) // CMP_STRIDE + 1


def _cparams(sem):
    return pltpu.CompilerParams(dimension_semantics=sem, vmem_limit_bytes=VMEM_LIMIT_BYTES)


def _dot(a, b):
    return jnp.dot(a, b, preferred_element_type=F32)


def _dot_nt(a, b):
    return lax.dot_general(a, b, (((1,), (1,)), ((), ())), preferred_element_type=F32)


def _split(x, terms):
    out = []
    rem = x
    for _ in range(terms - 1):
        p = rem.astype(BF16)
        out.append(p)
        rem = rem - p.astype(F32)
    out.append(rem.astype(BF16))
    return out


def _mm(a, b, passes=1, nt=False):
    d = _dot_nt if nt else _dot
    if passes == 1:
        return d(a.astype(BF16), b.astype(BF16))
    ah, al = _split(a, 2)
    bh, bl = _split(b, 2)
    return d(ah, bh) + (d(ah, bl) + d(al, bh))


def _mm_exact_lhs(a01, b):
    a = a01.astype(BF16)
    b0, b1, b2 = _split(b, 3)
    return _dot(a, b0) + (_dot(a, b1) + _dot(a, b2))


def _mm_exact_rhs(a, b01):
    b = b01.astype(BF16)
    a0, a1, a2 = _split(a, 3)
    return _dot(a0, b) + (_dot(a1, b) + _dot(a2, b))


def _silu(x):
    return x * jax.nn.sigmoid(x)


def _softplus(x):
    return jnp.maximum(x, 0.0) + jnp.log1p(jnp.exp(-jnp.abs(x)))


def _lane_masks():
    lane = lax.broadcasted_iota(jnp.int32, (1, LANES), 1)
    m0 = (lane < 64).astype(F32)
    return m0, 1.0 - m0


def _stack2(x, m0, m1):
    return jnp.concatenate([x * m0, x * m1], axis=0)


def _tri_inv_cat(lcats, passes):
    m0, m1 = _lane_masks()
    i = lax.broadcasted_iota(jnp.int32, (CHUNK, LANES), 0)
    j = lax.broadcasted_iota(jnp.int32, (CHUNK, LANES), 1) & (CHUNK - 1)
    eye = (i == j).astype(F32)
    blk16 = (i >> 4) == (j >> 4)
    st = lambda x: _stack2(x, m0, m1)
    mm = lambda a, b: _mm(a, b, passes)
    each = lambda f, *ls: [f(*xs) for xs in zip(*ls)]
    ld = each(lambda l: jnp.where(blk16, l, 0.0), lcats)
    off = each(lambda l: jnp.where(blk16, 0.0, l), lcats)
    sld = each(st, ld)
    l2 = each(mm, ld, sld)
    sl2 = each(st, l2)
    n1 = each(lambda d, l, s: ((eye - d) + l) - mm(d, s), ld, l2, sl2)
    l4 = each(mm, l2, sl2)
    sl4 = each(st, l4)
    n2 = each(lambda a, s: a + mm(a, s), n1, sl4)
    l8 = each(mm, l4, sl4)
    x = each(lambda a, l: a + mm(a, st(l)), n2, l8)
    sx = each(st, x)
    mt = each(lambda a, o: mm(a, st(o)), x, off)
    m2 = each(lambda a: mm(a, st(a)), mt)
    mx = each(mm, mt, sx)
    m2x = each(mm, m2, sx)
    m3x = each(lambda a, b: mm(a, st(b)), m2, mx)
    return each(lambda a, b, c, d: (a - b) + (c - d), x, mx, m2x, m3x)


def _tri_ones(n, lower, block=None):
    i = lax.broadcasted_iota(jnp.int32, (n, n), 0)
    j = lax.broadcasted_iota(jnp.int32, (n, n), 1)
    m = (i >= j) if lower else (i <= j)
    if block is not None:
        sh = block.bit_length() - 1
        m = jnp.logical_and(m, (i >> sh) == (j >> sh))
    return m.astype(F32)


def _norm_linear_kernel(x_ref, nw_ref, *refs, n_out):
    w_refs, o_refs = refs[:n_out], refs[n_out:]
    x = x_ref[...]
    h = x * lax.rsqrt(jnp.mean(x * x, axis=-1, keepdims=True) + RMS_EPS) * nw_ref[...]
    hb = h.astype(BF16)
    for w_ref, o_ref in zip(w_refs, o_refs):
        o_ref[...] = _dot(hb, w_ref[...]).astype(o_ref.dtype)


def norm_linear(x, nw, weights, out_dtypes, tm=512):
    m, k = x.shape
    n_out = len(weights)
    in_specs = [pl.BlockSpec((tm, k), lambda i: (i, 0)), pl.BlockSpec((1, k), lambda i: (0, 0))]
    in_specs += [pl.BlockSpec(w.shape, lambda i: (0, 0)) for w in weights]
    out_specs = [pl.BlockSpec((tm, w.shape[1]), lambda i: (i, 0)) for w in weights]
    out_shape = [jax.ShapeDtypeStruct((m, w.shape[1]), dt) for w, dt in zip(weights, out_dtypes)]
    return pl.pallas_call(
        functools.partial(_norm_linear_kernel, n_out=n_out),
        grid=(m // tm,), in_specs=in_specs, out_specs=out_specs, out_shape=out_shape,
        compiler_params=_cparams(("parallel",)), name="norm_linear",
    )(x, nw.reshape(1, k), *weights)


def _linear_res_kernel(res_ref, *refs, n_in):
    a_refs, w_refs, o_ref = refs[:n_in], refs[n_in:2 * n_in], refs[2 * n_in]
    acc = res_ref[...]
    for a_ref, w_ref in zip(a_refs, w_refs):
        acc = acc + _dot(a_ref[...].astype(BF16), w_ref[...])
    o_ref[...] = acc


def linear_res(res, acts, weights, tm=512):
    m, n = res.shape
    n_in = len(acts)
    in_specs = [pl.BlockSpec((tm, n), lambda i: (i, 0))]
    in_specs += [pl.BlockSpec((tm, a.shape[1]), lambda i: (i, 0)) for a in acts]
    in_specs += [pl.BlockSpec(w.shape, lambda i: (0, 0)) for w in weights]
    return pl.pallas_call(
        functools.partial(_linear_res_kernel, n_in=n_in),
        grid=(m // tm,), in_specs=in_specs, out_specs=pl.BlockSpec((tm, n), lambda i: (i, 0)),
        out_shape=jax.ShapeDtypeStruct((m, n), F32),
        compiler_params=_cparams(("parallel",)), name="linear_res",
    )(res, *acts, *weights)


def _ffn_kernel(x_ref, nw_ref, wg_ref, wu_ref, wd_ref, fw_ref, o_ref, xn_scr, acc_scr, *, final_norm):
    f = pl.program_id(1)

    @pl.when(f == 0)
    def _():
        x = x_ref[...]
        h = x * lax.rsqrt(jnp.mean(x * x, axis=-1, keepdims=True) + RMS_EPS) * nw_ref[...]
        xn_scr[...] = h.astype(BF16)
        acc_scr[...] = jnp.zeros_like(acc_scr)

    xn = xn_scr[...]
    g = _dot(xn, wg_ref[...])
    u = _dot(xn, wu_ref[...])
    hid = (_silu(g) * u).astype(BF16)
    acc_scr[...] += _dot(hid, wd_ref[...])

    @pl.when(f == pl.num_programs(1) - 1)
    def _():
        y = x_ref[...] + acc_scr[...]
        if final_norm:
            y = y * lax.rsqrt(jnp.mean(y * y, axis=-1, keepdims=True) + RMS_EPS) * fw_ref[...]
        o_ref[...] = y


def ffn(x, nw, wg, wu, wd, fw, final_norm, tm=512, tf=1408):
    m, d = x.shape
    dff = wg.shape[1]
    return pl.pallas_call(
        functools.partial(_ffn_kernel, final_norm=final_norm),
        grid=(m // tm, dff // tf),
        in_specs=[pl.BlockSpec((tm, d), lambda i, f: (i, 0)),
                  pl.BlockSpec((1, d), lambda i, f: (0, 0)),
                  pl.BlockSpec((d, tf), lambda i, f: (0, f)),
                  pl.BlockSpec((d, tf), lambda i, f: (0, f)),
                  pl.BlockSpec((tf, d), lambda i, f: (f, 0)),
                  pl.BlockSpec((1, d), lambda i, f: (0, 0))],
        out_specs=pl.BlockSpec((tm, d), lambda i, f: (i, 0)),
        out_shape=jax.ShapeDtypeStruct((m, d), F32),
        scratch_shapes=[pltpu.VMEM((tm, d), BF16), pltpu.VMEM((tm, d), F32)],
        compiler_params=_cparams(("parallel", "arbitrary")), name="ffn",
    )(x, nw.reshape(1, d), wg, wu, wd, fw.reshape(1, d))


def _gdn_kernel(q_ref, k_ref, v_ref, qh_ref, kh_ref, vh_ref, cwq_ref, cwk_ref, cwv_ref,
                z_ref, ba_ref, bat_ref, hp_ref, nw_ref, o_ref, scr, state, *, tc, inv_passes):
    h = pl.program_id(1)
    t = pl.program_id(2)

    @pl.when(t == 0)
    def _():
        state[...] = jnp.zeros_like(state)

    not_first = (t > 0).astype(F32)

    def conv(x_ref, halo_ref, w_ref, slot):
        scr[slot, 0:8, :] = halo_ref[0] * not_first
        scr[slot, 8:8 + tc, :] = x_ref[0]
        w = w_ref[...]
        y = (w[3:4] * scr[slot, 8:8 + tc, :] + w[2:3] * scr[slot, 7:7 + tc, :]
             + w[1:2] * scr[slot, 6:6 + tc, :] + w[0:1] * scr[slot, 5:5 + tc, :])
        return _silu(y)

    q = conv(q_ref, qh_ref, cwq_ref, 0)
    k = conv(k_ref, kh_ref, cwk_ref, 1)
    v = conv(v_ref, vh_ref, cwv_ref, 2)
    qn = q * lax.rsqrt(jnp.sum(q * q, axis=-1, keepdims=True) + 1e-6)
    kn = k * lax.rsqrt(jnp.sum(k * k, axis=-1, keepdims=True) + 1e-6)

    lane = lax.broadcasted_iota(jnp.int32, (1, LANES), 1)
    ba = ba_ref[0]
    b_col = jnp.sum(jnp.where(lane == h, ba, 0.0), axis=-1, keepdims=True)
    a_col = jnp.sum(jnp.where(lane == h + GDN_HEADS, ba, 0.0), axis=-1, keepdims=True)
    beta = jax.nn.sigmoid(b_col)
    a_log = hp_ref[0, 0:1, :]
    dt_b = hp_ref[0, 1:2, :]
    g_rep = -jnp.exp(a_log) * _softplus(a_col + dt_b)

    m0, m1 = _lane_masks()
    i64 = lax.broadcasted_iota(jnp.int32, (CHUNK, LANES), 0)
    j64 = lax.broadcasted_iota(jnp.int32, (CHUNK, LANES), 1) & (CHUNK - 1)
    low_half = lax.broadcasted_iota(jnp.int32, (CHUNK, LANES), 1) < CHUNK
    ltri_bd = _tri_ones(2 * CHUNK, True, CHUNK)
    utri_bd = _tri_ones(2 * CHUNK, False, CHUNK)
    eye128 = (lax.broadcasted_iota(jnp.int32, (LANES, LANES), 0)
              == lax.broadcasted_iota(jnp.int32, (LANES, LANES), 1)).astype(F32)
    zeros64 = jnp.zeros((CHUNK, LANES), F32)
    scale = 1.0 / math.sqrt(LANES)
    nw = nw_ref[...]

    npair = tc // (2 * CHUNK)
    each = lambda f, *ls: [f(*xs) for xs in zip(*ls)]
    halves = lambda x: jnp.where(low_half, x[0:CHUNK], x[CHUNK:2 * CHUNK])
    prow = [slice(2 * CHUNK * p, 2 * CHUNK * (p + 1)) for p in range(npair)]
    kp = [kn[rs] for rs in prow]
    qp = [qn[rs] * scale for rs in prow]
    vp = [v[rs] for rs in prow]
    beta_p = [beta[rs] for rs in prow]
    gc_col = [_mm_exact_lhs(ltri_bd, g_rep[rs]) for rs in prow]
    g_row = [-jnp.exp(a_log[:, 0:1]) * _softplus(bat_ref[0, 0, p, 1:2, :] + dt_b[:, 0:1]) for p in range(npair)]
    gc_row = each(lambda x: _mm_exact_rhs(jnp.broadcast_to(x, (CHUNK, LANES)), utri_bd), g_row)
    dcat = each(lambda c_, r_: jnp.exp(jnp.where(i64 >= j64, halves(c_) - r_, -jnp.inf)), gc_col, gc_row)
    kb = each(lambda k_, b_: k_ * b_, kp, beta_p)
    gram = each(lambda x, y_: _mm(x, y_, nt=True), kb, kp)
    gq = each(lambda x, y_: _mm(x, y_, nt=True), qp, kp)
    strict = each(lambda x, d_: jnp.where(i64 > j64, halves(x) * d_, 0.0), gram, dcat)
    qk_bd = each(lambda x, d_: _stack2(jnp.where(i64 >= j64, halves(x) * d_, 0.0), m0, m1), gq, dcat)
    eg = each(jnp.exp, gc_col)
    g_last = each(lambda x: [x[CHUNK * (c + 1) - 1:CHUNK * (c + 1), :] for c in range(2)], gc_col)
    kdt = each(lambda k_, gl_, gc_: jnp.concatenate(
        [k_[CHUNK * c:CHUNK * (c + 1)] * jnp.exp(gl_[c] - gc_[CHUNK * c:CHUNK * (c + 1)]) for c in range(2)],
        axis=0).T, kp, g_last, gc_col)
    tinv = _tri_inv_cat(strict, inv_passes)
    uw = each(lambda t_, v_, b_, kb_, e_: _mm(_stack2(t_, m0, m1), jnp.concatenate([v_ * b_, kb_ * e_], axis=1)),
              tinv, vp, beta_p, kb, eg)
    u = each(lambda x: x[:, 0:LANES], uw)
    w = each(lambda x: x[:, LANES:2 * LANES], uw)
    q_hat = each(lambda q_, e_, m_, w_: q_ * e_ - _mm(m_, w_), qp, eg, qk_bd, w)
    o_hat = each(_mm, qk_bd, u)
    pad = lambda x, c: jnp.concatenate([x[0:CHUNK], zeros64] if c == 0 else [zeros64, x[CHUNK:2 * CHUNK]], axis=0)
    trans = [eye128 * jnp.exp(g_last[p][c]) - _mm(kdt[p], pad(w[p], c)) for p in range(npair) for c in range(2)]
    h_add = [_mm(kdt[p], pad(u[p], c)) for p in range(npair) for c in range(2)]
    s_state = state[...]
    for p in range(npair):
        for c in range(2):
            cr = slice(CHUNK * c, CHUNK * (c + 1))
            o = _mm(q_hat[p][cr], s_state) + o_hat[p][cr]
            s_state = _mm(trans[2 * p + c], s_state, passes=3) + h_add[2 * p + c]
            on = o * lax.rsqrt(jnp.mean(o * o, axis=-1, keepdims=True) + RMS_EPS) * nw
            r0 = 2 * CHUNK * p + CHUNK * c
            o_ref[0, r0:r0 + CHUNK, :] = (on * _silu(z_ref[0, r0:r0 + CHUNK, :])).astype(o_ref.dtype)
    state[...] = s_state


def gated_deltanet(qkv, z, ba, conv_w, a_log, dt_bias, norm_w, tc=1024, inv_passes=1):
    b, s, _ = qkv.shape
    hh = GDN_HEADS
    bat = ba[..., :2 * hh].reshape(b, s // 128, 128, 2, hh).transpose(0, 4, 1, 3, 2)
    hp = jnp.zeros((hh, 8, LANES), F32)
    hp = hp.at[:, 0, :].set(a_log.astype(F32)[:, None]).at[:, 1, :].set(dt_bias.astype(F32)[:, None])
    cw = jnp.zeros((8, 3 * hh * LANES), F32).at[:GDN_CONV].set(conv_w.astype(F32))
    hb = tc // 8
    blk = lambda off: pl.BlockSpec((1, tc, LANES), lambda bi, hi, ti: (bi, ti, hi + off))
    halo = lambda off: pl.BlockSpec((1, 8, LANES), lambda bi, hi, ti: (bi, jnp.maximum(ti * hb - 1, 0), hi + off))
    cws = lambda off: pl.BlockSpec((8, LANES), lambda bi, hi, ti: (0, hi + off))
    return pl.pallas_call(
        functools.partial(_gdn_kernel, tc=tc, inv_passes=inv_passes),
        grid=(b, hh, s // tc),
        in_specs=[blk(0), blk(hh), blk(2 * hh), halo(0), halo(hh), halo(2 * hh), cws(0), cws(hh), cws(2 * hh),
                  blk(0),
                  pl.BlockSpec((1, tc, LANES), lambda bi, hi, ti: (bi, ti, 0)),
                  pl.BlockSpec((1, 1, tc // 128, 2, LANES), lambda bi, hi, ti: (bi, hi, ti, 0, 0)),
                  pl.BlockSpec((1, 8, LANES), lambda bi, hi, ti: (hi, 0, 0)),
                  pl.BlockSpec((1, LANES), lambda bi, hi, ti: (0, 0))],
        out_specs=blk(0),
        out_shape=jax.ShapeDtypeStruct((b, s, hh * LANES), BF16),
        scratch_shapes=[pltpu.VMEM((3, tc + 8, LANES), F32), pltpu.VMEM((LANES, LANES), F32)],
        compiler_params=_cparams(("parallel", "parallel", "arbitrary")), name="gated_deltanet",
    )(qkv, qkv, qkv, qkv, qkv, qkv, cw, cw, cw, z, ba, bat, hp, norm_w.astype(F32).reshape(1, LANES))


def _compress_kernel(x_ref, pe_ref, w1_ref, w2_ref, o_ref):
    nb = x_ref.shape[3]
    acc = jnp.zeros((nb, LANES), F32)
    for g in range(NSA_GROUPS):
        x = x_ref[0, 0, g]
        first = _dot((x + pe_ref[0, 0]).astype(BF16), w1_ref[0, 0])
        second = _dot((x + pe_ref[0, 1]).astype(BF16), w1_ref[0, 1])
        hid = _silu(first + pltpu.roll(second, nb - 1, 0))
        acc = acc + _dot(hid.astype(BF16), w2_ref[0, g])
    o_ref[0, 0] = acc.astype(o_ref.dtype)


def compress(kvc, pe_k, w1_k, w2_k, pe_v, w1_v, w2_v):
    b, s, _ = kvc.shape
    nb = s // CMP_STRIDE
    width = CMP_STRIDE * NSA_DH
    x = kvc.reshape(b, nb, CMP_STRIDE, 2, NSA_GROUPS, NSA_DH).transpose(0, 3, 4, 1, 2, 5).reshape(b, 2, NSA_GROUPS, nb, width)
    pe = jnp.stack([pe_k, pe_v]).astype(F32).reshape(2, 2, 1, width)
    w1 = jnp.stack([w1_k, w1_v]).astype(BF16).reshape(2, 2, width, -1)
    hid = w1.shape[-1]
    w2 = jnp.stack([w2_k, w2_v]).astype(BF16)
    w2p = jnp.zeros((2, NSA_GROUPS, hid, LANES), BF16)
    for g in range(NSA_GROUPS):
        w2p = w2p.at[:, g, :, NSA_DH * g:NSA_DH * (g + 1)].set(w2)
    return pl.pallas_call(
        _compress_kernel,
        grid=(b, 2),
        in_specs=[pl.BlockSpec((1, 1, NSA_GROUPS, nb, width), lambda bi, j: (bi, j, 0, 0, 0)),
                  pl.BlockSpec((1, 2, 1, width), lambda bi, j: (j, 0, 0, 0)),
                  pl.BlockSpec((1, 2, width, hid), lambda bi, j: (j, 0, 0, 0)),
                  pl.BlockSpec((1, NSA_GROUPS, hid, LANES), lambda bi, j: (j, 0, 0, 0))],
        out_specs=pl.BlockSpec((1, 1, nb, LANES), lambda bi, j: (bi, j, 0, 0)),
        out_shape=jax.ShapeDtypeStruct((b, 2, nb, LANES), BF16),
        compiler_params=_cparams(("parallel", "parallel")), name="nsa_compress",
    )(x, pe, w1, w2p)


def _softmax_pv(s, v):
    m = jnp.max(s, axis=-1, keepdims=True)
    p = jnp.exp(s - m)
    l = jnp.sum(p, axis=-1, keepdims=True)
    return _dot(p.astype(BF16), v) / l


def _nsa_kernel(q_ref, gl_ref, kcvc_ref, kv_ref, bw_ref, bfar_ref, bc_ref, ovl_ref, o_ref):
    n = pl.program_id(1)
    qb_rows = q_ref.shape[1]
    ns = (kv_ref.shape[1] - KV_PAD) // SLC_LEN
    far_w = 8 * SLC_LEN
    lane = lax.broadcasted_iota(jnp.int32, (qb_rows, LANES), 1)
    lane_s = lane & (ns - 1)
    gates = jax.nn.sigmoid(gl_ref[0])
    kc = kcvc_ref[0, 0]
    vc = kcvc_ref[0, 1]
    ovl = ovl_ref[...]
    row_blk = lax.broadcasted_iota(jnp.int32, (LANES, far_w), 0)
    col_blk = lax.broadcasted_iota(jnp.int32, (LANES, far_w), 1) >> 6
    wcol = lax.broadcasted_iota(jnp.int32, (NSA_HPG * qb_rows, WIN_TILE), 1)
    groups = range(NSA_GROUPS)
    tile4 = lambda x: jnp.concatenate([x] * NSA_HPG, axis=0)

    qs = [jnp.concatenate([q_ref[0, :, LANES * (NSA_HPG * g + hh):LANES * (NSA_HPG * g + hh + 1)]
                           for hh in range(NSA_HPG)], axis=0) for g in groups]
    c0 = (qb_rows // CMP_STRIDE) * n
    lane4 = lax.broadcasted_iota(jnp.int32, (NSA_HPG * qb_rows, LANES), 1)
    shift = (c0 + (LANES - CMP_TILE_ORIGIN)) % LANES
    bc = [jnp.where(lane4 < c0 - (CMP_PAST_BLOCKS - 1), bfar_ref[g],
                    jnp.where(lane4 >= c0 + CMP_FUTURE_BLOCK, NEG, pltpu.roll(bc_ref[g], shift, 1))) for g in groups]
    start = pl.multiple_of(n * SLC_LEN + (KV_PAD - 3 * SLC_LEN), SLC_LEN)
    k_near = kv_ref[0, pl.ds(start, 4 * SLC_LEN), 0:LANES]
    v_near = kv_ref[0, pl.ds(start, 4 * SLC_LEN), LANES:2 * LANES]
    wst = pl.multiple_of(n * SLC_LEN + (KV_PAD + SLC_LEN - WIN_TILE), SLC_LEN)
    k_w = kv_ref[0, pl.ds(wst, WIN_TILE), 2 * LANES:3 * LANES]
    v_w = kv_ref[0, pl.ds(wst, WIN_TILE), 3 * LANES:4 * LANES]
    s_cmp = [_dot_nt(qs[g], kc) + bc[g] for g in groups]
    s_near = [_dot_nt(qs[g], k_near) + bw_ref[g, :, WIN_TILE - 4 * SLC_LEN:WIN_TILE] for g in groups]
    s_win = [jnp.where(wcol >= (WIN_TILE - SLC_LEN) - SLC_LEN * n, _dot_nt(qs[g], k_w) + bw_ref[g], NEG)
             for g in groups]

    p_cmp = []
    for g in groups:
        m = jnp.max(s_cmp[g], axis=-1, keepdims=True)
        e = jnp.where(bc[g] > 0.5 * NEG, jnp.exp(s_cmp[g] - m), 0.0)
        p_cmp.append(e / jnp.maximum(jnp.sum(e, axis=-1, keepdims=True), 1e-30))
    o_cmp = [_dot(p.astype(BF16), vc) for p in p_cmp]
    imp = [_mm_exact_rhs(p[0:qb_rows] + p[qb_rows:2 * qb_rows] + p[2 * qb_rows:3 * qb_rows] + p[3 * qb_rows:4 * qb_rows],
                         ovl) for p in p_cmp]

    forced = jnp.logical_or(jnp.logical_or(lane_s == 0, lane_s == n), lane_s == n - 1)
    val = [jnp.where(lane_s > n, -1.0, jnp.where(forced, 1e6, x)) for x in imp]
    rank = [jnp.zeros((qb_rows, LANES), F32) for _ in groups]
    for r in range(1, ns):
        tie = (lane_s >= r).astype(F32)
        for g in groups:
            other = pltpu.roll(val[g], r, 1)
            rank[g] = rank[g] + jnp.where(other > val[g], 1.0, jnp.where(other == val[g], tie, 0.0))
    sel = [jnp.where(jnp.logical_and(x < N_SELECT, lane < ns), 1.0, 0.0) for x in rank]

    o_win = [_softmax_pv(s_win[g], v_w) for g in groups]

    e_near = (lax.broadcasted_iota(jnp.int32, (LANES, 4 * SLC_LEN), 0)
              == (lax.broadcasted_iota(jnp.int32, (LANES, 4 * SLC_LEN), 1) >> 6) + (n - 3)).astype(BF16)
    carry = []
    for g in groups:
        sm = _dot(sel[g].astype(BF16), e_near)
        s = jnp.where(tile4(sm) > 0.5, s_near[g], NEG)
        m = jnp.max(s, axis=-1, keepdims=True)
        p = jnp.exp(s - m)
        carry += [m, jnp.sum(p, axis=-1, keepdims=True), _dot(p.astype(BF16), v_near)]
    sel_far = [jnp.where(lane <= n - 4, x, 0.0).astype(BF16) for x in sel]
    b_far = [jnp.concatenate([bfar_ref[g]] * (far_w // LANES), axis=1) for g in groups]

    def far_body(it, carry):
        st = pl.multiple_of(KV_PAD + it * far_w, math.gcd(KV_PAD, far_w))
        k_c = kv_ref[0, pl.ds(st, far_w), 0:LANES]
        v_c = kv_ref[0, pl.ds(st, far_w), LANES:2 * LANES]
        e_c = (row_blk == col_blk + 8 * it).astype(BF16)
        sc = [_dot_nt(qs[g], k_c) + b_far[g] for g in groups]
        smc = [_dot(sel_far[g], e_c) for g in groups]
        out = []
        for g in groups:
            m_i, l_i, acc_i = carry[3 * g:3 * g + 3]
            s_g = jnp.where(tile4(smc[g]) > 0.5, sc[g], NEG)
            m_n = jnp.maximum(m_i, jnp.max(s_g, axis=-1, keepdims=True))
            alpha = jnp.exp(m_i - m_n)
            pc = jnp.exp(s_g - m_n)
            out += [m_n, alpha * l_i + jnp.sum(pc, axis=-1, keepdims=True),
                    alpha * acc_i + _dot(pc.astype(BF16), v_c)]
        return tuple(out)

    carry = lax.fori_loop(0, (n // 4 + 1) // 2, far_body, tuple(carry))
    o_slc = [carry[3 * g + 2] / carry[3 * g + 1] for g in groups]

    for g in groups:
        pieces = []
        for hh in range(NSA_HPG):
            hd = NSA_HPG * g + hh
            rs = slice(qb_rows * hh, qb_rows * (hh + 1))
            pieces.append(gates[:, 3 * hd:3 * hd + 1] * o_cmp[g][rs] + gates[:, 3 * hd + 1:3 * hd + 2] * o_slc[g][rs]
                          + gates[:, 3 * hd + 2:3 * hd + 3] * o_win[g][rs])
        for hp in range(2):
            lo, hi = pieces[2 * hp], pieces[2 * hp + 1]
            if g == 0:
                hi = pltpu.roll(hi, NSA_DH, 1)
            else:
                lo = pltpu.roll(lo, NSA_DH, 1)
            o_ref[0, :, LANES * (2 * g + hp):LANES * (2 * g + hp + 1)] = jnp.where(lane < NSA_DH, lo, hi).astype(o_ref.dtype)


def _rel_bucket(dist):
    n = jnp.maximum(dist, 0)
    exact = NUM_BUCKETS // 2
    nf = jnp.maximum(n, 1).astype(F32)
    large = exact + (jnp.log(nf / exact) / math.log(REL_MAX_DIST / exact) * (NUM_BUCKETS - exact)).astype(jnp.int32)
    return jnp.where(n < exact, n, jnp.minimum(large, NUM_BUCKETS - 1))


def nsa_attention(qpad, gl, kcvc, kv4, rel_bias):
    b, s, _ = qpad.shape
    qb = SLC_LEN
    nq = s // qb
    nc = s // CMP_STRIDE - CMP_LEN // CMP_STRIDE + 1
    ns = s // SLC_LEN
    table = rel_bias.astype(F32)
    dist_w = jnp.arange(qb)[:, None] + (WIN_TILE - SLC_LEN) - jnp.arange(WIN_TILE)[None, :]
    ok_w = jnp.logical_and(dist_w >= 0, dist_w < WINDOW)
    bw = jnp.where(ok_w[None], table[_rel_bucket(dist_w)].transpose(2, 0, 1), NEG)
    bw = bw.reshape(NSA_GROUPS, NSA_HPG * qb, WIN_TILE)
    far_val = table[NUM_BUCKETS - 1]
    bfar = jnp.broadcast_to(far_val[:, None, None], (NSA_HEADS, qb, LANES)).reshape(NSA_GROUPS, NSA_HPG * qb, LANES)
    cmp_end = jnp.arange(LANES) * CMP_STRIDE + CMP_LEN - 1
    dist_c = jnp.arange(qb)[:, None] - CMP_STRIDE * (jnp.arange(LANES)[None, :] - CMP_TILE_ORIGIN) - (CMP_LEN - 1)
    bc = jnp.where((dist_c >= 0)[None], table[_rel_bucket(dist_c)].transpose(2, 0, 1), NEG)
    bc = bc.reshape(NSA_GROUPS, NSA_HPG * qb, LANES)
    cmp_start = jnp.arange(LANES) * CMP_STRIDE
    slc_j = jnp.arange(LANES) % ns
    ovl = jnp.logical_and(jnp.logical_and(cmp_start[:, None] < (slc_j[None, :] + 1) * SLC_LEN,
                                          cmp_end[:, None] >= slc_j[None, :] * SLC_LEN),
                          jnp.arange(LANES)[:, None] < nc).astype(F32)
    kvp = jnp.pad(kv4, ((0, 0), (KV_PAD, 0), (0, 0)))
    sp = s + KV_PAD
    return pl.pallas_call(
        _nsa_kernel,
        grid=(b, nq),
        in_specs=[pl.BlockSpec((1, qb, NSA_HEADS * LANES), lambda bi, n: (bi, n, 0)),
                  pl.BlockSpec((1, qb, LANES), lambda bi, n: (bi, n, 0)),
                  pl.BlockSpec((1, 2, s // CMP_STRIDE, LANES), lambda bi, n: (bi, 0, 0, 0)),
                  pl.BlockSpec((1, sp, 4 * LANES), lambda bi, n: (bi, 0, 0)),
                  pl.BlockSpec((NSA_GROUPS, NSA_HPG * qb, WIN_TILE), lambda bi, n: (0, 0, 0)),
                  pl.BlockSpec((NSA_GROUPS, NSA_HPG * qb, LANES), lambda bi, n: (0, 0, 0)),
                  pl.BlockSpec((NSA_GROUPS, NSA_HPG * qb, LANES), lambda bi, n: (0, 0, 0)),
                  pl.BlockSpec((LANES, LANES), lambda bi, n: (0, 0))],
        out_specs=pl.BlockSpec((1, qb, NSA_HEADS * NSA_DH), lambda bi, n: (bi, n, 0)),
        out_shape=jax.ShapeDtypeStruct((b, s, NSA_HEADS * NSA_DH), BF16),
        compiler_params=_cparams(("parallel", "arbitrary")), name="nsa_attention",
    )(qpad, gl, kcvc, kvp, bw, bfar, bc, ovl)


def _rwkv_proj_kernel(x_ref, halo_ref, nw_ref, mu_ref, vec_ref, wr_ref, wk_ref, wv_ref,
                      w1_ref, w2_ref, a1_ref, a2_ref, g1_ref, g2_ref,
                      r_ref, k_ref, v_ref, lw_ref, kk_ref, a_ref, g_ref, scr, *, tm):
    t = pl.program_id(1)
    nw = nw_ref[...]

    def norm(x):
        return x * lax.rsqrt(jnp.mean(x * x, axis=-1, keepdims=True) + RMS_EPS) * nw

    h = norm(x_ref[0])
    scr[0:8, :] = norm(halo_ref[0]) * (t > 0).astype(F32)
    scr[8:8 + tm, :] = h
    xx = scr[7:7 + tm, :] - h
    mix = lambda j: (h + xx * mu_ref[j:j + 1, :]).astype(BF16)
    w0, a0, k_k, k_a = vec_ref[0:1, :], vec_ref[1:2, :], vec_ref[2:3, :], vec_ref[3:4, :]
    r = _dot(mix(0), wr_ref[...])
    wl = w0 + _dot(jnp.tanh(_dot(mix(1), w1_ref[...])).astype(BF16), w2_ref[...])
    k = _dot(mix(2), wk_ref[...])
    v = _dot(mix(3), wv_ref[...])
    a = jax.nn.sigmoid(a0 + _dot(_dot(mix(4), a1_ref[...]).astype(BF16), a2_ref[...]))
    g = _dot(jax.nn.sigmoid(_dot(mix(5), g1_ref[...])).astype(BF16), g2_ref[...])
    w_raw = -_softplus(-wl) - 0.5
    r_ref[0] = r.astype(r_ref.dtype)
    k_ref[0] = (k * (1.0 + (a - 1.0) * k_a)).astype(k_ref.dtype)
    v_ref[0] = v.astype(v_ref.dtype)
    lw_ref[0] = -jnp.exp(w_raw)
    kk_ref[0] = (k * k_k).astype(kk_ref.dtype)
    a_ref[0] = a.astype(a_ref.dtype)
    g_ref[0] = g.astype(g_ref.dtype)


def rwkv_proj(x, nw, mu, w_r, w_k, w_v, w0, w1, w2, a0, a1, a2, g1, g2, k_k, k_a, tm=256):
    b, s, d = x.shape
    pad_to = lambda w, rows, cols: jnp.zeros((rows, cols), BF16).at[:w.shape[0], :w.shape[1]].set(w.astype(BF16))
    lora = lambda n: -(-n // LANES) * LANES
    mu8 = jnp.zeros((8, d), F32).at[:6].set(mu.astype(F32))
    vec = jnp.zeros((8, d), F32).at[0].set(w0).at[1].set(a0).at[2].set(k_k).at[3].set(k_a)
    big = [w.astype(BF16) for w in (w_r, w_k, w_v)]
    small = [pad_to(w1, d, lora(w1.shape[1])), pad_to(w2, lora(w2.shape[0]), d),
             pad_to(a1, d, lora(a1.shape[1])), pad_to(a2, lora(a2.shape[0]), d),
             pad_to(g1, d, lora(g1.shape[1])), pad_to(g2, lora(g2.shape[0]), d)]
    whole = lambda w: pl.BlockSpec(w.shape, lambda bi, ti: (0, 0))
    hb = tm // 8
    tile = pl.BlockSpec((1, tm, d), lambda bi, ti: (bi, ti, 0))
    return pl.pallas_call(
        functools.partial(_rwkv_proj_kernel, tm=tm),
        grid=(b, s // tm),
        in_specs=[tile, pl.BlockSpec((1, 8, d), lambda bi, ti: (bi, jnp.maximum(ti * hb - 1, 0), 0)),
                  pl.BlockSpec((1, d), lambda bi, ti: (0, 0)), whole(mu8), whole(vec)]
                 + [whole(w) for w in big] + [whole(w) for w in small],
        out_specs=[tile] * 7,
        out_shape=[jax.ShapeDtypeStruct((b, s, d), dt) for dt in (BF16, BF16, BF16, F32, BF16, BF16, BF16)],
        scratch_shapes=[pltpu.VMEM((tm + 8, d), F32)],
        compiler_params=_cparams(("parallel", "arbitrary")), name="rwkv_proj",
    )(x, x, nw.reshape(1, d), mu8, vec, *big, *small)


def _wkv_kernel(r_ref, k_ref, v_ref, lw_ref, kk_ref, a_ref, g_ref, lnw_ref, lnb_ref, rk_ref,
                o_ref, state, *, tc, inv_passes):
    t = pl.program_id(2)

    @pl.when(t == 0)
    def _():
        state[...] = jnp.zeros_like(state)

    m0, m1 = _lane_masks()
    lane = lax.broadcasted_iota(jnp.int32, (1, LANES), 1)
    low = lane < 64

    def seg_sum(x):
        return jnp.where(low, jnp.sum(x * m0, axis=-1, keepdims=True), jnp.sum(x * m1, axis=-1, keepdims=True))

    npair = r_ref.shape[2] // LANES
    pairs = range(npair)
    tile_of = lambda ref, pi: ref[0, :, LANES * pi:LANES * (pi + 1)].astype(F32)
    r_all, k_all, v_all = ([tile_of(ref, pi) for pi in pairs] for ref in (r_ref, k_ref, v_ref))
    kk = [tile_of(kk_ref, pi) for pi in pairs]
    kk = [x * lax.rsqrt(seg_sum(x * x) + 1e-12) for x in kk]
    a_all = [-x for x in kk]
    b_all = [x * tile_of(a_ref, pi) for pi, x in zip(pairs, kk)]

    i64 = lax.broadcasted_iota(jnp.int32, (CHUNK, LANES), 0)
    j64 = lax.broadcasted_iota(jnp.int32, (CHUNK, LANES), 1) & (CHUNK - 1)
    strict_m = i64 > j64
    incl_m = i64 >= j64
    ltri = _tri_ones(CHUNK, True)
    eye128 = (lax.broadcasted_iota(jnp.int32, (LANES, LANES), 0)
              == lax.broadcasted_iota(jnp.int32, (LANES, LANES), 1)).astype(F32)
    bd = ((lax.broadcasted_iota(jnp.int32, (LANES, LANES), 0) >> 6)
          == (lax.broadcasted_iota(jnp.int32, (LANES, LANES), 1) >> 6))
    zeros64 = jnp.zeros((CHUNK, LANES), F32)
    st = lambda x: _stack2(x, m0, m1)

    nch = tc // CHUNK
    each = lambda f, *ls: [f(*xs) for xs in zip(*ls)]
    items = [(pi, slice(CHUNK * c, CHUNK * (c + 1))) for c in range(nch) for pi in pairs]
    lw = [lw_ref[0, rs, LANES * pi:LANES * (pi + 1)] for pi, rs in items]
    r, k, v = ([x[pi][rs] for pi, rs in items] for x in (r_all, k_all, v_all))
    av, bv = ([x[pi][rs] for pi, rs in items] for x in (a_all, b_all))
    cs = each(lambda x: _mm_exact_lhs(ltri, x), lw)
    cl = each(lambda x: x[CHUNK - 1:CHUNK, :], cs)
    a_t = each(lambda a_, c_, l_: a_ * jnp.exp(c_ - l_), av, cs, lw)
    r_t = each(lambda r_, c_: r_ * jnp.exp(c_), r, cs)
    inv_w = each(lambda c_: jnp.exp(-c_), cs)
    b_t = each(lambda x, w_: x * w_, bv, inv_w)
    k_t = each(lambda x, w_: x * w_, k, inv_w)
    to_end = each(lambda e_, c_: jnp.exp(e_ - c_), cl, cs)
    bkt = each(lambda b_, k_, e_: jnp.concatenate([b_ * e_, k_ * e_], axis=0).T, bv, k, to_end)
    ar = each(lambda a_, r_: jnp.concatenate([a_, r_], axis=0), a_t, r_t)
    gb = each(lambda x, y_: _mm(x, st(y_), nt=True), ar, b_t)
    gk = each(lambda x, y_: _mm(x, st(y_), nt=True), ar, k_t)
    l_ab = each(lambda x: jnp.where(strict_m, -x[0:CHUNK], 0.0), gb)
    l_ak = each(lambda x: jnp.where(strict_m, x[0:CHUNK], 0.0), gk)
    m_rb = each(lambda x: jnp.where(incl_m, x[CHUNK:2 * CHUNK], 0.0), gb)
    m_rk = each(lambda x: jnp.where(incl_m, x[CHUNK:2 * CHUNK], 0.0), gk)
    vst = each(st, v)
    lakv = each(_mm, l_ak, vst)
    tinv = _tri_inv_cat(l_ab, inv_passes)
    p_mat = each(lambda t_, a_: _mm(t_, st(a_)), tinv, a_t)
    q_mat = each(lambda t_, x: _mm(t_, st(x)), tinv, lakv)
    r_hat = each(lambda r_, m_, p_: r_ + _mm(m_, st(p_)), r_t, m_rb, p_mat)
    y_hat = each(lambda mb, q_, mk, vs: _mm(mb, st(q_)) + _mm(mk, vs), m_rb, q_mat, m_rk, vst)
    trans = each(lambda e_, bk, p_: eye128 * jnp.exp(e_)
                 + jnp.where(bd, _mm(bk, jnp.concatenate([p_, zeros64], axis=0)), 0.0), cl, bkt, p_mat)
    h_add = each(lambda bk, q_, v_: jnp.where(bd, _mm(bk, jnp.concatenate([q_, v_], axis=0)), 0.0), bkt, q_mat, v)
    hs = [state[pi] for pi in pairs]
    ys = [[] for _ in pairs]
    for idx, (pi, _) in enumerate(items):
        ys[pi].append(_mm(r_hat[idx], hs[pi]) + y_hat[idx])
        hs[pi] = _mm(trans[idx], hs[pi], passes=3) + h_add[idx]
    for pi in pairs:
        state[pi] = hs[pi]
        lanes = slice(LANES * pi, LANES * (pi + 1))
        y = jnp.concatenate(ys[pi], axis=0)
        mean = seg_sum(y) * (1.0 / 64)
        dlt = y - mean
        var = seg_sum(dlt * dlt) * (1.0 / 64)
        yn = dlt * lax.rsqrt(var + RWKV_GN_EPS) * lnw_ref[:, lanes] + lnb_ref[:, lanes]
        yn = yn + seg_sum(r_all[pi] * k_all[pi] * rk_ref[:, lanes]) * v_all[pi]
        o_ref[0, :, lanes] = (yn * tile_of(g_ref, pi)).astype(o_ref.dtype)


def wkv7(r, k, v, lw, kk, a, g, lnx_w, lnx_b, r_k, tc=512, npair=4, inv_passes=1):
    b, s, d = r.shape
    width = npair * LANES
    tile = pl.BlockSpec((1, tc, width), lambda bi, pi, ti: (bi, ti, pi))
    vec = pl.BlockSpec((1, width), lambda bi, pi, ti: (0, pi))
    return pl.pallas_call(
        functools.partial(_wkv_kernel, tc=tc, inv_passes=inv_passes),
        grid=(b, d // width, s // tc),
        in_specs=[tile] * 7 + [vec] * 3,
        out_specs=tile,
        out_shape=jax.ShapeDtypeStruct((b, s, d), BF16),
        scratch_shapes=[pltpu.VMEM((npair, LANES, LANES), F32)],
        compiler_params=_cparams(("parallel", "parallel", "arbitrary")), name="wkv7",
    )(r, k, v, lw, kk, a, g, lnx_w.reshape(1, d), lnx_b.reshape(1, d), r_k.reshape(1, d))


def _in_proj_weights(w_in):
    gw = GDN_HEADS * LANES
    c = 4 * gw
    w_qkv, w_z = w_in[:, 0:3 * gw], w_in[:, 3 * gw:c]
    d = w_in.shape[0]
    w_ba = jnp.zeros((d, LANES), w_in.dtype).at[:, :2 * GDN_HEADS].set(w_in[:, c:c + 2 * GDN_HEADS])
    c += 2 * GDN_HEADS
    nw = NSA_HEADS * NSA_DH
    w_q = w_in[:, c:c + nw].reshape(d, NSA_GROUPS, NSA_HPG, NSA_DH) * (NSA_DH ** -0.5)
    c += nw
    w_qpad = jnp.zeros((d, NSA_GROUPS, NSA_HPG, NSA_GROUPS, NSA_DH), w_in.dtype)
    for g in range(NSA_GROUPS):
        w_qpad = w_qpad.at[:, g, :, g, :].set(w_q[:, g])
    w_qpad = w_qpad.reshape(d, NSA_HEADS * LANES)
    kvw = NSA_GROUPS * NSA_DH
    w_cmp = w_in[:, c:c + 2 * kvw]
    w_kv4 = w_in[:, c + 2 * kvw:c + 6 * kvw]
    c += 6 * kvw
    w_gate = jnp.zeros((d, LANES), w_in.dtype).at[:, :3 * NSA_HEADS].set(w_in[:, c:c + 3 * NSA_HEADS])
    ws = [w_qkv, w_z, w_ba, w_qpad, w_cmp, w_kv4, w_gate]
    dts = [F32, F32, F32, BF16, F32, BF16, F32]
    return [w.astype(BF16) for w in ws], dts


def kernel(x, rel_bias, final_norm, e_attn_norm, e_w_in, e_conv_w, e_a_log, e_dt_bias, e_gdn_norm, e_cmpk_pe, e_cmpk_w1, e_cmpk_w2, e_cmpv_pe, e_cmpv_w1, e_cmpv_w2, e_w_out, e_ffn_norm, e_ffn_gate, e_ffn_up, e_ffn_down, o_attn_norm, o_mu, o_w_r, o_w_k, o_w_v, o_w_o, o_w0, o_w1, o_w2, o_a0, o_a1, o_a2, o_g1, o_g2, o_k_k, o_k_a, o_r_k, o_lnx_w, o_lnx_b, o_ffn_norm, o_ffn_gate, o_ffn_up, o_ffn_down):
    b, s, d = x.shape
    m = b * s
    xf = x.reshape(m, d)
    bf = lambda w: w.astype(BF16)

    ws, dts = _in_proj_weights(e_w_in[0])
    qkv, z, ba, qpad, kvc, kv4, gl = norm_linear(xf, e_attn_norm[0], ws, dts)
    sh = lambda t: t.reshape(b, s, t.shape[-1])
    o_gdn = gated_deltanet(sh(qkv), sh(z), sh(ba), e_conv_w[0], e_a_log[0], e_dt_bias[0], e_gdn_norm[0])
    kcvc = compress(sh(kvc), e_cmpk_pe[0], e_cmpk_w1[0], e_cmpk_w2[0], e_cmpv_pe[0], e_cmpv_w1[0], e_cmpv_w2[0])
    o_nsa = nsa_attention(sh(qpad), sh(gl), kcvc, sh(kv4), rel_bias)
    gw = GDN_HEADS * LANES
    xf = linear_res(xf, [o_gdn.reshape(m, -1), o_nsa.reshape(m, -1)], [bf(e_w_out[0][:gw]), bf(e_w_out[0][gw:])])
    xf = ffn(xf, e_ffn_norm[0], bf(e_ffn_gate[0]), bf(e_ffn_up[0]), bf(e_ffn_down[0]), final_norm, False)

    r, k, v, lw, kk, a, g = rwkv_proj(xf.reshape(b, s, d), o_attn_norm[0], o_mu[0], o_w_r[0], o_w_k[0], o_w_v[0],
                                      o_w0[0], o_w1[0], o_w2[0], o_a0[0], o_a1[0], o_a2[0], o_g1[0], o_g2[0],
                                      o_k_k[0], o_k_a[0])
    yg = wkv7(r, k, v, lw, kk, a, g, o_lnx_w[0], o_lnx_b[0], o_r_k[0].reshape(-1))
    xf = linear_res(xf, [yg.reshape(m, d)], [bf(o_w_o[0])])
    xf = ffn(xf, o_ffn_norm[0], bf(o_ffn_gate[0]), bf(o_ffn_up[0]), bf(o_ffn_down[0]), final_norm, True)
    return xf.reshape(b, s, d)
```

```python
import functools
import math

import jax
import jax.numpy as jnp
from jax import lax
from jax.experimental import pallas as pl
from jax.experimental.pallas import tpu as pltpu

F32 = jnp.float32
BF16 = jnp.bfloat16

VMEM_LIMIT_BYTES = 52 * 1024 * 1024
LANES = 128
CHUNK = 64

RMS_EPS = 1e-6
GDN_HEADS = 4
GDN_CONV = 4
NSA_HEADS = 8
NSA_GROUPS = 2
NSA_HPG = 4
NSA_DH = 64
CMP_LEN = 32
CMP_STRIDE = 16
SLC_LEN = 64
N_SELECT = 8
WINDOW = 512
NUM_BUCKETS = 32
REL_MAX_DIST = 128
NEG = -1e30
KV_PAD = 640
WIN_TILE = 640
RWKV_GN_EPS = 64e-5
BIAS_FAR_DIST = math.ceil((NUM_BUCKETS // 2) * (REL_MAX_DIST / (NUM_BUCKETS // 2))
                          ** ((NUM_BUCKETS // 2 - 1) / (NUM_BUCKETS // 2)))
CMP_TILE_ORIGIN = 64
CMP_PAST_BLOCKS = -(-(BIAS_FAR_DIST + CMP_LEN - 1) // CMP_STRIDE)
CMP_FUTURE_BLOCK = (SLC_LEN - CMP_LEN) // CMP_STRIDE + 1


def _cparams(sem):
    return pltpu.CompilerParams(dimension_semantics=sem, vmem_limit_bytes=VMEM_LIMIT_BYTES)


def _dot(a, b):
    return jnp.dot(a, b, preferred_element_type=F32)


def _dot_nt(a, b):
    return lax.dot_general(a, b, (((1,), (1,)), ((), ())), preferred_element_type=F32)


def _split(x, terms):
    out = []
    rem = x
    for _ in range(terms - 1):
        p = rem.astype(BF16)
        out.append(p)
        rem = rem - p.astype(F32)
    out.append(rem.astype(BF16))
    return out


def _mm(a, b, passes=1, nt=False):
    d = _dot_nt if nt else _dot
    if passes == 1:
        return d(a.astype(BF16), b.astype(BF16))
    ah, al = _split(a, 2)
    bh, bl = _split(b, 2)
    return d(ah, bh) + (d(ah, bl) + d(al, bh))


def _mm_exact_lhs(a01, b):
    a = a01.astype(BF16)
    b0, b1, b2 = _split(b, 3)
    return _dot(a, b0) + (_dot(a, b1) + _dot(a, b2))


def _mm_exact_rhs(a, b01):
    b = b01.astype(BF16)
    a0, a1, a2 = _split(a, 3)
    return _dot(a0, b) + (_dot(a1, b) + _dot(a2, b))


def _silu(x):
    return x * jax.nn.sigmoid(x)


def _softplus(x):
    return jnp.maximum(x, 0.0) + jnp.log1p(jnp.exp(-jnp.abs(x)))


def _lane_masks():
    lane = lax.broadcasted_iota(jnp.int32, (1, LANES), 1)
    m0 = (lane < 64).astype(F32)
    return m0, 1.0 - m0


def _stack2(x, m0, m1):
    return jnp.concatenate([x * m0, x * m1], axis=0)


def _tri_inv_cat(lcats, passes):
    m0, m1 = _lane_masks()
    i = lax.broadcasted_iota(jnp.int32, (CHUNK, LANES), 0)
    j = lax.broadcasted_iota(jnp.int32, (CHUNK, LANES), 1) & (CHUNK - 1)
    eye = (i == j).astype(F32)
    blk16 = (i >> 4) == (j >> 4)
    st = lambda x: _stack2(x, m0, m1)
    mm = lambda a, b: _mm(a, b, passes)
    each = lambda f, *ls: [f(*xs) for xs in zip(*ls)]
    ld = each(lambda l: jnp.where(blk16, l, 0.0), lcats)
    off = each(lambda l: jnp.where(blk16, 0.0, l), lcats)
    sld = each(st, ld)
    l2 = each(mm, ld, sld)
    sl2 = each(st, l2)
    n1 = each(lambda d, l, s: ((eye - d) + l) - mm(d, s), ld, l2, sl2)
    l4 = each(mm, l2, sl2)
    sl4 = each(st, l4)
    n2 = each(lambda a, s: a + mm(a, s), n1, sl4)
    l8 = each(mm, l4, sl4)
    x = each(lambda a, l: a + mm(a, st(l)), n2, l8)
    sx = each(st, x)
    mt = each(lambda a, o: mm(a, st(o)), x, off)
    m2 = each(lambda a: mm(a, st(a)), mt)
    mx = each(mm, mt, sx)
    m2x = each(mm, m2, sx)
    m3x = each(lambda a, b: mm(a, st(b)), m2, mx)
    return each(lambda a, b, c, d: (a - b) + (c - d), x, mx, m2x, m3x)


def _tri_ones(n, lower, block=None):
    i = lax.broadcasted_iota(jnp.int32, (n, n), 0)
    j = lax.broadcasted_iota(jnp.int32, (n, n), 1)
    m = (i >= j) if lower else (i <= j)
    if block is not None:
        sh = block.bit_length() - 1
        m = jnp.logical_and(m, (i >> sh) == (j >> sh))
    return m.astype(F32)


def _norm_linear_kernel(x_ref, nw_ref, *refs, n_out):
    w_refs, o_refs = refs[:n_out], refs[n_out:]
    x = x_ref[...]
    h = x * lax.rsqrt(jnp.mean(x * x, axis=-1, keepdims=True) + RMS_EPS) * nw_ref[...]
    hb = h.astype(BF16)
    for w_ref, o_ref in zip(w_refs, o_refs):
        o_ref[...] = _dot(hb, w_ref[...]).astype(o_ref.dtype)


def norm_linear(x, nw, weights, out_dtypes, tm=512):
    m, k = x.shape
    n_out = len(weights)
    in_specs = [pl.BlockSpec((tm, k), lambda i: (i, 0)), pl.BlockSpec((1, k), lambda i: (0, 0))]
    in_specs += [pl.BlockSpec(w.shape, lambda i: (0, 0)) for w in weights]
    out_specs = [pl.BlockSpec((tm, w.shape[1]), lambda i: (i, 0)) for w in weights]
    out_shape = [jax.ShapeDtypeStruct((m, w.shape[1]), dt) for w, dt in zip(weights, out_dtypes)]
    return pl.pallas_call(
        functools.partial(_norm_linear_kernel, n_out=n_out),
        grid=(m // tm,), in_specs=in_specs, out_specs=out_specs, out_shape=out_shape,
        compiler_params=_cparams(("parallel",)), name="norm_linear",
    )(x, nw.reshape(1, k), *weights)


def _linear_res_kernel(res_ref, *refs, n_in):
    a_refs, w_refs, o_ref = refs[:n_in], refs[n_in:2 * n_in], refs[2 * n_in]
    acc = res_ref[...]
    for a_ref, w_ref in zip(a_refs, w_refs):
        acc = acc + _dot(a_ref[...].astype(BF16), w_ref[...])
    o_ref[...] = acc


def linear_res(res, acts, weights, tm=512):
    m, n = res.shape
    n_in = len(acts)
    in_specs = [pl.BlockSpec((tm, n), lambda i: (i, 0))]
    in_specs += [pl.BlockSpec((tm, a.shape[1]), lambda i: (i, 0)) for a in acts]
    in_specs += [pl.BlockSpec(w.shape, lambda i: (0, 0)) for w in weights]
    return pl.pallas_call(
        functools.partial(_linear_res_kernel, n_in=n_in),
        grid=(m // tm,), in_specs=in_specs, out_specs=pl.BlockSpec((tm, n), lambda i: (i, 0)),
        out_shape=jax.ShapeDtypeStruct((m, n), F32),
        compiler_params=_cparams(("parallel",)), name="linear_res",
    )(res, *acts, *weights)


def _ffn_kernel(x_ref, nw_ref, wg_ref, wu_ref, wd_ref, fw_ref, o_ref, xn_scr, acc_scr, *, final_norm):
    f = pl.program_id(1)

    @pl.when(f == 0)
    def _():
        x = x_ref[...]
        h = x * lax.rsqrt(jnp.mean(x * x, axis=-1, keepdims=True) + RMS_EPS) * nw_ref[...]
        xn_scr[...] = h.astype(BF16)
        acc_scr[...] = jnp.zeros_like(acc_scr)

    xn = xn_scr[...]
    g = _dot(xn, wg_ref[...])
    u = _dot(xn, wu_ref[...])
    hid = (_silu(g) * u).astype(BF16)
    acc_scr[...] += _dot(hid, wd_ref[...])

    @pl.when(f == pl.num_programs(1) - 1)
    def _():
        y = x_ref[...] + acc_scr[...]
        if final_norm:
            y = y * lax.rsqrt(jnp.mean(y * y, axis=-1, keepdims=True) + RMS_EPS) * fw_ref[...]
        o_ref[...] = y


def ffn(x, nw, wg, wu, wd, fw, final_norm, tm=512, tf=1408):
    m, d = x.shape
    dff = wg.shape[1]
    return pl.pallas_call(
        functools.partial(_ffn_kernel, final_norm=final_norm),
        grid=(m // tm, dff // tf),
        in_specs=[pl.BlockSpec((tm, d), lambda i, f: (i, 0)),
                  pl.BlockSpec((1, d), lambda i, f: (0, 0)),
                  pl.BlockSpec((d, tf), lambda i, f: (0, f)),
                  pl.BlockSpec((d, tf), lambda i, f: (0, f)),
                  pl.BlockSpec((tf, d), lambda i, f: (f, 0)),
                  pl.BlockSpec((1, d), lambda i, f: (0, 0))],
        out_specs=pl.BlockSpec((tm, d), lambda i, f: (i, 0)),
        out_shape=jax.ShapeDtypeStruct((m, d), F32),
        scratch_shapes=[pltpu.VMEM((tm, d), BF16), pltpu.VMEM((tm, d), F32)],
        compiler_params=_cparams(("parallel", "arbitrary")), name="ffn",
    )(x, nw.reshape(1, d), wg, wu, wd, fw.reshape(1, d))


def _gdn_kernel(q_ref, k_ref, v_ref, qh_ref, kh_ref, vh_ref, cwq_ref, cwk_ref, cwv_ref,
                z_ref, ba_ref, bat_ref, hp_ref, nw_ref, o_ref, scr, state, *, tc, inv_passes):
    h = pl.program_id(1)
    t = pl.program_id(2)

    @pl.when(t == 0)
    def _():
        state[...] = jnp.zeros_like(state)

    not_first = (t > 0).astype(F32)

    def conv(x_ref, halo_ref, w_ref, slot):
        scr[slot, 0:8, :] = halo_ref[0] * not_first
        scr[slot, 8:8 + tc, :] = x_ref[0]
        w = w_ref[...]
        y = (w[3:4] * scr[slot, 8:8 + tc, :] + w[2:3] * scr[slot, 7:7 + tc, :]
             + w[1:2] * scr[slot, 6:6 + tc, :] + w[0:1] * scr[slot, 5:5 + tc, :])
        return _silu(y)

    q = conv(q_ref, qh_ref, cwq_ref, 0)
    k = conv(k_ref, kh_ref, cwk_ref, 1)
    v = conv(v_ref, vh_ref, cwv_ref, 2)
    qn = q * lax.rsqrt(jnp.sum(q * q, axis=-1, keepdims=True) + 1e-6)
    kn = k * lax.rsqrt(jnp.sum(k * k, axis=-1, keepdims=True) + 1e-6)

    lane = lax.broadcasted_iota(jnp.int32, (1, LANES), 1)
    ba = ba_ref[0]
    b_col = jnp.sum(jnp.where(lane == h, ba, 0.0), axis=-1, keepdims=True)
    a_col = jnp.sum(jnp.where(lane == h + GDN_HEADS, ba, 0.0), axis=-1, keepdims=True)
    beta = jax.nn.sigmoid(b_col)
    a_log = hp_ref[0, 0:1, :]
    dt_b = hp_ref[0, 1:2, :]
    g_rep = -jnp.exp(a_log) * _softplus(a_col + dt_b)

    m0, m1 = _lane_masks()
    i64 = lax.broadcasted_iota(jnp.int32, (CHUNK, LANES), 0)
    j64 = lax.broadcasted_iota(jnp.int32, (CHUNK, LANES), 1) & (CHUNK - 1)
    low_half = lax.broadcasted_iota(jnp.int32, (CHUNK, LANES), 1) < CHUNK
    ltri_bd = _tri_ones(2 * CHUNK, True, CHUNK)
    utri_bd = _tri_ones(2 * CHUNK, False, CHUNK)
    eye128 = (lax.broadcasted_iota(jnp.int32, (LANES, LANES), 0)
              == lax.broadcasted_iota(jnp.int32, (LANES, LANES), 1)).astype(F32)
    zeros64 = jnp.zeros((CHUNK, LANES), F32)
    scale = 1.0 / math.sqrt(LANES)
    nw = nw_ref[...]

    npair = tc // (2 * CHUNK)
    each = lambda f, *ls: [f(*xs) for xs in zip(*ls)]
    halves = lambda x: jnp.where(low_half, x[0:CHUNK], x[CHUNK:2 * CHUNK])
    prow = [slice(2 * CHUNK * p, 2 * CHUNK * (p + 1)) for p in range(npair)]
    kp = [kn[rs] for rs in prow]
    qp = [qn[rs] * scale for rs in prow]
    vp = [v[rs] for rs in prow]
    beta_p = [beta[rs] for rs in prow]
    gc_col = [_mm_exact_lhs(ltri_bd, g_rep[rs]) for rs in prow]
    g_row = [-jnp.exp(a_log[:, 0:1]) * _softplus(bat_ref[0, 0, p, 1:2, :] + dt_b[:, 0:1]) for p in range(npair)]
    gc_row = each(lambda x: _mm_exact_rhs(jnp.broadcast_to(x, (CHUNK, LANES)), utri_bd), g_row)
    dcat = each(lambda c_, r_: jnp.exp(jnp.where(i64 >= j64, halves(c_) - r_, -jnp.inf)), gc_col, gc_row)
    kb = each(lambda k_, b_: k_ * b_, kp, beta_p)
    gram = each(lambda x, y_: _mm(x, y_, nt=True), kb, kp)
    gq = each(lambda x, y_: _mm(x, y_, nt=True), qp, kp)
    strict = each(lambda x, d_: jnp.where(i64 > j64, halves(x) * d_, 0.0), gram, dcat)
    qk_bd = each(lambda x, d_: _stack2(jnp.where(i64 >= j64, halves(x) * d_, 0.0), m0, m1), gq, dcat)
    eg = each(jnp.exp, gc_col)
    g_last = each(lambda x: [x[CHUNK * (c + 1) - 1:CHUNK * (c + 1), :] for c in range(2)], gc_col)
    kdt = each(lambda k_, gl_, gc_: jnp.concatenate(
        [k_[CHUNK * c:CHUNK * (c + 1)] * jnp.exp(gl_[c] - gc_[CHUNK * c:CHUNK * (c + 1)]) for c in range(2)],
        axis=0).T, kp, g_last, gc_col)
    tinv = _tri_inv_cat(strict, inv_passes)
    uw = each(lambda t_, v_, b_, kb_, e_: _mm(_stack2(t_, m0, m1), jnp.concatenate([v_ * b_, kb_ * e_], axis=1)),
              tinv, vp, beta_p, kb, eg)
    u = each(lambda x: x[:, 0:LANES], uw)
    w = each(lambda x: x[:, LANES:2 * LANES], uw)
    q_hat = each(lambda q_, e_, m_, w_: q_ * e_ - _mm(m_, w_), qp, eg, qk_bd, w)
    o_hat = each(_mm, qk_bd, u)
    pad = lambda x, c: jnp.concatenate([x[0:CHUNK], zeros64] if c == 0 else [zeros64, x[CHUNK:2 * CHUNK]], axis=0)
    trans = [eye128 * jnp.exp(g_last[p][c]) - _mm(kdt[p], pad(w[p], c)) for p in range(npair) for c in range(2)]
    h_add = [_mm(kdt[p], pad(u[p], c)) for p in range(npair) for c in range(2)]
    s_state = state[...]
    for p in range(npair):
        for c in range(2):
            cr = slice(CHUNK * c, CHUNK * (c + 1))
            o = _mm(q_hat[p][cr], s_state) + o_hat[p][cr]
            s_state = _mm(trans[2 * p + c], s_state, passes=3) + h_add[2 * p + c]
            on = o * lax.rsqrt(jnp.mean(o * o, axis=-1, keepdims=True) + RMS_EPS) * nw
            r0 = 2 * CHUNK * p + CHUNK * c
            o_ref[0, r0:r0 + CHUNK, :] = (on * _silu(z_ref[0, r0:r0 + CHUNK, :])).astype(o_ref.dtype)
    state[...] = s_state


def gated_deltanet(qkv, z, ba, conv_w, a_log, dt_bias, norm_w, tc=1024, inv_passes=1):
    b, s, _ = qkv.shape
    hh = GDN_HEADS
    bat = ba[..., :2 * hh].reshape(b, s // 128, 128, 2, hh).transpose(0, 4, 1, 3, 2)
    hp = jnp.zeros((hh, 8, LANES), F32)
    hp = hp.at[:, 0, :].set(a_log.astype(F32)[:, None]).at[:, 1, :].set(dt_bias.astype(F32)[:, None])
    cw = jnp.zeros((8, 3 * hh * LANES), F32).at[:GDN_CONV].set(conv_w.astype(F32))
    hb = tc // 8
    blk = lambda off: pl.BlockSpec((1, tc, LANES), lambda bi, hi, ti: (bi, ti, hi + off))
    halo = lambda off: pl.BlockSpec((1, 8, LANES), lambda bi, hi, ti: (bi, jnp.maximum(ti * hb - 1, 0), hi + off))
    cws = lambda off: pl.BlockSpec((8, LANES), lambda bi, hi, ti: (0, hi + off))
    return pl.pallas_call(
        functools.partial(_gdn_kernel, tc=tc, inv_passes=inv_passes),
        grid=(b, hh, s // tc),
        in_specs=[blk(0), blk(hh), blk(2 * hh), halo(0), halo(hh), halo(2 * hh), cws(0), cws(hh), cws(2 * hh),
                  blk(0),
                  pl.BlockSpec((1, tc, LANES), lambda bi, hi, ti: (bi, ti, 0)),
                  pl.BlockSpec((1, 1, tc // 128, 2, LANES), lambda bi, hi, ti: (bi, hi, ti, 0, 0)),
                  pl.BlockSpec((1, 8, LANES), lambda bi, hi, ti: (hi, 0, 0)),
                  pl.BlockSpec((1, LANES), lambda bi, hi, ti: (0, 0))],
        out_specs=blk(0),
        out_shape=jax.ShapeDtypeStruct((b, s, hh * LANES), BF16),
        scratch_shapes=[pltpu.VMEM((3, tc + 8, LANES), F32), pltpu.VMEM((LANES, LANES), F32)],
        compiler_params=_cparams(("parallel", "parallel", "arbitrary")), name="gated_deltanet",
    )(qkv, qkv, qkv, qkv, qkv, qkv, cw, cw, cw, z, ba, bat, hp, norm_w.astype(F32).reshape(1, LANES))


def _compress_kernel(x_ref, pe_ref, w1_ref, w2_ref, o_ref):
    nb = x_ref.shape[3]
    acc = jnp.zeros((nb, LANES), F32)
    for g in range(NSA_GROUPS):
        x = x_ref[0, 0, g]
        first = _dot((x + pe_ref[0, 0]).astype(BF16), w1_ref[0, 0])
        second = _dot((x + pe_ref[0, 1]).astype(BF16), w1_ref[0, 1])
        hid = _silu(first + pltpu.roll(second, nb - 1, 0))
        acc = acc + _dot(hid.astype(BF16), w2_ref[0, g])
    o_ref[0, 0] = acc.astype(o_ref.dtype)


def compress(kvc, pe_k, w1_k, w2_k, pe_v, w1_v, w2_v):
    b, s, _ = kvc.shape
    nb = s // CMP_STRIDE
    width = CMP_STRIDE * NSA_DH
    x = kvc.reshape(b, nb, CMP_STRIDE, 2, NSA_GROUPS, NSA_DH).transpose(0, 3, 4, 1, 2, 5).reshape(b, 2, NSA_GROUPS, nb, width)
    pe = jnp.stack([pe_k, pe_v]).astype(F32).reshape(2, 2, 1, width)
    w1 = jnp.stack([w1_k, w1_v]).astype(BF16).reshape(2, 2, width, -1)
    hid = w1.shape[-1]
    w2 = jnp.stack([w2_k, w2_v]).astype(BF16)
    w2p = jnp.zeros((2, NSA_GROUPS, hid, LANES), BF16)
    for g in range(NSA_GROUPS):
        w2p = w2p.at[:, g, :, NSA_DH * g:NSA_DH * (g + 1)].set(w2)
    return pl.pallas_call(
        _compress_kernel,
        grid=(b, 2),
        in_specs=[pl.BlockSpec((1, 1, NSA_GROUPS, nb, width), lambda bi, j: (bi, j, 0, 0, 0)),
                  pl.BlockSpec((1, 2, 1, width), lambda bi, j: (j, 0, 0, 0)),
                  pl.BlockSpec((1, 2, width, hid), lambda bi, j: (j, 0, 0, 0)),
                  pl.BlockSpec((1, NSA_GROUPS, hid, LANES), lambda bi, j: (j, 0, 0, 0))],
        out_specs=pl.BlockSpec((1, 1, nb, LANES), lambda bi, j: (bi, j, 0, 0)),
        out_shape=jax.ShapeDtypeStruct((b, 2, nb, LANES), BF16),
        compiler_params=_cparams(("parallel", "parallel")), name="nsa_compress",
    )(x, pe, w1, w2p)


def _softmax_pv(s, v):
    m = jnp.max(s, axis=-1, keepdims=True)
    p = jnp.exp(s - m)
    l = jnp.sum(p, axis=-1, keepdims=True)
    return _dot(p.astype(BF16), v) / l


def _nsa_kernel(q_ref, gl_ref, kcvc_ref, kv_ref, bw_ref, bfar_ref, bc_ref, ovl_ref, o_ref):
    n = pl.program_id(1)
    qb_rows = q_ref.shape[1]
    ns = (kv_ref.shape[1] - KV_PAD) // SLC_LEN
    far_w = 8 * SLC_LEN
    lane = lax.broadcasted_iota(jnp.int32, (qb_rows, LANES), 1)
    lane_s = lane & (ns - 1)
    gates = jax.nn.sigmoid(gl_ref[0])
    kc = kcvc_ref[0, 0]
    vc = kcvc_ref[0, 1]
    ovl = ovl_ref[...]
    row_blk = lax.broadcasted_iota(jnp.int32, (LANES, far_w), 0)
    col_blk = lax.broadcasted_iota(jnp.int32, (LANES, far_w), 1) >> 6
    wcol = lax.broadcasted_iota(jnp.int32, (NSA_HPG * qb_rows, WIN_TILE), 1)
    groups = range(NSA_GROUPS)
    tile4 = lambda x: jnp.concatenate([x] * NSA_HPG, axis=0)

    qs = [jnp.concatenate([q_ref[0, :, LANES * (NSA_HPG * g + hh):LANES * (NSA_HPG * g + hh + 1)]
                           for hh in range(NSA_HPG)], axis=0) for g in groups]
    c0 = (qb_rows // CMP_STRIDE) * n
    lane4 = lax.broadcasted_iota(jnp.int32, (NSA_HPG * qb_rows, LANES), 1)
    shift = (c0 + (LANES - CMP_TILE_ORIGIN)) % LANES
    bc = [jnp.where(lane4 < c0 - (CMP_PAST_BLOCKS - 1), bfar_ref[g],
                    jnp.where(lane4 >= c0 + CMP_FUTURE_BLOCK, NEG, pltpu.roll(bc_ref[g], shift, 1))) for g in groups]
    start = pl.multiple_of(n * SLC_LEN + (KV_PAD - 3 * SLC_LEN), SLC_LEN)
    k_near = kv_ref[0, pl.ds(start, 4 * SLC_LEN), 0:LANES]
    v_near = kv_ref[0, pl.ds(start, 4 * SLC_LEN), LANES:2 * LANES]
    wst = pl.multiple_of(n * SLC_LEN + (KV_PAD + SLC_LEN - WIN_TILE), SLC_LEN)
    k_w = kv_ref[0, pl.ds(wst, WIN_TILE), 2 * LANES:3 * LANES]
    v_w = kv_ref[0, pl.ds(wst, WIN_TILE), 3 * LANES:4 * LANES]
    s_cmp = [_dot_nt(qs[g], kc) + bc[g] for g in groups]
    s_near = [_dot_nt(qs[g], k_near) + bw_ref[g, :, WIN_TILE - 4 * SLC_LEN:WIN_TILE] for g in groups]
    s_win = [jnp.where(wcol >= (WIN_TILE - SLC_LEN) - SLC_LEN * n, _dot_nt(qs[g], k_w) + bw_ref[g], NEG)
             for g in groups]

    p_cmp = []
    for g in groups:
        m = jnp.max(s_cmp[g], axis=-1, keepdims=True)
        e = jnp.where(bc[g] > 0.5 * NEG, jnp.exp(s_cmp[g] - m), 0.0)
        p_cmp.append(e / jnp.maximum(jnp.sum(e, axis=-1, keepdims=True), 1e-30))
    o_cmp = [_dot(p.astype(BF16), vc) for p in p_cmp]
    imp = [_mm_exact_rhs(p[0:qb_rows] + p[qb_rows:2 * qb_rows] + p[2 * qb_rows:3 * qb_rows] + p[3 * qb_rows:4 * qb_rows],
                         ovl) for p in p_cmp]

    forced = jnp.logical_or(jnp.logical_or(lane_s == 0, lane_s == n), lane_s == n - 1)
    val = [jnp.where(lane_s > n, -1.0, jnp.where(forced, 1e6, x)) for x in imp]
    rank = [jnp.zeros((qb_rows, LANES), F32) for _ in groups]
    for r in range(1, ns):
        tie = (lane_s >= r).astype(F32)
        for g in groups:
            other = pltpu.roll(val[g], r, 1)
            rank[g] = rank[g] + jnp.where(other > val[g], 1.0, jnp.where(other == val[g], tie, 0.0))
    sel = [jnp.where(jnp.logical_and(x < N_SELECT, lane < ns), 1.0, 0.0) for x in rank]

    o_win = [_softmax_pv(s_win[g], v_w) for g in groups]

    e_near = (lax.broadcasted_iota(jnp.int32, (LANES, 4 * SLC_LEN), 0)
              == (lax.broadcasted_iota(jnp.int32, (LANES, 4 * SLC_LEN), 1) >> 6) + (n - 3)).astype(BF16)
    carry = []
    for g in groups:
        sm = _dot(sel[g].astype(BF16), e_near)
        s = s_near[g] + tile4((sm - 1.0) * (-NEG))
        m = jnp.max(s, axis=-1, keepdims=True)
        p = jnp.exp(s - m)
        carry += [m, jnp.sum(p, axis=-1, keepdims=True), _dot(p.astype(BF16), v_near)]
    sel_far = [jnp.where(lane <= n - 4, x, 0.0).astype(BF16) for x in sel]
    b_far = [bfar_ref[g][:, 0:1] for g in groups]

    def far_body(it, carry):
        st = pl.multiple_of(KV_PAD + it * far_w, math.gcd(KV_PAD, far_w))
        k_c = kv_ref[0, pl.ds(st, far_w), 0:LANES]
        v_c = kv_ref[0, pl.ds(st, far_w), LANES:2 * LANES]
        e_c = (row_blk == col_blk + 8 * it).astype(BF16)
        sc = [_dot_nt(qs[g], k_c) for g in groups]
        smc = [_dot(sel_far[g], e_c) for g in groups]
        out = []
        for g in groups:
            m_i, l_i, acc_i = carry[3 * g:3 * g + 3]
            s_g = sc[g] + tile4((smc[g] - 1.0) * (-NEG))
            m_n = jnp.maximum(m_i, jnp.max(s_g, axis=-1, keepdims=True) + b_far[g])
            alpha = jnp.exp(m_i - m_n)
            pc = jnp.exp(s_g - (m_n - b_far[g]))
            out += [m_n, alpha * l_i + jnp.sum(pc, axis=-1, keepdims=True),
                    alpha * acc_i + _dot(pc.astype(BF16), v_c)]
        return tuple(out)

    carry = lax.fori_loop(0, (n // 4 + 1) // 2, far_body, tuple(carry))
    o_slc = [carry[3 * g + 2] / carry[3 * g + 1] for g in groups]

    for g in groups:
        pieces = []
        for hh in range(NSA_HPG):
            hd = NSA_HPG * g + hh
            rs = slice(qb_rows * hh, qb_rows * (hh + 1))
            pieces.append(gates[:, 3 * hd:3 * hd + 1] * o_cmp[g][rs] + gates[:, 3 * hd + 1:3 * hd + 2] * o_slc[g][rs]
                          + gates[:, 3 * hd + 2:3 * hd + 3] * o_win[g][rs])
        for hp in range(2):
            lo, hi = pieces[2 * hp], pieces[2 * hp + 1]
            if g == 0:
                hi = pltpu.roll(hi, NSA_DH, 1)
            else:
                lo = pltpu.roll(lo, NSA_DH, 1)
            o_ref[0, :, LANES * (2 * g + hp):LANES * (2 * g + hp + 1)] = jnp.where(lane < NSA_DH, lo, hi).astype(o_ref.dtype)


def _rel_bucket(dist):
    n = jnp.maximum(dist, 0)
    exact = NUM_BUCKETS // 2
    nf = jnp.maximum(n, 1).astype(F32)
    large = exact + (jnp.log(nf / exact) / math.log(REL_MAX_DIST / exact) * (NUM_BUCKETS - exact)).astype(jnp.int32)
    return jnp.where(n < exact, n, jnp.minimum(large, NUM_BUCKETS - 1))


def nsa_attention(qpad, gl, kcvc, kv4, rel_bias):
    b, s, _ = qpad.shape
    qb = SLC_LEN
    nq = s // qb
    nc = s // CMP_STRIDE - CMP_LEN // CMP_STRIDE + 1
    ns = s // SLC_LEN
    table = rel_bias.astype(F32)
    dist_w = jnp.arange(qb)[:, None] + (WIN_TILE - SLC_LEN) - jnp.arange(WIN_TILE)[None, :]
    ok_w = jnp.logical_and(dist_w >= 0, dist_w < WINDOW)
    bw = jnp.where(ok_w[None], table[_rel_bucket(dist_w)].transpose(2, 0, 1), NEG)
    bw = bw.reshape(NSA_GROUPS, NSA_HPG * qb, WIN_TILE)
    far_val = table[NUM_BUCKETS - 1]
    bfar = jnp.broadcast_to(far_val[:, None, None], (NSA_HEADS, qb, LANES)).reshape(NSA_GROUPS, NSA_HPG * qb, LANES)
    cmp_end = jnp.arange(LANES) * CMP_STRIDE + CMP_LEN - 1
    dist_c = jnp.arange(qb)[:, None] - CMP_STRIDE * (jnp.arange(LANES)[None, :] - CMP_TILE_ORIGIN) - (CMP_LEN - 1)
    bc = jnp.where((dist_c >= 0)[None], table[_rel_bucket(dist_c)].transpose(2, 0, 1), NEG)
    bc = bc.reshape(NSA_GROUPS, NSA_HPG * qb, LANES)
    cmp_start = jnp.arange(LANES) * CMP_STRIDE
    slc_j = jnp.arange(LANES) % ns
    ovl = jnp.logical_and(jnp.logical_and(cmp_start[:, None] < (slc_j[None, :] + 1) * SLC_LEN,
                                          cmp_end[:, None] >= slc_j[None, :] * SLC_LEN),
                          jnp.arange(LANES)[:, None] < nc).astype(F32)
    kvp = jnp.pad(kv4, ((0, 0), (KV_PAD, 0), (0, 0)))
    sp = s + KV_PAD
    return pl.pallas_call(
        _nsa_kernel,
        grid=(b, nq),
        in_specs=[pl.BlockSpec((1, qb, NSA_HEADS * LANES), lambda bi, n: (bi, n, 0)),
                  pl.BlockSpec((1, qb, LANES), lambda bi, n: (bi, n, 0)),
                  pl.BlockSpec((1, 2, s // CMP_STRIDE, LANES), lambda bi, n: (bi, 0, 0, 0)),
                  pl.BlockSpec((1, sp, 4 * LANES), lambda bi, n: (bi, 0, 0)),
                  pl.BlockSpec((NSA_GROUPS, NSA_HPG * qb, WIN_TILE), lambda bi, n: (0, 0, 0)),
                  pl.BlockSpec((NSA_GROUPS, NSA_HPG * qb, LANES), lambda bi, n: (0, 0, 0)),
                  pl.BlockSpec((NSA_GROUPS, NSA_HPG * qb, LANES), lambda bi, n: (0, 0, 0)),
                  pl.BlockSpec((LANES, LANES), lambda bi, n: (0, 0))],
        out_specs=pl.BlockSpec((1, qb, NSA_HEADS * NSA_DH), lambda bi, n: (bi, n, 0)),
        out_shape=jax.ShapeDtypeStruct((b, s, NSA_HEADS * NSA_DH), BF16),
        compiler_params=_cparams(("parallel", "arbitrary")), name="nsa_attention",
    )(qpad, gl, kcvc, kvp, bw, bfar, bc, ovl)


def _rwkv_proj_kernel(x_ref, halo_ref, nw_ref, mu_ref, vec_ref, wr_ref, wk_ref, wv_ref,
                      w1_ref, w2_ref, a1_ref, a2_ref, g1_ref, g2_ref,
                      r_ref, k_ref, v_ref, lw_ref, kk_ref, a_ref, g_ref, scr, *, tm):
    t = pl.program_id(1)
    nw = nw_ref[...]

    def norm(x):
        return x * lax.rsqrt(jnp.mean(x * x, axis=-1, keepdims=True) + RMS_EPS) * nw

    h = norm(x_ref[0])
    scr[0:8, :] = norm(halo_ref[0]) * (t > 0).astype(F32)
    scr[8:8 + tm, :] = h
    xx = scr[7:7 + tm, :] - h
    mix = lambda j: (h + xx * mu_ref[j:j + 1, :]).astype(BF16)
    w0, a0, k_k, k_a = vec_ref[0:1, :], vec_ref[1:2, :], vec_ref[2:3, :], vec_ref[3:4, :]
    r = _dot(mix(0), wr_ref[...])
    wl = w0 + _dot(jnp.tanh(_dot(mix(1), w1_ref[...])).astype(BF16), w2_ref[...])
    k = _dot(mix(2), wk_ref[...])
    v = _dot(mix(3), wv_ref[...])
    a = jax.nn.sigmoid(a0 + _dot(_dot(mix(4), a1_ref[...]).astype(BF16), a2_ref[...]))
    g = _dot(jax.nn.sigmoid(_dot(mix(5), g1_ref[...])).astype(BF16), g2_ref[...])
    w_raw = -_softplus(-wl) - 0.5
    r_ref[0] = r.astype(r_ref.dtype)
    k_ref[0] = (k * (1.0 + (a - 1.0) * k_a)).astype(k_ref.dtype)
    v_ref[0] = v.astype(v_ref.dtype)
    lw_ref[0] = -jnp.exp(w_raw)
    kk_ref[0] = (k * k_k).astype(kk_ref.dtype)
    a_ref[0] = a.astype(a_ref.dtype)
    g_ref[0] = g.astype(g_ref.dtype)


def rwkv_proj(x, nw, mu, w_r, w_k, w_v, w0, w1, w2, a0, a1, a2, g1, g2, k_k, k_a, tm=256):
    b, s, d = x.shape
    pad_to = lambda w, rows, cols: jnp.zeros((rows, cols), BF16).at[:w.shape[0], :w.shape[1]].set(w.astype(BF16))
    lora = lambda n: -(-n // LANES) * LANES
    mu8 = jnp.zeros((8, d), F32).at[:6].set(mu.astype(F32))
    vec = jnp.zeros((8, d), F32).at[0].set(w0).at[1].set(a0).at[2].set(k_k).at[3].set(k_a)
    big = [w.astype(BF16) for w in (w_r, w_k, w_v)]
    small = [pad_to(w1, d, lora(w1.shape[1])), pad_to(w2, lora(w2.shape[0]), d),
             pad_to(a1, d, lora(a1.shape[1])), pad_to(a2, lora(a2.shape[0]), d),
             pad_to(g1, d, lora(g1.shape[1])), pad_to(g2, lora(g2.shape[0]), d)]
    whole = lambda w: pl.BlockSpec(w.shape, lambda bi, ti: (0, 0))
    hb = tm // 8
    tile = pl.BlockSpec((1, tm, d), lambda bi, ti: (bi, ti, 0))
    return pl.pallas_call(
        functools.partial(_rwkv_proj_kernel, tm=tm),
        grid=(b, s // tm),
        in_specs=[tile, pl.BlockSpec((1, 8, d), lambda bi, ti: (bi, jnp.maximum(ti * hb - 1, 0), 0)),
                  pl.BlockSpec((1, d), lambda bi, ti: (0, 0)), whole(mu8), whole(vec)]
                 + [whole(w) for w in big] + [whole(w) for w in small],
        out_specs=[tile] * 7,
        out_shape=[jax.ShapeDtypeStruct((b, s, d), dt) for dt in (BF16, BF16, BF16, F32, BF16, BF16, BF16)],
        scratch_shapes=[pltpu.VMEM((tm + 8, d), F32)],
        compiler_params=_cparams(("parallel", "arbitrary")), name="rwkv_proj",
    )(x, x, nw.reshape(1, d), mu8, vec, *big, *small)


def _wkv_kernel(r_ref, k_ref, v_ref, lw_ref, kk_ref, a_ref, g_ref, lnw_ref, lnb_ref, rk_ref,
                o_ref, state, *, tc, inv_passes):
    t = pl.program_id(2)

    @pl.when(t == 0)
    def _():
        state[...] = jnp.zeros_like(state)

    m0, m1 = _lane_masks()
    lane = lax.broadcasted_iota(jnp.int32, (1, LANES), 1)
    low = lane < 64

    def seg_sum(x):
        return jnp.where(low, jnp.sum(x * m0, axis=-1, keepdims=True), jnp.sum(x * m1, axis=-1, keepdims=True))

    npair = r_ref.shape[2] // LANES
    pairs = range(npair)
    tile_of = lambda ref, pi: ref[0, :, LANES * pi:LANES * (pi + 1)].astype(F32)
    r_all, k_all, v_all = ([tile_of(ref, pi) for pi in pairs] for ref in (r_ref, k_ref, v_ref))
    kk = [tile_of(kk_ref, pi) for pi in pairs]
    kk = [x * lax.rsqrt(seg_sum(x * x) + 1e-12) for x in kk]
    a_all = [-x for x in kk]
    b_all = [x * tile_of(a_ref, pi) for pi, x in zip(pairs, kk)]

    i64 = lax.broadcasted_iota(jnp.int32, (CHUNK, LANES), 0)
    j64 = lax.broadcasted_iota(jnp.int32, (CHUNK, LANES), 1) & (CHUNK - 1)
    strict_m = i64 > j64
    incl_m = i64 >= j64
    ltri = _tri_ones(CHUNK, True)
    eye128 = (lax.broadcasted_iota(jnp.int32, (LANES, LANES), 0)
              == lax.broadcasted_iota(jnp.int32, (LANES, LANES), 1)).astype(F32)
    bd = ((lax.broadcasted_iota(jnp.int32, (LANES, LANES), 0) >> 6)
          == (lax.broadcasted_iota(jnp.int32, (LANES, LANES), 1) >> 6))
    zeros64 = jnp.zeros((CHUNK, LANES), F32)
    st = lambda x: _stack2(x, m0, m1)

    nch = tc // CHUNK
    each = lambda f, *ls: [f(*xs) for xs in zip(*ls)]
    items = [(pi, slice(CHUNK * c, CHUNK * (c + 1))) for c in range(nch) for pi in pairs]
    lw = [lw_ref[0, rs, LANES * pi:LANES * (pi + 1)] for pi, rs in items]
    r, k, v = ([x[pi][rs] for pi, rs in items] for x in (r_all, k_all, v_all))
    av, bv = ([x[pi][rs] for pi, rs in items] for x in (a_all, b_all))
    cs = each(lambda x: _mm_exact_lhs(ltri, x), lw)
    cl = each(lambda x: x[CHUNK - 1:CHUNK, :], cs)
    a_t = each(lambda a_, c_, l_: a_ * jnp.exp(c_ - l_), av, cs, lw)
    r_t = each(lambda r_, c_: r_ * jnp.exp(c_), r, cs)
    inv_w = each(lambda c_: jnp.exp(-c_), cs)
    b_t = each(lambda x, w_: x * w_, bv, inv_w)
    k_t = each(lambda x, w_: x * w_, k, inv_w)
    to_end = each(lambda e_, c_: jnp.exp(e_ - c_), cl, cs)
    bkt = each(lambda b_, k_, e_: jnp.concatenate([b_ * e_, k_ * e_], axis=0).T, bv, k, to_end)
    ar = each(lambda a_, r_: jnp.concatenate([a_, r_], axis=0), a_t, r_t)
    gb = each(lambda x, y_: _mm(x, st(y_), nt=True), ar, b_t)
    gk = each(lambda x, y_: _mm(x, st(y_), nt=True), ar, k_t)
    l_ab = each(lambda x: jnp.where(strict_m, -x[0:CHUNK], 0.0), gb)
    l_ak = each(lambda x: jnp.where(strict_m, x[0:CHUNK], 0.0), gk)
    m_rb = each(lambda x: jnp.where(incl_m, x[CHUNK:2 * CHUNK], 0.0), gb)
    m_rk = each(lambda x: jnp.where(incl_m, x[CHUNK:2 * CHUNK], 0.0), gk)
    vst = each(st, v)
    lakv = each(_mm, l_ak, vst)
    tinv = _tri_inv_cat(l_ab, inv_passes)
    p_mat = each(lambda t_, a_: _mm(t_, st(a_)), tinv, a_t)
    q_mat = each(lambda t_, x: _mm(t_, st(x)), tinv, lakv)
    r_hat = each(lambda r_, m_, p_: r_ + _mm(m_, st(p_)), r_t, m_rb, p_mat)
    y_hat = each(lambda mb, q_, mk, vs: _mm(mb, st(q_)) + _mm(mk, vs), m_rb, q_mat, m_rk, vst)
    trans = each(lambda e_, bk, p_: eye128 * jnp.exp(e_)
                 + jnp.where(bd, _mm(bk, jnp.concatenate([p_, zeros64], axis=0)), 0.0), cl, bkt, p_mat)
    h_add = each(lambda bk, q_, v_: jnp.where(bd, _mm(bk, jnp.concatenate([q_, v_], axis=0)), 0.0), bkt, q_mat, v)
    hs = [state[pi] for pi in pairs]
    ys = [[] for _ in pairs]
    for idx, (pi, _) in enumerate(items):
        ys[pi].append(_mm(r_hat[idx], hs[pi]) + y_hat[idx])
        hs[pi] = _mm(trans[idx], hs[pi], passes=3) + h_add[idx]
    for pi in pairs:
        state[pi] = hs[pi]
        lanes = slice(LANES * pi, LANES * (pi + 1))
        y = jnp.concatenate(ys[pi], axis=0)
        mean = seg_sum(y) * (1.0 / 64)
        dlt = y - mean
        var = seg_sum(dlt * dlt) * (1.0 / 64)
        yn = dlt * lax.rsqrt(var + RWKV_GN_EPS) * lnw_ref[:, lanes] + lnb_ref[:, lanes]
        yn = yn + seg_sum(r_all[pi] * k_all[pi] * rk_ref[:, lanes]) * v_all[pi]
        o_ref[0, :, lanes] = (yn * tile_of(g_ref, pi)).astype(o_ref.dtype)


def wkv7(r, k, v, lw, kk, a, g, lnx_w, lnx_b, r_k, tc=512, npair=4, inv_passes=1):
    b, s, d = r.shape
    width = npair * LANES
    tile = pl.BlockSpec((1, tc, width), lambda bi, pi, ti: (bi, ti, pi))
    vec = pl.BlockSpec((1, width), lambda bi, pi, ti: (0, pi))
    return pl.pallas_call(
        functools.partial(_wkv_kernel, tc=tc, inv_passes=inv_passes),
        grid=(b, d // width, s // tc),
        in_specs=[tile] * 7 + [vec] * 3,
        out_specs=tile,
        out_shape=jax.ShapeDtypeStruct((b, s, d), BF16),
        scratch_shapes=[pltpu.VMEM((npair, LANES, LANES), F32)],
        compiler_params=_cparams(("parallel", "parallel", "arbitrary")), name="wkv7",
    )(r, k, v, lw, kk, a, g, lnx_w.reshape(1, d), lnx_b.reshape(1, d), r_k.reshape(1, d))


def _in_proj_weights(w_in):
    gw = GDN_HEADS * LANES
    c = 4 * gw
    w_qkv, w_z = w_in[:, 0:3 * gw], w_in[:, 3 * gw:c]
    d = w_in.shape[0]
    w_ba = jnp.zeros((d, LANES), w_in.dtype).at[:, :2 * GDN_HEADS].set(w_in[:, c:c + 2 * GDN_HEADS])
    c += 2 * GDN_HEADS
    nw = NSA_HEADS * NSA_DH
    w_q = w_in[:, c:c + nw].reshape(d, NSA_GROUPS, NSA_HPG, NSA_DH) * (NSA_DH ** -0.5)
    c += nw
    w_qpad = jnp.zeros((d, NSA_GROUPS, NSA_HPG, NSA_GROUPS, NSA_DH), w_in.dtype)
    for g in range(NSA_GROUPS):
        w_qpad = w_qpad.at[:, g, :, g, :].set(w_q[:, g])
    w_qpad = w_qpad.reshape(d, NSA_HEADS * LANES)
    kvw = NSA_GROUPS * NSA_DH
    w_cmp = w_in[:, c:c + 2 * kvw]
    w_kv4 = w_in[:, c + 2 * kvw:c + 6 * kvw]
    c += 6 * kvw
    w_gate = jnp.zeros((d, LANES), w_in.dtype).at[:, :3 * NSA_HEADS].set(w_in[:, c:c + 3 * NSA_HEADS])
    ws = [w_qkv, w_z, w_ba, w_qpad, w_cmp, w_kv4, w_gate]
    dts = [F32, F32, F32, BF16, F32, BF16, F32]
    return [w.astype(BF16) for w in ws], dts


def kernel(x, rel_bias, final_norm, e_attn_norm, e_w_in, e_conv_w, e_a_log, e_dt_bias, e_gdn_norm, e_cmpk_pe, e_cmpk_w1, e_cmpk_w2, e_cmpv_pe, e_cmpv_w1, e_cmpv_w2, e_w_out, e_ffn_norm, e_ffn_gate, e_ffn_up, e_ffn_down, o_attn_norm, o_mu, o_w_r, o_w_k, o_w_v, o_w_o, o_w0, o_w1, o_w2, o_a0, o_a1, o_a2, o_g1, o_g2, o_k_k, o_k_a, o_r_k, o_lnx_w, o_lnx_b, o_ffn_norm, o_ffn_gate, o_ffn_up, o_ffn_down):
    b, s, d = x.shape
    m = b * s
    xf = x.reshape(m, d)
    bf = lambda w: w.astype(BF16)

    ws, dts = _in_proj_weights(e_w_in[0])
    qkv, z, ba, qpad, kvc, kv4, gl = norm_linear(xf, e_attn_norm[0], ws, dts)
    sh = lambda t: t.reshape(b, s, t.shape[-1])
    o_gdn = gated_deltanet(sh(qkv), sh(z), sh(ba), e_conv_w[0], e_a_log[0], e_dt_bias[0], e_gdn_norm[0])
    kcvc = compress(sh(kvc), e_cmpk_pe[0], e_cmpk_w1[0], e_cmpk_w2[0], e_cmpv_pe[0], e_cmpv_w1[0], e_cmpv_w2[0])
    o_nsa = nsa_attention(sh(qpad), sh(gl), kcvc, sh(kv4), rel_bias)
    gw = GDN_HEADS * LANES
    xf = linear_res(xf, [o_gdn.reshape(m, -1), o_nsa.reshape(m, -1)], [bf(e_w_out[0][:gw]), bf(e_w_out[0][gw:])])
    xf = ffn(xf, e_ffn_norm[0], bf(e_ffn_gate[0]), bf(e_ffn_up[0]), bf(e_ffn_down[0]), final_norm, False)

    r, k, v, lw, kk, a, g = rwkv_proj(xf.reshape(b, s, d), o_attn_norm[0], o_mu[0], o_w_r[0], o_w_k[0], o_w_v[0],
                                      o_w0[0], o_w1[0], o_w2[0], o_a0[0], o_a1[0], o_a2[0], o_g1[0], o_g2[0],
                                      o_k_k[0], o_k_a[0])
    yg = wkv7(r, k, v, lw, kk, a, g, o_lnx_w[0], o_lnx_b[0], o_r_k[0].reshape(-1))
    xf = linear_res(xf, [yg.reshape(m, d)], [bf(o_w_o[0])])
    xf = ffn(xf, o_ffn_norm[0], bf(o_ffn_gate[0]), bf(o_ffn_up[0]), bf(o_ffn_down[0]), final_norm, True)
    return xf.reshape(b, s, d)
```
